```python
import math
import jax, jax.numpy as jnp
from jax import lax
import numpy as np

D_MODEL = 2048
BATCH = 2
SEQ = 4096
DEPTH = 1
DEC_BATCH = 128
DEC_SEQ = 8
PAST_LEN = 2048
PAGE_SIZE = 128

ATT_GROUPS = ((128, 1), (512, 4), (2048, 16))
N_ATT_GROUPS = 3
ATT_HEADS = 4
ATT_HEAD_DIM = 128
ATT_WIDTH = N_ATT_GROUPS * ATT_HEADS * ATT_HEAD_DIM
ATT_OUT = ATT_HEADS * ATT_HEAD_DIM
ROPE_THETA = 10000.0
Q_BLOCK = 128
HGRN_KDIM = 128
HGRN_HEADS = D_MODEL // HGRN_KDIM
HGRN_VDIM = D_MODEL // HGRN_HEADS
HGRN_WIDTH = HGRN_HEADS * HGRN_KDIM
HGRN_VWIDTH = HGRN_HEADS * HGRN_VDIM
HGRN_CHUNK = 64
MEM_LEN = 256
CROSS_HEADS = 4
CROSS_HEAD_DIM = 128
CROSS_WIDTH = CROSS_HEADS * CROSS_HEAD_DIM
N_GROUPS = 4
EXPERTS_PER_GROUP = 8
N_EXPERTS = N_GROUPS * EXPERTS_PER_GROUP
EXPERT_FF = 512
TOP_K_INNER = 2
EPS = 1e-6
IN_SIZES = (ATT_WIDTH, ATT_WIDTH, ATT_WIDTH, HGRN_WIDTH, HGRN_WIDTH, HGRN_VWIDTH, HGRN_VWIDTH, D_MODEL, D_MODEL)
IN_WIDTH = 3 * ATT_WIDTH + 2 * HGRN_WIDTH + 2 * HGRN_VWIDTH + 2 * D_MODEL

kernel_name = 'hybrid_dilated_hgrn2_hmoe_step'


def _split_points(sizes):
    pts, acc = [], 0
    for s in sizes[:-1]:
        acc += s
        pts.append(acc)
    return pts


def _rmsnorm(x, g):
    xf = x.astype(jnp.float32)
    y = xf * lax.rsqrt(jnp.mean(xf * xf, axis=-1, keepdims=True) + EPS)
    return (y * g.astype(jnp.float32)).astype(x.dtype)


def _rope(x, pos):
    half = x.shape[-1] // 2
    inv_freq = ROPE_THETA ** (-jnp.arange(half, dtype=jnp.float32) / half)
    ang = pos.astype(jnp.float32)[:, None] * inv_freq[None, :]
    ang = ang.reshape((1, pos.shape[0]) + (1,) * (x.ndim - 3) + (half,))
    cos, sin = jnp.cos(ang), jnp.sin(ang)
    xf = x.astype(jnp.float32)
    x1, x2 = xf[..., :half], xf[..., half:]
    return jnp.concatenate([x1 * cos - x2 * sin, x2 * cos + x1 * sin], axis=-1).astype(x.dtype)


def _dilated_attend(q, k_all, v_all, q_off, window, dil):
    B, Tq, H, Dh = q.shape
    Tk = k_all.shape[1]
    n_taps = window // dil + 1
    qb = min(Q_BLOCK, Tq)
    nb = -(-Tq // qb)
    qp = jnp.pad(q, ((0, 0), (0, nb * qb - Tq), (0, 0), (0, 0)))
    q_blocks = qp.reshape(B, nb, qb, H, Dh).swapaxes(0, 1)
    starts = jnp.arange(nb, dtype=jnp.int32) * qb
    taps = jnp.arange(n_taps, dtype=jnp.int32) * dil
    scale = Dh ** -0.5

    def one_block(args):
        qblk, start = args
        kpos = q_off + start + jnp.arange(qb, dtype=jnp.int32)[:, None] - taps[None, :]
        valid = kpos >= 0
        idx = jnp.clip(kpos, 0, Tk - 1)
        kg = jnp.take(k_all, idx, axis=1).astype(jnp.float32)
        vg = jnp.take(v_all, idx, axis=1).astype(jnp.float32)
        s = jnp.einsum('bqhd,bqnhd->bqhn', qblk.astype(jnp.float32), kg) * scale
        s = jnp.where(valid[None, :, None, :], s, -jnp.inf)
        lse = jax.nn.logsumexp(s, axis=-1)
        p = jnp.exp(s - lse[..., None])
        o = jnp.einsum('bqhn,bqnhd->bqhd', p, vg)
        return o, lse

    o, lse = lax.map(one_block, (q_blocks, starts))
    o = o.swapaxes(0, 1).reshape(B, nb * qb, H, Dh)[:, :Tq]
    lse = lse.swapaxes(0, 1).reshape(B, nb * qb, H)[:, :Tq]
    return o, lse


def _hgrn2_scan(q, k, v, log_f, S0):
    B, T, H, Dk = q.shape
    C = min(HGRN_CHUNK, T)
    n = -(-T // C)
    pad = n * C - T

    def prep(a):
        a = jnp.pad(a.astype(jnp.float32), ((0, 0), (0, pad), (0, 0), (0, 0)))
        return a.reshape(B, n, C, H, a.shape[-1]).transpose(1, 0, 3, 2, 4)

    causal = jnp.tril(jnp.ones((C, C), dtype=bool))

    def step(S, inp):
        qc, kc, vc, gc = inp
        b = jnp.cumsum(gc, axis=2)
        o_inter = jnp.einsum('bhtk,bhkv->bhtv', qc * jnp.exp(b), S)
        rel = jnp.where(causal[None, None, :, :, None], b[:, :, :, None, :] - b[:, :, None, :, :], -jnp.inf)
        a = jnp.einsum('bhtk,bhsk,bhtsk->bhts', qc, kc, jnp.exp(rel))
        o = o_inter + jnp.einsum('bhts,bhsv->bhtv', a, vc)
        b_last = b[:, :, -1:, :]
        S_new = S * jnp.exp(b_last[:, :, 0, :])[..., None] + jnp.einsum('bhsk,bhsv->bhkv', kc * jnp.exp(b_last - b), vc)
        return S_new, o

    S_T, o = lax.scan(step, S0.astype(jnp.float32), (prep(q), prep(k), prep(v), prep(log_f)))
    o = o.transpose(1, 0, 3, 2, 4).reshape(B, n * C, H, v.shape[-1])[:, :T]
    return o, S_T


def _cross_attend(xn, mem_kv, w_cq, w_co):
    B, T, _ = xn.shape
    q = (xn @ w_cq).reshape(B, T, CROSS_HEADS, CROSS_HEAD_DIM).astype(jnp.float32)
    kv = mem_kv.astype(jnp.float32)
    s = jnp.einsum('bqhd,bmhd->bhqm', q, kv[:, :, 0]) * CROSS_HEAD_DIM ** -0.5
    p = jax.nn.softmax(s, axis=-1)
    o = jnp.einsum('bhqm,bmhd->bqhd', p, kv[:, :, 1]).reshape(B, T, CROSS_WIDTH).astype(xn.dtype)
    return o @ w_co


def _hier_moe(xn, w_rg, b_rg, w_re, b_re, w_e_gate, w_e_up, w_e_down):
    B, T, D = xn.shape
    xt = xn.reshape(B * T, D)
    pg = jax.nn.softmax((xt @ w_rg).astype(jnp.float32) + b_rg.astype(jnp.float32), axis=-1)
    p_top, g_idx = lax.top_k(pg, 1)
    le = ((xt @ w_re).astype(jnp.float32) + b_re.astype(jnp.float32)).reshape(-1, N_GROUPS, EXPERTS_PER_GROUP)
    le_sel = jnp.take_along_axis(le, g_idx[:, :, None], axis=1)[:, 0]
    v2, e_idx = lax.top_k(le_sel, TOP_K_INNER)
    w2 = jax.nn.softmax(v2, axis=-1) * p_top
    inner = jnp.sum(jax.nn.one_hot(e_idx, EXPERTS_PER_GROUP, dtype=jnp.float32) * w2[..., None], axis=1)
    cw = jax.nn.one_hot(g_idx[:, 0], N_GROUPS, dtype=jnp.float32)[:, :, None] * inner[:, None, :]
    y = jnp.zeros((B * T, D), jnp.float32)
    for g in range(N_GROUPS):
        h = jax.nn.silu(jnp.einsum('nd,edf->nef', xt, w_e_gate[g])) * jnp.einsum('nd,edf->nef', xt, w_e_up[g])
        h = (h.astype(jnp.float32) * cw[:, g, :, None]).astype(xt.dtype)
        y = y + jnp.einsum('nef,efd->nd', h, w_e_down[g]).astype(jnp.float32)
    return y.reshape(B, T, D).astype(xn.dtype)


def _layer(x, pos0, win_bufs, S0, mem_kv, layer, hgrn_lb_logits, norm_mix, w_in, w_proj_attn, w_proj_hgrn,
           w_out, hgrn_norm, norm_cross, w_cq, w_co, norm_ffn, w_rg, b_rg, w_re, b_re, w_e_gate, w_e_up, w_e_down):
    B, T, _ = x.shape
    dt = x.dtype
    pos = pos0 + jnp.arange(T, dtype=jnp.int32)
    xn = _rmsnorm(x, norm_mix)
    proj = xn @ w_in
    qa, ka, va, qh, fh, ih, ogh, ga, gh = jnp.split(proj, _split_points(IN_SIZES), axis=-1)

    att_shape = (B, T, N_ATT_GROUPS, ATT_HEADS, ATT_HEAD_DIM)
    qa = _rope(qa.reshape(att_shape), pos)
    ka = _rope(ka.reshape(att_shape), pos)
    va = va.reshape(att_shape)
    outs, lses, kv_rows = [], [], []
    for g, (window, dil) in enumerate(ATT_GROUPS):
        kg, vg = ka[:, :, g], va[:, :, g]
        kv_rows.append(jnp.stack([kg, vg], axis=2))
        if win_bufs is None:
            k_all, v_all, q_off = kg, vg, 0
        else:
            buf = win_bufs[g]
            k_all = jnp.concatenate([buf[:, :, 0].astype(dt), kg], axis=1)
            v_all = jnp.concatenate([buf[:, :, 1].astype(dt), vg], axis=1)
            q_off = buf.shape[1]
        o, lse = _dilated_attend(qa[:, :, g], k_all, v_all, q_off, window, dil)
        outs.append(o)
        lses.append(lse)
    o = jnp.stack(outs, axis=2)
    wgt = jax.nn.softmax(jnp.stack(lses, axis=2), axis=2)
    attn = jnp.sum(o * wgt[..., None], axis=2).reshape(B, T, ATT_OUT).astype(dt)

    lb_all = jnp.cumsum(jax.nn.softmax(hgrn_lb_logits.astype(jnp.float32), axis=0), axis=0)
    lb = lb_all[layer].reshape(HGRN_HEADS, HGRN_KDIM)
    f = lb + (1.0 - lb) * jax.nn.sigmoid(fh.astype(jnp.float32).reshape(B, T, HGRN_HEADS, HGRN_KDIM))
    qf = jax.nn.silu(qh.astype(jnp.float32)).reshape(B, T, HGRN_HEADS, HGRN_KDIM)
    o_h, S_T = _hgrn2_scan(qf, 1.0 - f, ih.reshape(B, T, HGRN_HEADS, HGRN_VDIM), jnp.log(f), S0)
    o_h = _rmsnorm(o_h, hgrn_norm) * jax.nn.silu(ogh.astype(jnp.float32).reshape(B, T, HGRN_HEADS, HGRN_VDIM))
    hgrn = o_h.reshape(B, T, HGRN_VWIDTH).astype(dt)

    merged = jax.nn.sigmoid(ga) * (attn @ w_proj_attn) + jax.nn.sigmoid(gh) * (hgrn @ w_proj_hgrn)
    x = x + merged @ w_out
    x = x + _cross_attend(_rmsnorm(x, norm_cross), mem_kv, w_cq, w_co)
    x = x + _hier_moe(_rmsnorm(x, norm_ffn), w_rg, b_rg, w_re, b_re, w_e_gate, w_e_up, w_e_down)
    return x, kv_rows, S_T


def setup_inputs(seed: int = 0) -> dict:
    key = jax.random.key(seed)
    ks = jax.random.split(key, 32)
    f32 = jnp.float32

    def nrm(k, shape, scale):
        return jax.random.normal(k, shape, f32) * scale

    def gain(k, shape):
        return 1.0 + 0.01 * jax.random.normal(k, shape, f32)

    D = D_MODEL
    return {
        'x_prompt': nrm(ks[0], (BATCH, SEQ, D), 1.0),
        'x_sample': nrm(ks[1], (DEC_BATCH, DEC_SEQ, D), 1.0),
        'cache_swa1': nrm(ks[2], (DEPTH, DEC_BATCH, min(ATT_GROUPS[0][0], PAST_LEN), 2, ATT_HEADS, ATT_HEAD_DIM), 1.0),
        'cache_swa2': nrm(ks[3], (DEPTH, DEC_BATCH, min(ATT_GROUPS[1][0], PAST_LEN), 2, ATT_HEADS, ATT_HEAD_DIM), 1.0),
        'cache_swa3': nrm(ks[4], (DEPTH, DEC_BATCH, min(ATT_GROUPS[2][0], PAST_LEN), 2, ATT_HEADS, ATT_HEAD_DIM), 1.0),
        'state_hgrn': nrm(ks[5], (DEPTH, DEC_BATCH, HGRN_HEADS, HGRN_KDIM, HGRN_VDIM), 0.3),
        'cache_mem_kv': nrm(ks[6], (DEPTH, DEC_BATCH, MEM_LEN, 2, CROSS_HEADS, CROSS_HEAD_DIM), 1.0),
        'mem_prompt': nrm(ks[7], (BATCH, MEM_LEN, D), 1.0),
        'hgrn_lb_logits': nrm(ks[8], (DEPTH + 1, HGRN_WIDTH), 0.5),
        'norm_mix': gain(ks[9], (DEPTH, D)),
        'w_in': nrm(ks[10], (DEPTH, D, IN_WIDTH), D ** -0.5),
        'w_proj_attn': nrm(ks[11], (DEPTH, ATT_OUT, D), ATT_OUT ** -0.5),
        'w_proj_hgrn': nrm(ks[12], (DEPTH, HGRN_VWIDTH, D), HGRN_VWIDTH ** -0.5),
        'w_out': nrm(ks[13], (DEPTH, D, D), D ** -0.5),
        'hgrn_norm': gain(ks[14], (DEPTH, HGRN_VDIM)),
        'norm_cross': gain(ks[15], (DEPTH, D)),
        'norm_mem': gain(ks[16], (DEPTH, D)),
        'w_cq': nrm(ks[17], (DEPTH, D, CROSS_WIDTH), D ** -0.5),
        'w_ckv': nrm(ks[18], (DEPTH, D, 2 * CROSS_WIDTH), D ** -0.5),
        'w_co': nrm(ks[19], (DEPTH, CROSS_WIDTH, D), CROSS_WIDTH ** -0.5),
        'norm_ffn': gain(ks[20], (DEPTH, D)),
        'w_rg': nrm(ks[21], (DEPTH, D, N_GROUPS), D ** -0.5),
        'b_rg': nrm(ks[22], (DEPTH, N_GROUPS), 0.01),
        'w_re': nrm(ks[23], (DEPTH, D, N_EXPERTS), D ** -0.5),
        'b_re': nrm(ks[24], (DEPTH, N_EXPERTS), 0.01),
        'w_e_gate': nrm(ks[25], (DEPTH, N_GROUPS, EXPERTS_PER_GROUP, D, EXPERT_FF), D ** -0.5),
        'w_e_up': nrm(ks[26], (DEPTH, N_GROUPS, EXPERTS_PER_GROUP, D, EXPERT_FF), D ** -0.5),
        'w_e_down': nrm(ks[27], (DEPTH, N_GROUPS, EXPERTS_PER_GROUP, EXPERT_FF, D), EXPERT_FF ** -0.5),
        'norm_final': gain(ks[28], (D,)),
    }


def reference(x_prompt, x_sample, cache_swa1, cache_swa2, cache_swa3, state_hgrn, cache_mem_kv, mem_prompt,
              hgrn_lb_logits, norm_mix, w_in, w_proj_attn, w_proj_hgrn, w_out, hgrn_norm, norm_cross, norm_mem,
              w_cq, w_ckv, w_co, norm_ffn, w_rg, b_rg, w_re, b_re, w_e_gate, w_e_up, w_e_down, norm_final):
    Bp = x_prompt.shape[0]
    h_p, h_s = x_prompt, x_sample
    swa_p = ([], [], [])
    swa_s = ([], [], [])
    hg_p, hg_s, memkv_p = [], [], []
    for l in range(DEPTH):
        lw = (norm_mix[l], w_in[l], w_proj_attn[l], w_proj_hgrn[l], w_out[l], hgrn_norm[l], norm_cross[l],
              w_cq[l], w_co[l], norm_ffn[l], w_rg[l], b_rg[l], w_re[l], b_re[l], w_e_gate[l], w_e_up[l], w_e_down[l])
        mkv = (_rmsnorm(mem_prompt, norm_mem[l]) @ w_ckv[l]).reshape(Bp, MEM_LEN, 2, CROSS_HEADS, CROSS_HEAD_DIM)
        S0 = jnp.zeros((Bp, HGRN_HEADS, HGRN_KDIM, HGRN_VDIM), jnp.float32)
        h_p, rows_p, S_p = _layer(h_p, 0, None, S0, mkv, l, hgrn_lb_logits, *lw)
        bufs = (cache_swa1[l], cache_swa2[l], cache_swa3[l])
        h_s, rows_s, S_s = _layer(h_s, PAST_LEN, bufs, state_hgrn[l], cache_mem_kv[l], l, hgrn_lb_logits, *lw)
        for g, (window, _) in enumerate(ATT_GROUPS):
            swa_p[g].append(rows_p[g][:, max(0, rows_p[g].shape[1] - window):])
            swa_s[g].append(rows_s[g])
        hg_p.append(S_p.astype(x_prompt.dtype))
        hg_s.append(S_s.astype(x_sample.dtype))
        memkv_p.append(mkv)
    y_prompt = _rmsnorm(h_p, norm_final)
    y_sample = _rmsnorm(h_s, norm_final)
    return (y_prompt, y_sample,
            jnp.stack(swa_p[0]), jnp.stack(swa_p[1]), jnp.stack(swa_p[2]), jnp.stack(hg_p), jnp.stack(memkv_p),
            jnp.stack(swa_s[0]), jnp.stack(swa_s[1]), jnp.stack(swa_s[2]), jnp.stack(hg_s))
```

```python
import functools

import jax
import jax.numpy as jnp
from jax import lax
from jax.experimental import pallas as pl
from jax.experimental.pallas import tpu as pltpu

F32 = jnp.float32
BF16 = jnp.bfloat16

ATT_GROUPS = ((128, 1), (512, 4), (2048, 16))
ATT_HEADS = 4
HEAD_DIM = 128
ATT_OUT = ATT_HEADS * HEAD_DIM
ATT_WIDTH = len(ATT_GROUPS) * ATT_OUT
KV_ROW = 2 * ATT_OUT
HGRN_HEADS = 16
ROPE_THETA = 10000.0
EPS = 1e-6
N_EXPERTS = 32
N_GROUPS = 4
EXPERTS_PER_GROUP = 8
NEG = -1e30
ATT_BLOCK = 128
HGRN_SUB = 8
VMEM_LIMIT = 56 * 1024 * 1024


def _params(*sem):
    return pltpu.CompilerParams(dimension_semantics=sem, vmem_limit_bytes=VMEM_LIMIT)


def _dot(a, b):
    return jnp.dot(a, b, preferred_element_type=F32)


def _dot_nt(a, b):
    return lax.dot_general(a, b, (((1,), (1,)), ((), ())), preferred_element_type=F32)


def _sigmoid(x):
    return 1.0 / (1.0 + jnp.exp(-x))


def _silu(x):
    return x * _sigmoid(x)


def _rmsnorm_body(x_ref, g_ref, o_ref):
    x = x_ref[...]
    ms = jnp.mean(x * x, axis=-1, keepdims=True)
    o_ref[...] = (x * lax.rsqrt(ms + EPS) * g_ref[...]).astype(o_ref.dtype)


def _rmsnorm(x, g, out_dtype, tm):
    m, d = x.shape
    return pl.pallas_call(
        _rmsnorm_body,
        grid=(m // tm,),
        in_specs=[pl.BlockSpec((tm, d), lambda i: (i, 0)), pl.BlockSpec((1, d), lambda i: (0, 0))],
        out_specs=pl.BlockSpec((tm, d), lambda i: (i, 0)),
        out_shape=jax.ShapeDtypeStruct((m, d), out_dtype),
        compiler_params=_params("arbitrary"),
        name="rmsnorm",
    )(x, g.reshape(1, d))


def _mm_body(*refs, has_res):
    if has_res:
        a_ref, w_ref, r_ref, o_ref, wb_ref = refs
    else:
        a_ref, w_ref, o_ref, wb_ref = refs

    @pl.when(pl.program_id(1) == 0)
    def _():
        wb_ref[...] = w_ref[...].astype(BF16)

    acc = _dot(a_ref[...].astype(BF16), wb_ref[...])
    if has_res:
        acc = acc + r_ref[...]
    o_ref[...] = acc.astype(o_ref.dtype)


def _matmul(a, w, out_dtype, tm, tn, res=None, name="matmul"):
    m, k = a.shape
    n = w.shape[1]
    in_specs = [pl.BlockSpec((tm, k), lambda j, i: (i, 0)), pl.BlockSpec((k, tn), lambda j, i: (0, j))]
    args = [a, w]
    if res is not None:
        in_specs.append(pl.BlockSpec((tm, tn), lambda j, i: (i, j)))
        args.append(res)
    return pl.pallas_call(
        functools.partial(_mm_body, has_res=res is not None),
        grid=(n // tn, m // tm),
        in_specs=in_specs,
        out_specs=pl.BlockSpec((tm, tn), lambda j, i: (i, j)),
        out_shape=jax.ShapeDtypeStruct((m, n), out_dtype),
        scratch_shapes=[pltpu.VMEM((k, tn), BF16)],
        compiler_params=_params("arbitrary", "arbitrary"),
        name=name,
    )(*args)


def _rope_body(q_ref, k_ref, v_ref, cos_ref, sin_ref, qo_ref, kv_ref):
    cos = cos_ref[...]
    sin = sin_ref[...]
    scale = HEAD_DIM ** -0.5
    for j in range(ATT_WIDTH // HEAD_DIM):
        sl = slice(j * HEAD_DIM, (j + 1) * HEAD_DIM)
        q = q_ref[:, sl]
        qo_ref[:, sl] = (q * cos + pltpu.roll(q, HEAD_DIM // 2, axis=1) * sin) * scale
        k = k_ref[:, sl]
        g, h = divmod(j, ATT_HEADS)
        kv_ref[:, g * KV_ROW + h * HEAD_DIM: g * KV_ROW + (h + 1) * HEAD_DIM] = (
            k * cos + pltpu.roll(k, HEAD_DIM // 2, axis=1) * sin)
    for g in range(len(ATT_GROUPS)):
        kv_ref[:, g * KV_ROW + ATT_OUT: (g + 1) * KV_ROW] = v_ref[:, g * ATT_OUT: (g + 1) * ATT_OUT]


def _rope(proj, cos2, sin2, tm):
    n = proj.shape[0]
    blk = lambda c: pl.BlockSpec((tm, ATT_WIDTH), lambda i, c=c: (i, c))
    tab = pl.BlockSpec((tm, HEAD_DIM), lambda i: (i, 0))
    return pl.pallas_call(
        _rope_body,
        grid=(n // tm,),
        in_specs=[blk(0), blk(1), blk(2), tab, tab],
        out_specs=[pl.BlockSpec((tm, ATT_WIDTH), lambda i: (i, 0)),
                   pl.BlockSpec((tm, len(ATT_GROUPS) * KV_ROW), lambda i: (i, 0))],
        out_shape=[jax.ShapeDtypeStruct((n, ATT_WIDTH), F32),
                   jax.ShapeDtypeStruct((n, len(ATT_GROUPS) * KV_ROW), F32)],
        compiler_params=_params("arbitrary"),
        name="rope",
    )(proj, proj, proj, cos2, sin2)


def _attn_prompt_body(q_ref, kp_ref, kc_ref, vp_ref, vc_ref, o_ref, lse_ref):
    i = pl.program_id(2)
    row = lax.broadcasted_iota(jnp.int32, (ATT_BLOCK, ATT_BLOCK), 0)
    col = lax.broadcasted_iota(jnp.int32, (ATT_BLOCK, ATT_BLOCK), 1)
    mask_p = jnp.logical_and(col >= row, i > 0)
    mask_c = col <= row
    for h in range(ATT_HEADS):
        sl = slice(h * HEAD_DIM, (h + 1) * HEAD_DIM)
        q = q_ref[:, sl].astype(BF16)
        s_p = jnp.where(mask_p, _dot_nt(q, kp_ref[:, sl].astype(BF16)), NEG)
        s_c = jnp.where(mask_c, _dot_nt(q, kc_ref[:, sl].astype(BF16)), NEG)
        m = jnp.maximum(jnp.max(s_p, axis=1, keepdims=True), jnp.max(s_c, axis=1, keepdims=True))
        p_p = jnp.where(mask_p, jnp.exp(s_p - m), 0.0)
        p_c = jnp.where(mask_c, jnp.exp(s_c - m), 0.0)
        l = jnp.sum(p_p, axis=1, keepdims=True) + jnp.sum(p_c, axis=1, keepdims=True)
        o = _dot(p_p.astype(BF16), vp_ref[:, sl].astype(BF16)) + _dot(p_c.astype(BF16), vc_ref[:, sl].astype(BF16))
        o_ref[:, sl] = o / l
        lse_ref[:, sl] = jnp.broadcast_to(m + jnp.log(l), (ATT_BLOCK, HEAD_DIM))


def _attn_prompt(q_rot, kv_all, g, batch, seq):
    _, d = ATT_GROUPS[g]
    n = q_rot.shape[0]
    ng = len(ATT_GROUPS)
    nblk = seq // d // ATT_BLOCK
    qv = q_rot.reshape(n // d, d * ATT_WIDTH)
    kvv = kv_all.reshape(n // d, d * ng * KV_ROW)
    blk = (ATT_BLOCK, ATT_OUT)
    prev = lambda i: jnp.maximum(i - 1, 0)
    in_specs = [
        pl.BlockSpec(blk, lambda b, r, i: (b * nblk + i, r * ng + g)),
        pl.BlockSpec(blk, lambda b, r, i: (b * nblk + prev(i), r * 2 * ng + 2 * g)),
        pl.BlockSpec(blk, lambda b, r, i: (b * nblk + i, r * 2 * ng + 2 * g)),
        pl.BlockSpec(blk, lambda b, r, i: (b * nblk + prev(i), r * 2 * ng + 2 * g + 1)),
        pl.BlockSpec(blk, lambda b, r, i: (b * nblk + i, r * 2 * ng + 2 * g + 1)),
    ]
    out_spec = pl.BlockSpec(blk, lambda b, r, i: (b * nblk + i, r))
    rows = batch * seq
    o, lse = pl.pallas_call(
        _attn_prompt_body,
        grid=(batch, d, nblk),
        in_specs=in_specs,
        out_specs=[out_spec, out_spec],
        out_shape=[jax.ShapeDtypeStruct((rows // d, d * ATT_OUT), F32)] * 2,
        compiler_params=_params("arbitrary", "arbitrary", "arbitrary"),
        name=f"attn_prompt_g{g}",
    )(qv, kvv, kvv, kvv, kvv)
    return o.reshape(rows, ATT_OUT), lse.reshape(rows, ATT_OUT)


def _merge_groups_body(o0, l0, o1, l1, o2, l2, out_ref):
    ls = (l0[...], l1[...], l2[...])
    mx = jnp.maximum(jnp.maximum(ls[0], ls[1]), ls[2])
    ws = [jnp.exp(l - mx) for l in ls]
    num = ws[0] * o0[...] + ws[1] * o1[...] + ws[2] * o2[...]
    out_ref[...] = num / (ws[0] + ws[1] + ws[2])


def _merge_groups(parts, tm):
    rows = parts[0].shape[0]
    spec = pl.BlockSpec((tm, ATT_OUT), lambda i: (i, 0))
    return pl.pallas_call(
        _merge_groups_body,
        grid=(rows // tm,),
        in_specs=[spec] * 6,
        out_specs=spec,
        out_shape=jax.ShapeDtypeStruct((rows, ATT_OUT), F32),
        compiler_params=_params("arbitrary"),
        name="merge_groups",
    )(*parts)


def _attn_sample_body(q_ref, kvn_ref, c1_ref, c2_ref, c3_ref, o_ref, *, tq):
    nrow = ATT_HEADS * tq
    nkey = ATT_BLOCK
    rid = lax.broadcasted_iota(jnp.int32, (nrow, 1), 0)
    i_row = rid % tq
    head_row = rid // tq
    lane_head = lax.broadcasted_iota(jnp.int32, (1, ATT_OUT), 1) // HEAD_DIM
    head_mask = head_row == lane_head
    key = lax.broadcasted_iota(jnp.int32, (1, nkey), 1)
    pad = jnp.zeros((nkey - tq, ATT_OUT), F32)
    caches = (c1_ref, c2_ref, c3_ref)
    outs, lses = [], []
    for g, (_, d) in enumerate(ATT_GROUPS):
        qg = q_ref[:, g * ATT_OUT:(g + 1) * ATT_OUT]
        qrows = jnp.where(head_mask, jnp.concatenate([qg] * ATT_HEADS, axis=0), 0.0).astype(BF16)
        k_new = jnp.concatenate([kvn_ref[:, g * KV_ROW: g * KV_ROW + ATT_OUT], pad], axis=0)
        v_new = jnp.concatenate([kvn_ref[:, g * KV_ROW + ATT_OUT: (g + 1) * KV_ROW], pad], axis=0)
        valid_new = jnp.logical_and(jnp.logical_and(key < tq, key <= i_row), (key % d) == (i_row % d))
        blocks = [(k_new, v_new, valid_new)]
        for r in range(min(d, tq)):
            kc = caches[g][:, r * KV_ROW: r * KV_ROW + ATT_OUT]
            vc = caches[g][:, r * KV_ROW + ATT_OUT: (r + 1) * KV_ROW]
            valid = jnp.logical_and((i_row % d) == r, key >= i_row // d)
            blocks.append((kc, vc, valid))
        m = jnp.full((nrow, 1), NEG, F32)
        l = jnp.zeros((nrow, 1), F32)
        acc = jnp.zeros((nrow, ATT_OUT), F32)
        for kb, vb, valid in blocks:
            s = jnp.where(valid, _dot_nt(qrows, kb.astype(BF16)), NEG)
            m_new = jnp.maximum(m, jnp.max(s, axis=1, keepdims=True))
            alpha = jnp.exp(m - m_new)
            p = jnp.where(valid, jnp.exp(s - m_new), 0.0)
            l = alpha * l + jnp.sum(p, axis=1, keepdims=True)
            acc = alpha * acc + _dot(p.astype(BF16), vb.astype(BF16))
            m = m_new
        o = acc / l
        lse = m + jnp.log(l)
        outs.append(jnp.concatenate(
            [o[h * tq:(h + 1) * tq, h * HEAD_DIM:(h + 1) * HEAD_DIM] for h in range(ATT_HEADS)], axis=1))
        lses.append(jnp.concatenate(
            [jnp.broadcast_to(lse[h * tq:(h + 1) * tq], (tq, HEAD_DIM)) for h in range(ATT_HEADS)], axis=1))
    mx = jnp.maximum(jnp.maximum(lses[0], lses[1]), lses[2])
    ws = [jnp.exp(x - mx) for x in lses]
    o_ref[...] = (ws[0] * outs[0] + ws[1] * outs[1] + ws[2] * outs[2]) / (ws[0] + ws[1] + ws[2])


def _attn_sample(q_rot, kv_all, caches, row0, batch, tq):
    ng = len(ATT_GROUPS)
    blk0 = row0 // tq
    views, specs = [], []
    for (w, d), c in zip(ATT_GROUPS, caches):
        assert c.shape[1] == w and w // d == ATT_BLOCK, "window buffers must hold exactly one window"
        views.append(c.reshape(batch, w // d, d * KV_ROW))
        specs.append(pl.BlockSpec((None, w // d, min(d, tq) * KV_ROW), lambda b: (b, 0, 0)))
    return pl.pallas_call(
        functools.partial(_attn_sample_body, tq=tq),
        grid=(batch,),
        in_specs=[pl.BlockSpec((tq, ATT_WIDTH), lambda b: (blk0 + b, 0)),
                  pl.BlockSpec((tq, ng * KV_ROW), lambda b: (blk0 + b, 0))] + specs,
        out_specs=pl.BlockSpec((tq, ATT_OUT), lambda b: (b, 0)),
        out_shape=jax.ShapeDtypeStruct((batch * tq, ATT_OUT), F32),
        compiler_params=_params("arbitrary"),
        name="attn_sample",
    )(q_rot, kv_all, *views)


def _cumsum_rows(x):
    c = x.shape[0]
    row = lax.broadcasted_iota(jnp.int32, (c, 1), 0)
    sh = 1
    while sh < c:
        x = x + jnp.where(row >= sh, pltpu.roll(x, sh, axis=0), 0.0)
        sh *= 2
    return x


def _bcast_rows(b, first, period):
    c, w = b.shape
    parts = [jnp.broadcast_to(b[p * period + first: p * period + first + 1, :], (period, w))
             for p in range(c // period)]
    return parts[0] if len(parts) == 1 else jnp.concatenate(parts, axis=0)


def _hgrn_body(*refs, chunk, has_s0):
    if has_s0:
        qh_ref, fh_ref, ih_ref, og_ref, lbl_ref, gn_ref, s0_ref, o_ref, so_ref, st_ref = refs
    else:
        qh_ref, fh_ref, ih_ref, og_ref, lbl_ref, gn_ref, o_ref, so_ref, st_ref = refs
    c = chunk
    nh = ATT_HEADS
    t = pl.program_id(2)

    @pl.when(t == 0)
    def _():
        for h in range(nh):
            if has_s0:
                st_ref[h] = s0_ref[h].T
            else:
                st_ref[h] = jnp.zeros((HEAD_DIM, HEAD_DIM), F32)

    lbl = lbl_ref[...]
    e = jnp.exp(lbl - jnp.max(lbl, axis=0, keepdims=True))
    lb = e[0:1, :] / jnp.sum(e, axis=0, keepdims=True)
    f = lb + (1.0 - lb) * _sigmoid(fh_ref[...])
    kk = 1.0 - f
    qf = _silu(qh_ref[...])
    b = _cumsum_rows(jnp.log(f))
    vv = ih_ref[...]

    row = lax.broadcasted_iota(jnp.int32, (c, c), 0)
    col = lax.broadcasted_iota(jnp.int32, (c, c), 1)
    rid = lax.broadcasted_iota(jnp.int32, (c, 1), 0)

    a = [jnp.zeros((c, c), F32) for _ in range(nh)]
    s = c // 2
    while s >= HGRN_SUB:
        ref = _bcast_rows(b, s - 1, 2 * s)
        second = (rid % (2 * s)) >= s
        dlt = b - ref
        ee = jnp.exp(jnp.where(second, dlt, -dlt))
        ql = jnp.where(second, qf * ee, 0.0).astype(BF16)
        kl = jnp.where(second, 0.0, kk * ee).astype(BF16)
        same = (row // (2 * s)) == (col // (2 * s))
        for h in range(nh):
            sl = slice(h * HEAD_DIM, (h + 1) * HEAD_DIM)
            a[h] = a[h] + jnp.where(same, _dot_nt(ql[:, sl], kl[:, sl]), 0.0)
        s //= 2
    sub = min(HGRN_SUB, c)
    ref = _bcast_rows(b, 0, sub)
    dlt = b - ref
    qd = (qf * jnp.exp(dlt)).astype(BF16)
    kd = (kk * jnp.exp(jnp.minimum(-dlt, 80.0))).astype(BF16)
    diag = jnp.logical_and((row // sub) == (col // sub), col <= row)
    qe = (qf * jnp.exp(b)).astype(BF16)
    b_last = b[c - 1:c, :]
    kend = (kk * jnp.exp(b_last - b)).astype(BF16)
    dec = jnp.exp(b_last)
    gn = gn_ref[...]
    og = og_ref[...]
    for h in range(nh):
        sl = slice(h * HEAD_DIM, (h + 1) * HEAD_DIM)
        ah = a[h] + jnp.where(diag, _dot_nt(qd[:, sl], kd[:, sl]), 0.0)
        v = vv[:, sl]
        vb = v.astype(BF16)
        st = st_ref[h]
        o = _dot_nt(qe[:, sl], st.astype(BF16)) + _dot(ah.astype(BF16), vb)
        st_new = st * dec[:, sl] + lax.dot_general(vb, kend[:, sl], (((0,), (0,)), ((), ())),
                                                   preferred_element_type=F32)
        st_ref[h] = st_new
        ms = jnp.mean(o * o, axis=-1, keepdims=True)
        o_ref[:, sl] = o * lax.rsqrt(ms + EPS) * gn * _silu(og[:, sl])

        @pl.when(t == pl.num_programs(2) - 1)
        def _(h=h, st_new=st_new):
            so_ref[h] = st_new.T


def _hgrn(proj, col0, lb_logits, gnorm, s0, row0, batch, seq, chunk):
    nt = seq // chunk
    hb = HGRN_HEADS // ATT_HEADS
    width = HGRN_HEADS * HEAD_DIM
    cb0 = col0 // ATT_OUT
    rb0 = row0 // chunk

    def col(j):
        return pl.BlockSpec((chunk, ATT_OUT), lambda b, h, t, j=j: (rb0 + b * nt + t, cb0 + j * hb + h))

    in_specs = [col(0), col(1), col(2), col(3),
                pl.BlockSpec((lb_logits.shape[0], ATT_OUT), lambda b, h, t: (0, h)),
                pl.BlockSpec((1, HEAD_DIM), lambda b, h, t: (0, 0))]
    args = [proj, proj, proj, proj, lb_logits, gnorm.reshape(1, HEAD_DIM)]
    state_spec = pl.BlockSpec((None, ATT_HEADS, HEAD_DIM, HEAD_DIM), lambda b, h, t: (b, h, 0, 0))
    if s0 is not None:
        in_specs.append(state_spec)
        args.append(s0)
    return pl.pallas_call(
        functools.partial(_hgrn_body, chunk=chunk, has_s0=s0 is not None),
        grid=(batch, hb, nt),
        in_specs=in_specs,
        out_specs=[pl.BlockSpec((chunk, ATT_OUT), lambda b, h, t: (b * nt + t, h)), state_spec],
        out_shape=[jax.ShapeDtypeStruct((batch * seq, width), F32),
                   jax.ShapeDtypeStruct((batch, HGRN_HEADS, HEAD_DIM, HEAD_DIM), F32)],
        scratch_shapes=[pltpu.VMEM((ATT_HEADS, HEAD_DIM, HEAD_DIM), F32)],
        compiler_params=_params("arbitrary", "arbitrary", "arbitrary"),
        name=f"hgrn_c{chunk}",
    )(*args)


def _gated_merge_body(att_ref, hg_ref, wa_ref, wh_ref, ga_ref, gh_ref, o_ref, wab_ref, whb_ref):
    @pl.when(pl.program_id(1) == 0)
    def _():
        wab_ref[...] = wa_ref[...].astype(BF16)
        whb_ref[...] = wh_ref[...].astype(BF16)

    pa = _dot(att_ref[...].astype(BF16), wab_ref[...])
    ph = _dot(hg_ref[...].astype(BF16), whb_ref[...])
    o_ref[...] = (_sigmoid(ga_ref[...]) * pa + _sigmoid(gh_ref[...]) * ph).astype(o_ref.dtype)


def _gated_merge(attn, hgrn, w_pa, w_ph, proj, col_ga, col_gh, tm, tn):
    m = attn.shape[0]
    n = w_pa.shape[1]
    ka, kh = w_pa.shape[0], w_ph.shape[0]
    ca, ch = col_ga // tn, col_gh // tn
    return pl.pallas_call(
        _gated_merge_body,
        grid=(n // tn, m // tm),
        in_specs=[pl.BlockSpec((tm, ka), lambda j, i: (i, 0)),
                  pl.BlockSpec((tm, kh), lambda j, i: (i, 0)),
                  pl.BlockSpec((ka, tn), lambda j, i: (0, j)),
                  pl.BlockSpec((kh, tn), lambda j, i: (0, j)),
                  pl.BlockSpec((tm, tn), lambda j, i: (i, ca + j)),
                  pl.BlockSpec((tm, tn), lambda j, i: (i, ch + j))],
        out_specs=pl.BlockSpec((tm, tn), lambda j, i: (i, j)),
        out_shape=jax.ShapeDtypeStruct((m, n), BF16),
        scratch_shapes=[pltpu.VMEM((ka, tn), BF16), pltpu.VMEM((kh, tn), BF16)],
        compiler_params=_params("arbitrary", "arbitrary"),
        name="gated_merge",
    )(attn, hgrn, w_pa, w_ph, proj, proj)


def _cross_body(q_ref, kv_ref, o_ref):
    scale = HEAD_DIM ** -0.5
    for h in range(ATT_HEADS):
        sl = slice(h * HEAD_DIM, (h + 1) * HEAD_DIM)
        q = (q_ref[:, sl] * scale).astype(BF16)
        k = kv_ref[:, sl].astype(BF16)
        v = kv_ref[:, ATT_OUT + h * HEAD_DIM: ATT_OUT + (h + 1) * HEAD_DIM].astype(BF16)
        s = _dot_nt(q, k)
        p = jnp.exp(s - jnp.max(s, axis=1, keepdims=True))
        o_ref[:, sl] = _dot(p.astype(BF16), v) / jnp.sum(p, axis=1, keepdims=True)


def _cross_attn(q, mem_kv, row0, batch, seq, tq):
    nq = seq // tq
    rb0 = row0 // tq
    mem_len = mem_kv.shape[1]
    return pl.pallas_call(
        _cross_body,
        grid=(batch, nq),
        in_specs=[pl.BlockSpec((tq, ATT_OUT), lambda b, i: (rb0 + b * nq + i, 0)),
                  pl.BlockSpec((None, mem_len, KV_ROW), lambda b, i: (b, 0, 0))],
        out_specs=pl.BlockSpec((tq, ATT_OUT), lambda b, i: (b * nq + i, 0)),
        out_shape=jax.ShapeDtypeStruct((batch * seq, ATT_OUT), F32),
        compiler_params=_params("arbitrary", "arbitrary"),
        name=f"cross_attn_t{tq}",
    )(q, mem_kv)


def _router_body(x_ref, g_ref, whi_ref, wlo_ref, b_ref, xn_ref, cw_ref):
    x = x_ref[...]
    ms = jnp.mean(x * x, axis=-1, keepdims=True)
    xn = x * lax.rsqrt(ms + EPS) * g_ref[...]
    hi = xn.astype(BF16)
    xn_ref[...] = hi
    lo = (xn - hi.astype(F32)).astype(BF16)
    logits = _dot(hi, whi_ref[...]) + _dot(lo, whi_ref[...]) + _dot(hi, wlo_ref[...]) + b_ref[...]
    lane = lax.broadcasted_iota(jnp.int32, logits.shape, 1)
    big = jnp.int32(1 << 20)
    is_g = jnp.logical_and(lane >= N_EXPERTS, lane < N_EXPERTS + N_GROUPS)
    lg = jnp.where(is_g, logits, NEG)
    mg = jnp.max(lg, axis=1, keepdims=True)
    p_top = 1.0 / jnp.sum(jnp.where(is_g, jnp.exp(lg - mg), 0.0), axis=1, keepdims=True)
    g_idx = jnp.min(jnp.where(jnp.logical_and(is_g, lg == mg), lane, big), axis=1, keepdims=True) - N_EXPERTS
    in_grp = jnp.logical_and(lane < N_EXPERTS, lane // EXPERTS_PER_GROUP == g_idx)
    le = jnp.where(in_grp, logits, NEG)
    v1 = jnp.max(le, axis=1, keepdims=True)
    i1 = jnp.min(jnp.where(jnp.logical_and(in_grp, le == v1), lane, big), axis=1, keepdims=True)
    rest = jnp.logical_and(in_grp, lane != i1)
    le2 = jnp.where(rest, logits, NEG)
    v2 = jnp.max(le2, axis=1, keepdims=True)
    i2 = jnp.min(jnp.where(jnp.logical_and(rest, le2 == v2), lane, big), axis=1, keepdims=True)
    e2 = jnp.exp(v2 - v1)
    w1 = p_top / (1.0 + e2)
    w2 = p_top * e2 / (1.0 + e2)
    cw_ref[...] = jnp.where(lane == i1, w1, 0.0) + jnp.where(lane == i2, w2, 0.0)


def _router(x, g, w_hi, w_lo, bias, tm):
    m, d = x.shape
    lanes = w_hi.shape[1]
    const = lambda shape: pl.BlockSpec(shape, lambda i: (0, 0))
    return pl.pallas_call(
        _router_body,
        grid=(m // tm,),
        in_specs=[pl.BlockSpec((tm, d), lambda i: (i, 0)), const((1, d)), const((d, lanes)), const((d, lanes)),
                  const((1, lanes))],
        out_specs=[pl.BlockSpec((tm, d), lambda i: (i, 0)), pl.BlockSpec((tm, lanes), lambda i: (i, 0))],
        out_shape=[jax.ShapeDtypeStruct((m, d), BF16), jax.ShapeDtypeStruct((m, lanes), F32)],
        compiler_params=_params("arbitrary"),
        name="router",
    )(x, g.reshape(1, d), w_hi, w_lo, bias)


def _moe_body(xn_ref, cw_ref, wg_ref, wu_ref, wd_ref, x_ref, gf_ref, y_ref, acc_ref):
    e = pl.program_id(1)

    @pl.when(e == 0)
    def _():
        acc_ref[...] = jnp.zeros_like(acc_ref)

    xn = xn_ref[...]
    h = _silu(_dot(xn, wg_ref[...])) * _dot(xn, wu_ref[...])
    cw = cw_ref[...]
    lane = lax.broadcasted_iota(jnp.int32, cw.shape, 1)
    w = jnp.sum(jnp.where(lane == e, cw, 0.0), axis=1, keepdims=True)
    acc_ref[...] += _dot((h * w).astype(BF16), wd_ref[...])

    @pl.when(e == pl.num_programs(1) - 1)
    def _():
        x = x_ref[...] + acc_ref[...]
        ms = jnp.mean(x * x, axis=-1, keepdims=True)
        y_ref[...] = x * lax.rsqrt(ms + EPS) * gf_ref[...]


def _moe(xn, cw, w_gate, w_up, w_down, x, g_final, tm):
    m, d = x.shape
    ne, _, ff = w_gate.shape
    lanes = cw.shape[1]
    row = lambda width: pl.BlockSpec((tm, width), lambda i, e: (i, 0))
    return pl.pallas_call(
        _moe_body,
        grid=(m // tm, ne),
        in_specs=[row(d), row(lanes),
                  pl.BlockSpec((None, d, ff), lambda i, e: (e, 0, 0)),
                  pl.BlockSpec((None, d, ff), lambda i, e: (e, 0, 0)),
                  pl.BlockSpec((None, ff, d), lambda i, e: (e, 0, 0)),
                  row(d), pl.BlockSpec((1, d), lambda i, e: (0, 0))],
        out_specs=row(d),
        out_shape=jax.ShapeDtypeStruct((m, d), F32),
        scratch_shapes=[pltpu.VMEM((tm, d), F32)],
        compiler_params=_params("arbitrary", "arbitrary"),
        name="moe",
    )(xn, cw, w_gate, w_up, w_down, x, g_final.reshape(1, d))


def _rope_tables(positions):
    half = HEAD_DIM // 2
    inv_freq = ROPE_THETA ** (-jnp.arange(half, dtype=F32) / half)
    ang = positions.astype(F32)[:, None] * inv_freq[None, :]
    cos, sin = jnp.cos(ang), jnp.sin(ang)
    return jnp.concatenate([cos, cos], axis=1), jnp.concatenate([-sin, sin], axis=1)


def kernel(x_prompt, x_sample, cache_swa1, cache_swa2, cache_swa3, state_hgrn, cache_mem_kv, mem_prompt,
           hgrn_lb_logits, norm_mix, w_in, w_proj_attn, w_proj_hgrn, w_out, hgrn_norm, norm_cross, norm_mem,
           w_cq, w_ckv, w_co, norm_ffn, w_rg, b_rg, w_re, b_re, w_e_gate, w_e_up, w_e_down, norm_final):
    bp, seq, d = x_prompt.shape
    bs, dseq, _ = x_sample.shape
    depth = w_in.shape[0]
    assert depth == 1, "single-layer trunk"
    past = cache_swa3.shape[2]
    mem_len = mem_prompt.shape[1]
    np_, ns = bp * seq, bs * dseq
    hw = HGRN_HEADS * HEAD_DIM
    col_hgrn = 3 * ATT_WIDTH
    col_ga = col_hgrn + 4 * hw
    col_gh = col_ga + d
    l = 0

    x_all = jnp.concatenate([x_prompt.reshape(np_, d), x_sample.reshape(ns, d)], axis=0)
    pos = jnp.concatenate([jnp.tile(jnp.arange(seq, dtype=jnp.int32), bp),
                           jnp.tile(past + jnp.arange(dseq, dtype=jnp.int32), bs)])
    cos2, sin2 = _rope_tables(pos)

    xn = _rmsnorm(x_all, norm_mix[l], BF16, 512)
    proj = _matmul(xn, w_in[l], F32, 512, 1536, name="proj_in")

    q_rot, kv_all = _rope(proj, cos2, sin2, 256)
    parts = []
    for g in range(len(ATT_GROUPS)):
        parts.extend(_attn_prompt(q_rot, kv_all, g, bp, seq))
    attn_p = _merge_groups(parts, 512)
    caches = (cache_swa1[l], cache_swa2[l], cache_swa3[l])
    attn_s = _attn_sample(q_rot, kv_all, [c.reshape(bs, c.shape[1], KV_ROW) for c in caches], np_, bs, dseq)
    attn = jnp.concatenate([attn_p, attn_s], axis=0)

    hg_p, st_p = _hgrn(proj, col_hgrn, hgrn_lb_logits, hgrn_norm[l], None, 0, bp, seq, 128)
    hg_s, st_s = _hgrn(proj, col_hgrn, hgrn_lb_logits, hgrn_norm[l], state_hgrn[l], np_, bs, dseq, dseq)
    hgrn = jnp.concatenate([hg_p, hg_s], axis=0)

    merged = _gated_merge(attn, hgrn, w_proj_attn[l], w_proj_hgrn[l], proj, col_ga, col_gh, 512, 512)
    x1 = _matmul(merged, w_out[l], F32, 512, 1024, res=x_all, name="proj_out")

    mem_n = _rmsnorm(mem_prompt.reshape(bp * mem_len, d), norm_mem[l], BF16, 256)
    mkv_p = _matmul(mem_n, w_ckv[l], F32, 256, KV_ROW, name="mem_kv")
    xc = _rmsnorm(x1, norm_cross[l], BF16, 512)
    qc = _matmul(xc, w_cq[l], F32, 512, ATT_OUT, name="cross_q")
    oc_p = _cross_attn(qc, mkv_p.reshape(bp, mem_len, KV_ROW), 0, bp, seq, 512)
    oc_s = _cross_attn(qc, cache_mem_kv[l].reshape(bs, mem_len, KV_ROW), np_, bs, dseq, dseq)
    oc = jnp.concatenate([oc_p, oc_s], axis=0)
    x2 = _matmul(oc, w_co[l], F32, 512, 1024, res=x1, name="cross_out")

    lanes = 128
    w_r = jnp.concatenate([w_re[l], w_rg[l], jnp.zeros((d, lanes - N_EXPERTS - N_GROUPS), F32)], axis=1)
    b_r = jnp.concatenate([b_re[l], b_rg[l], jnp.zeros((lanes - N_EXPERTS - N_GROUPS,), F32)]).reshape(1, lanes)
    w_r_hi = w_r.astype(BF16)
    w_r_lo = (w_r - w_r_hi.astype(F32)).astype(BF16)
    xf, cw = _router(x2, norm_ffn[l], w_r_hi, w_r_lo, b_r, 512)
    ff = w_e_gate.shape[-1]
    y = _moe(xf, cw,
             w_e_gate[l].reshape(N_EXPERTS, d, ff).astype(BF16),
             w_e_up[l].reshape(N_EXPERTS, d, ff).astype(BF16),
             w_e_down[l].reshape(N_EXPERTS, ff, d).astype(BF16),
             x2, norm_final, 512)

    ng = len(ATT_GROUPS)
    kv_p = kv_all[:np_].reshape(bp, seq, ng, 2, ATT_HEADS, HEAD_DIM)
    kv_s = kv_all[np_:].reshape(bs, dseq, ng, 2, ATT_HEADS, HEAD_DIM)
    swa_p = [kv_p[:, max(0, seq - w):, g][None] for g, (w, _) in enumerate(ATT_GROUPS)]
    swa_s = [kv_s[:, :, g][None] for g in range(ng)]
    return (y[:np_].reshape(bp, seq, d), y[np_:].reshape(bs, dseq, d),
            swa_p[0], swa_p[1], swa_p[2], st_p[None],
            mkv_p.reshape(1, bp, mem_len, 2, ATT_HEADS, HEAD_DIM),
            swa_s[0], swa_s[1], swa_s[2], st_s[None])
```

```python
import functools

import jax
import jax.numpy as jnp
from jax import lax
from jax.experimental import pallas as pl
from jax.experimental.pallas import tpu as pltpu

F32 = jnp.float32
BF16 = jnp.bfloat16

ATT_GROUPS = ((128, 1), (512, 4), (2048, 16))
ATT_HEADS = 4
HEAD_DIM = 128
ATT_OUT = ATT_HEADS * HEAD_DIM
ATT_WIDTH = len(ATT_GROUPS) * ATT_OUT
KV_ROW = 2 * ATT_OUT
KV_SLAB = 2 * ATT_HEADS
HGRN_HEADS = 16
ROPE_THETA = 10000.0
EPS = 1e-6
N_EXPERTS = 32
N_GROUPS = 4
EXPERTS_PER_GROUP = 8
NEG = -1e30
ATT_BLOCK = 128
HGRN_SUB = 8
VMEM_LIMIT = 56 * 1024 * 1024


def _params(*sem):
    return pltpu.CompilerParams(dimension_semantics=sem, vmem_limit_bytes=VMEM_LIMIT)


def _dot(a, b):
    return jnp.dot(a, b, preferred_element_type=F32)


def _dot_nt(a, b):
    return lax.dot_general(a, b, (((1,), (1,)), ((), ())), preferred_element_type=F32)


def _sigmoid(x):
    return 1.0 / (1.0 + jnp.exp(-x))


def _silu(x):
    return x * _sigmoid(x)


def _rmsnorm_body(x_ref, g_ref, o_ref):
    x = x_ref[...]
    ms = jnp.mean(x * x, axis=-1, keepdims=True)
    o_ref[...] = (x * lax.rsqrt(ms + EPS) * g_ref[...]).astype(o_ref.dtype)


def _rmsnorm(x, g, out_dtype, tm):
    m, d = x.shape
    return pl.pallas_call(
        _rmsnorm_body,
        grid=(m // tm,),
        in_specs=[pl.BlockSpec((tm, d), lambda i: (i, 0)), pl.BlockSpec((1, d), lambda i: (0, 0))],
        out_specs=pl.BlockSpec((tm, d), lambda i: (i, 0)),
        out_shape=jax.ShapeDtypeStruct((m, d), out_dtype),
        compiler_params=_params("arbitrary"),
        name="rmsnorm",
    )(x, g.reshape(1, d))


def _mm_body(*refs, has_res):
    if has_res:
        a_ref, w_ref, r_ref, o_ref, wb_ref = refs
    else:
        a_ref, w_ref, o_ref, wb_ref = refs

    @pl.when(pl.program_id(1) == 0)
    def _():
        wb_ref[...] = w_ref[...].astype(BF16)

    acc = _dot(a_ref[...].astype(BF16), wb_ref[...])
    if has_res:
        acc = acc + r_ref[...]
    o_ref[...] = acc.astype(o_ref.dtype)


def _matmul(a, w, out_dtype, tm, tn, res=None, name="matmul"):
    m, k = a.shape
    n = w.shape[1]
    in_specs = [pl.BlockSpec((tm, k), lambda j, i: (i, 0)), pl.BlockSpec((k, tn), lambda j, i: (0, j))]
    args = [a, w]
    if res is not None:
        in_specs.append(pl.BlockSpec((tm, tn), lambda j, i: (i, j)))
        args.append(res)
    return pl.pallas_call(
        functools.partial(_mm_body, has_res=res is not None),
        grid=(n // tn, m // tm),
        in_specs=in_specs,
        out_specs=pl.BlockSpec((tm, tn), lambda j, i: (i, j)),
        out_shape=jax.ShapeDtypeStruct((m, n), out_dtype),
        scratch_shapes=[pltpu.VMEM((k, tn), BF16)],
        compiler_params=_params("arbitrary", "arbitrary"),
        name=name,
    )(*args)


def _rope_body(q_ref, k_ref, v_ref, cos_ref, sin_ref, qo_ref, kv_ref):
    cos = cos_ref[...]
    sin = sin_ref[...]
    scale = HEAD_DIM ** -0.5
    for j in range(ATT_WIDTH // HEAD_DIM):
        sl = slice(j * HEAD_DIM, (j + 1) * HEAD_DIM)
        q = q_ref[:, sl]
        qo_ref[:, sl] = (q * cos + pltpu.roll(q, HEAD_DIM // 2, axis=1) * sin) * scale
        k = k_ref[:, sl]
        g, h = divmod(j, ATT_HEADS)
        kv_ref[:, g * KV_ROW + h * HEAD_DIM: g * KV_ROW + (h + 1) * HEAD_DIM] = (
            k * cos + pltpu.roll(k, HEAD_DIM // 2, axis=1) * sin)
    for g in range(len(ATT_GROUPS)):
        kv_ref[:, g * KV_ROW + ATT_OUT: (g + 1) * KV_ROW] = v_ref[:, g * ATT_OUT: (g + 1) * ATT_OUT]


def _rope(proj, cos2, sin2, tm):
    n = proj.shape[0]
    blk = lambda c: pl.BlockSpec((tm, ATT_WIDTH), lambda i, c=c: (i, c))
    tab = pl.BlockSpec((tm, HEAD_DIM), lambda i: (i, 0))
    return pl.pallas_call(
        _rope_body,
        grid=(n // tm,),
        in_specs=[blk(0), blk(1), blk(2), tab, tab],
        out_specs=[pl.BlockSpec((tm, ATT_WIDTH), lambda i: (i, 0)),
                   pl.BlockSpec((tm, len(ATT_GROUPS) * KV_ROW), lambda i: (i, 0))],
        out_shape=[jax.ShapeDtypeStruct((n, ATT_WIDTH), F32),
                   jax.ShapeDtypeStruct((n, len(ATT_GROUPS) * KV_ROW), F32)],
        compiler_params=_params("arbitrary"),
        name="rope",
    )(proj, proj, proj, cos2, sin2)


def _prompt_key_blocks(bq):
    table = []
    for g, (w, _) in enumerate(ATT_GROUPS):
        nback = -(-w // bq)
        table.extend((g, back) for back in range(nback, -1, -1))
    return tuple(table)


def _attn_prompt_body(q_ref, k_ref, v_ref, o_ref, m_scr, l_scr, acc_scr, *, bq, table):
    i = pl.program_id(1)
    j = pl.program_id(2)
    ng = len(ATT_GROUPS)

    @pl.when(j == 0)
    def _():
        m_scr[...] = jnp.full(m_scr.shape, NEG, F32)
        l_scr[...] = jnp.zeros(l_scr.shape, F32)
        acc_scr[...] = jnp.zeros(acc_scr.shape, F32)

    row = lax.broadcasted_iota(jnp.int32, (bq, bq), 0)
    col = lax.broadcasted_iota(jnp.int32, (bq, bq), 1)
    for g, (w, d) in enumerate(ATT_GROUPS):
        j0 = min(jj for jj, (gg, _) in enumerate(table) if gg == g)
        nback = max(back for gg, back in table if gg == g)
        back = nback - (j - j0)

        @pl.when(jnp.logical_and(jnp.logical_and(j >= j0, j <= j0 + nback), i >= back))
        def _(g=g, w=w, d=d, back=back):
            dist = back * bq + row - col
            valid = jnp.logical_and(jnp.logical_and(dist >= 0, dist <= w), (dist & (d - 1)) == 0)
            for h in range(ATT_HEADS):
                sl = slice(h * HEAD_DIM, (h + 1) * HEAD_DIM)
                q = q_ref[:, g * ATT_OUT + h * HEAD_DIM: g * ATT_OUT + (h + 1) * HEAD_DIM].astype(BF16)
                s = jnp.where(valid, _dot_nt(q, k_ref[:, sl].astype(BF16)), NEG)
                m_old = m_scr[g * ATT_HEADS + h]
                m_new = jnp.maximum(m_old, jnp.max(s, axis=1, keepdims=True))
                alpha = jnp.exp(m_old - m_new)
                p = jnp.where(valid, jnp.exp(s - m_new), 0.0)
                l_scr[g * ATT_HEADS + h] = alpha * l_scr[g * ATT_HEADS + h] + jnp.sum(p, axis=1, keepdims=True)
                acc_scr[g, :, sl] = alpha * acc_scr[g, :, sl] + _dot(p.astype(BF16), v_ref[:, sl].astype(BF16))
                m_scr[g * ATT_HEADS + h] = m_new

    @pl.when(j == pl.num_programs(2) - 1)
    def _():
        for h in range(ATT_HEADS):
            sl = slice(h * HEAD_DIM, (h + 1) * HEAD_DIM)
            lse = [m_scr[g * ATT_HEADS + h] + jnp.log(l_scr[g * ATT_HEADS + h]) for g in range(ng)]
            mx = functools.reduce(jnp.maximum, lse)
            ws = [jnp.exp(x - mx) for x in lse]
            num = sum(ws[g] / l_scr[g * ATT_HEADS + h] * acc_scr[g, :, sl] for g in range(ng))
            o_ref[:, sl] = num / sum(ws)


def _attn_prompt(q_rot, kv_all, batch, seq, bq):
    n = q_rot.shape[0]
    ng = len(ATT_GROUPS)
    nq = seq // bq
    table = _prompt_key_blocks(bq)
    groups = jnp.asarray([g for g, _ in table], jnp.int32)
    backs = jnp.asarray([b for _, b in table], jnp.int32)

    def kv_map(half):
        def index(b, i, j, g_ref, back_ref):
            return (b * nq + jnp.maximum(i - back_ref[j], 0), 2 * g_ref[j] + half)
        return index

    grid_spec = pltpu.PrefetchScalarGridSpec(
        num_scalar_prefetch=2,
        grid=(batch, nq, len(table)),
        in_specs=[pl.BlockSpec((bq, ATT_WIDTH), lambda b, i, j, g_ref, back_ref: (b * nq + i, 0)),
                  pl.BlockSpec((bq, ATT_OUT), kv_map(0)),
                  pl.BlockSpec((bq, ATT_OUT), kv_map(1))],
        out_specs=pl.BlockSpec((bq, ATT_OUT), lambda b, i, j, g_ref, back_ref: (b * nq + i, 0)),
        scratch_shapes=[pltpu.VMEM((ng * ATT_HEADS, bq, 1), F32), pltpu.VMEM((ng * ATT_HEADS, bq, 1), F32),
                        pltpu.VMEM((ng, bq, ATT_OUT), F32)],
    )

    def body(g_ref, back_ref, *refs):
        del g_ref, back_ref
        _attn_prompt_body(*refs, bq=bq, table=table)

    return pl.pallas_call(
        body,
        grid_spec=grid_spec,
        out_shape=jax.ShapeDtypeStruct((n, ATT_OUT), F32),
        compiler_params=_params("arbitrary", "arbitrary", "arbitrary"),
        name="attn_prompt",
    )(groups, backs, q_rot, kv_all, kv_all)


def _head_lanes(ref2d, first, rows):
    return jnp.concatenate([ref2d[pl.ds(first + h, rows, stride=KV_SLAB), :] for h in range(ATT_HEADS)], axis=1)


def _attn_sample_body(q_ref, kvn_ref, *rest, tq):
    cache_refs, o_ref = rest[:-2], rest[-1]
    nrow = ATT_HEADS * tq
    nkey = ATT_BLOCK
    rid = lax.broadcasted_iota(jnp.int32, (nrow, 1), 0)
    i_row = rid % tq
    head_row = rid // tq
    lane_head = lax.broadcasted_iota(jnp.int32, (1, ATT_OUT), 1) // HEAD_DIM
    head_mask = head_row == lane_head
    key = lax.broadcasted_iota(jnp.int32, (1, nkey), 1)
    pad = jnp.zeros((nkey - tq, ATT_OUT), F32)
    outs, lses = [], []
    ci = 0
    for g, (_, d) in enumerate(ATT_GROUPS):
        qg = q_ref[:, g * ATT_OUT:(g + 1) * ATT_OUT]
        qrows = jnp.where(head_mask, jnp.concatenate([qg] * ATT_HEADS, axis=0), 0.0).astype(BF16)
        k_new = jnp.concatenate([kvn_ref[:, g * KV_ROW: g * KV_ROW + ATT_OUT], pad], axis=0)
        v_new = jnp.concatenate([kvn_ref[:, g * KV_ROW + ATT_OUT: (g + 1) * KV_ROW], pad], axis=0)
        valid_new = jnp.logical_and(jnp.logical_and(key < tq, key <= i_row), (key % d) == (i_row % d))
        blocks = [(k_new, v_new, valid_new)]
        for r in range(min(d, tq)):
            c2 = cache_refs[ci].reshape(nkey * KV_SLAB, HEAD_DIM)
            ci += 1
            valid = jnp.logical_and((i_row % d) == r, key >= i_row // d)
            blocks.append((_head_lanes(c2, 0, nkey), _head_lanes(c2, ATT_HEADS, nkey), valid))
        m = jnp.full((nrow, 1), NEG, F32)
        l = jnp.zeros((nrow, 1), F32)
        acc = jnp.zeros((nrow, ATT_OUT), F32)
        for kb, vb, valid in blocks:
            s = jnp.where(valid, _dot_nt(qrows, kb.astype(BF16)), NEG)
            m_new = jnp.maximum(m, jnp.max(s, axis=1, keepdims=True))
            alpha = jnp.exp(m - m_new)
            p = jnp.where(valid, jnp.exp(s - m_new), 0.0)
            l = alpha * l + jnp.sum(p, axis=1, keepdims=True)
            acc = alpha * acc + _dot(p.astype(BF16), vb.astype(BF16))
            m = m_new
        o = acc / l
        lse = m + jnp.log(l)
        outs.append(jnp.concatenate(
            [o[h * tq:(h + 1) * tq, h * HEAD_DIM:(h + 1) * HEAD_DIM] for h in range(ATT_HEADS)], axis=1))
        lses.append(jnp.concatenate(
            [jnp.broadcast_to(lse[h * tq:(h + 1) * tq], (tq, HEAD_DIM)) for h in range(ATT_HEADS)], axis=1))
    mx = jnp.maximum(jnp.maximum(lses[0], lses[1]), lses[2])
    ws = [jnp.exp(x - mx) for x in lses]
    o_ref[...] = (ws[0] * outs[0] + ws[1] * outs[1] + ws[2] * outs[2]) / (ws[0] + ws[1] + ws[2])


def _attn_sample(q_rot, kv_all, caches, attn_buf, row0, batch, tq):
    ng = len(ATT_GROUPS)
    blk0 = row0 // tq
    views, specs = [], []
    for (w, d), c in zip(ATT_GROUPS, caches):
        assert c.shape[1] == w and w // d == ATT_BLOCK, "window buffers must hold exactly one window"
        view = c.reshape(batch, w // d, d, KV_SLAB, HEAD_DIM)
        for r in range(min(d, tq)):
            views.append(view)
            specs.append(pl.BlockSpec((None, w // d, None, KV_SLAB, HEAD_DIM), lambda b, r=r: (b, 0, r, 0, 0)))
    n_in = 2 + len(views)
    return pl.pallas_call(
        functools.partial(_attn_sample_body, tq=tq),
        grid=(batch,),
        in_specs=[pl.BlockSpec((tq, ATT_WIDTH), lambda b: (blk0 + b, 0)),
                  pl.BlockSpec((tq, ng * KV_ROW), lambda b: (blk0 + b, 0))] + specs
                 + [pl.BlockSpec(memory_space=pl.ANY)],
        out_specs=pl.BlockSpec((tq, ATT_OUT), lambda b: (blk0 + b, 0)),
        out_shape=jax.ShapeDtypeStruct(attn_buf.shape, F32),
        input_output_aliases={n_in: 0},
        compiler_params=_params("arbitrary"),
        name="attn_sample",
    )(q_rot, kv_all, *views, attn_buf)


def _kv_rows_body(kv_ref, o_ref):
    tm = kv_ref.shape[0]
    for s in range(KV_SLAB):
        o_ref[pl.ds(s, tm, stride=KV_SLAB), :] = kv_ref[:, s * HEAD_DIM:(s + 1) * HEAD_DIM]


def _kv_rows(kv_all, g, row0, batch, seq, keep, tm):
    nt = keep // tm
    rb0 = (row0 + seq - keep) // tm
    per_seq = seq // tm
    out = pl.pallas_call(
        _kv_rows_body,
        grid=(batch, nt),
        in_specs=[pl.BlockSpec((tm, KV_ROW), lambda b, t: (rb0 + b * per_seq + t, g))],
        out_specs=pl.BlockSpec((tm * KV_SLAB, HEAD_DIM), lambda b, t: (b * nt + t, 0)),
        out_shape=jax.ShapeDtypeStruct((batch * keep * KV_SLAB, HEAD_DIM), F32),
        compiler_params=_params("arbitrary", "arbitrary"),
        name=f"kv_rows_g{g}_{tm}",
    )(kv_all)
    return out.reshape(batch, keep, 2, ATT_HEADS, HEAD_DIM)


def _cumsum_rows(x):
    c = x.shape[0]
    row = lax.broadcasted_iota(jnp.int32, (c, 1), 0)
    sh = 1
    while sh < c:
        x = x + jnp.where(row >= sh, pltpu.roll(x, sh, axis=0), 0.0)
        sh *= 2
    return x


def _bcast_rows(b, first, period):
    c, w = b.shape
    parts = [jnp.broadcast_to(b[p * period + first: p * period + first + 1, :], (period, w))
             for p in range(c // period)]
    return parts[0] if len(parts) == 1 else jnp.concatenate(parts, axis=0)


def _hgrn_body(*refs, chunk, has_s0, has_alias):
    refs = list(refs)
    qh_ref, fh_ref, ih_ref, og_ref, lbl_ref, gn_ref = refs[:6]
    s0_ref = refs[6] if has_s0 else None
    o_ref, so_ref, st_ref = refs[-3:]
    c = chunk
    nh = ATT_HEADS
    t = pl.program_id(2)

    @pl.when(t == 0)
    def _():
        for h in range(nh):
            if has_s0:
                st_ref[h] = s0_ref[h].T
            else:
                st_ref[h] = jnp.zeros((HEAD_DIM, HEAD_DIM), F32)

    lbl = lbl_ref[...]
    e = jnp.exp(lbl - jnp.max(lbl, axis=0, keepdims=True))
    lb = e[0:1, :] / jnp.sum(e, axis=0, keepdims=True)
    f = lb + (1.0 - lb) * _sigmoid(fh_ref[...])
    kk = 1.0 - f
    qf = _silu(qh_ref[...])
    b = _cumsum_rows(jnp.log(f))
    vv = ih_ref[...]

    row = lax.broadcasted_iota(jnp.int32, (c, c), 0)
    col = lax.broadcasted_iota(jnp.int32, (c, c), 1)
    rid = lax.broadcasted_iota(jnp.int32, (c, 1), 0)

    a = [jnp.zeros((c, c), F32) for _ in range(nh)]
    s = c // 2
    while s >= HGRN_SUB:
        ref = _bcast_rows(b, s - 1, 2 * s)
        second = (rid % (2 * s)) >= s
        dlt = b - ref
        ee = jnp.exp(jnp.where(second, dlt, -dlt))
        ql = jnp.where(second, qf * ee, 0.0).astype(BF16)
        kl = jnp.where(second, 0.0, kk * ee).astype(BF16)
        same = (row // (2 * s)) == (col // (2 * s))
        for h in range(nh):
            sl = slice(h * HEAD_DIM, (h + 1) * HEAD_DIM)
            a[h] = a[h] + jnp.where(same, _dot_nt(ql[:, sl], kl[:, sl]), 0.0)
        s //= 2
    sub = min(HGRN_SUB, c)
    ref = _bcast_rows(b, 0, sub)
    dlt = b - ref
    qd = (qf * jnp.exp(dlt)).astype(BF16)
    kd = (kk * jnp.exp(jnp.minimum(-dlt, 80.0))).astype(BF16)
    diag = jnp.logical_and((row // sub) == (col // sub), col <= row)
    qe = (qf * jnp.exp(b)).astype(BF16)
    b_last = b[c - 1:c, :]
    kend = (kk * jnp.exp(b_last - b)).astype(BF16)
    dec = jnp.exp(b_last)
    gn = gn_ref[...]
    og = og_ref[...]
    for h in range(nh):
        sl = slice(h * HEAD_DIM, (h + 1) * HEAD_DIM)
        ah = a[h] + jnp.where(diag, _dot_nt(qd[:, sl], kd[:, sl]), 0.0)
        v = vv[:, sl]
        vb = v.astype(BF16)
        st = st_ref[h]
        o = _dot_nt(qe[:, sl], st.astype(BF16)) + _dot(ah.astype(BF16), vb)
        st_new = st * dec[:, sl] + lax.dot_general(vb, kend[:, sl], (((0,), (0,)), ((), ())),
                                                   preferred_element_type=F32)
        st_ref[h] = st_new
        ms = jnp.mean(o * o, axis=-1, keepdims=True)
        o_ref[:, sl] = o * lax.rsqrt(ms + EPS) * gn * _silu(og[:, sl])

        @pl.when(t == pl.num_programs(2) - 1)
        def _(h=h, st_new=st_new):
            so_ref[h] = st_new.T


def _hgrn(proj, col0, lb_logits, gnorm, s0, out_buf, row0, batch, seq, chunk):
    n = proj.shape[0]
    nt = seq // chunk
    hb = HGRN_HEADS // ATT_HEADS
    width = HGRN_HEADS * HEAD_DIM
    cb0 = col0 // ATT_OUT
    rb0 = row0 // chunk

    def col(j):
        return pl.BlockSpec((chunk, ATT_OUT), lambda b, h, t, j=j: (rb0 + b * nt + t, cb0 + j * hb + h))

    in_specs = [col(0), col(1), col(2), col(3),
                pl.BlockSpec((lb_logits.shape[0], ATT_OUT), lambda b, h, t: (0, h)),
                pl.BlockSpec((1, HEAD_DIM), lambda b, h, t: (0, 0))]
    args = [proj, proj, proj, proj, lb_logits, gnorm.reshape(1, HEAD_DIM)]
    state_spec = pl.BlockSpec((None, ATT_HEADS, HEAD_DIM, HEAD_DIM), lambda b, h, t: (b, h, 0, 0))
    if s0 is not None:
        in_specs.append(state_spec)
        args.append(s0)
    aliases = {}
    if out_buf is not None:
        aliases = {len(args): 0}
        in_specs.append(pl.BlockSpec(memory_space=pl.ANY))
        args.append(out_buf)
    return pl.pallas_call(
        functools.partial(_hgrn_body, chunk=chunk, has_s0=s0 is not None, has_alias=out_buf is not None),
        grid=(batch, hb, nt),
        in_specs=in_specs,
        out_specs=[pl.BlockSpec((chunk, ATT_OUT), lambda b, h, t: (rb0 + b * nt + t, h)), state_spec],
        out_shape=[jax.ShapeDtypeStruct((n, width), F32),
                   jax.ShapeDtypeStruct((batch, HGRN_HEADS, HEAD_DIM, HEAD_DIM), F32)],
        scratch_shapes=[pltpu.VMEM((ATT_HEADS, HEAD_DIM, HEAD_DIM), F32)],
        input_output_aliases=aliases,
        compiler_params=_params("arbitrary", "arbitrary", "arbitrary"),
        name=f"hgrn_c{chunk}",
    )(*args)


def _gated_merge_body(att_ref, hg_ref, wa_ref, wh_ref, ga_ref, gh_ref, o_ref, wab_ref, whb_ref):
    @pl.when(pl.program_id(1) == 0)
    def _():
        wab_ref[...] = wa_ref[...].astype(BF16)
        whb_ref[...] = wh_ref[...].astype(BF16)

    pa = _dot(att_ref[...].astype(BF16), wab_ref[...])
    ph = _dot(hg_ref[...].astype(BF16), whb_ref[...])
    o_ref[...] = (_sigmoid(ga_ref[...]) * pa + _sigmoid(gh_ref[...]) * ph).astype(o_ref.dtype)


def _gated_merge(attn, hgrn, w_pa, w_ph, proj, col_ga, col_gh, tm, tn):
    m = attn.shape[0]
    n = w_pa.shape[1]
    ka, kh = w_pa.shape[0], w_ph.shape[0]
    ca, ch = col_ga // tn, col_gh // tn
    return pl.pallas_call(
        _gated_merge_body,
        grid=(n // tn, m // tm),
        in_specs=[pl.BlockSpec((tm, ka), lambda j, i: (i, 0)),
                  pl.BlockSpec((tm, kh), lambda j, i: (i, 0)),
                  pl.BlockSpec((ka, tn), lambda j, i: (0, j)),
                  pl.BlockSpec((kh, tn), lambda j, i: (0, j)),
                  pl.BlockSpec((tm, tn), lambda j, i: (i, ca + j)),
                  pl.BlockSpec((tm, tn), lambda j, i: (i, ch + j))],
        out_specs=pl.BlockSpec((tm, tn), lambda j, i: (i, j)),
        out_shape=jax.ShapeDtypeStruct((m, n), BF16),
        scratch_shapes=[pltpu.VMEM((ka, tn), BF16), pltpu.VMEM((kh, tn), BF16)],
        compiler_params=_params("arbitrary", "arbitrary"),
        name="gated_merge",
    )(attn, hgrn, w_pa, w_ph, proj, proj)


def _cross_body(q_ref, kv_ref, *rest, slab_rows):
    o_ref = rest[-1]
    scale = HEAD_DIM ** -0.5
    if slab_rows:
        mem_len = kv_ref.shape[0]
        kv2 = kv_ref.reshape(mem_len * KV_SLAB, HEAD_DIM)
    for h in range(ATT_HEADS):
        sl = slice(h * HEAD_DIM, (h + 1) * HEAD_DIM)
        q = (q_ref[:, sl] * scale).astype(BF16)
        if slab_rows:
            k = kv2[pl.ds(h, mem_len, stride=KV_SLAB), :].astype(BF16)
            v = kv2[pl.ds(ATT_HEADS + h, mem_len, stride=KV_SLAB), :].astype(BF16)
        else:
            k = kv_ref[:, sl].astype(BF16)
            v = kv_ref[:, ATT_OUT + h * HEAD_DIM: ATT_OUT + (h + 1) * HEAD_DIM].astype(BF16)
        s = _dot_nt(q, k)
        p = jnp.exp(s - jnp.max(s, axis=1, keepdims=True))
        o_ref[:, sl] = _dot(p.astype(BF16), v) / jnp.sum(p, axis=1, keepdims=True)


def _cross_attn(q, mem_kv, out_buf, row0, batch, seq, tq):
    n = q.shape[0]
    nq = seq // tq
    rb0 = row0 // tq
    slab_rows = mem_kv.ndim == 4
    kv_spec = pl.BlockSpec((None,) + mem_kv.shape[1:], lambda b, i: (b,) + (0,) * (mem_kv.ndim - 1))
    in_specs = [pl.BlockSpec((tq, ATT_OUT), lambda b, i: (rb0 + b * nq + i, 0)), kv_spec]
    args = [q, mem_kv]
    aliases = {}
    if out_buf is not None:
        aliases = {2: 0}
        in_specs.append(pl.BlockSpec(memory_space=pl.ANY))
        args.append(out_buf)
    return pl.pallas_call(
        functools.partial(_cross_body, slab_rows=slab_rows),
        grid=(batch, nq),
        in_specs=in_specs,
        out_specs=pl.BlockSpec((tq, ATT_OUT), lambda b, i: (rb0 + b * nq + i, 0)),
        out_shape=jax.ShapeDtypeStruct((n, ATT_OUT), F32),
        input_output_aliases=aliases,
        compiler_params=_params("arbitrary", "arbitrary"),
        name=f"cross_attn_t{tq}",
    )(*args)


def _router_body(x_ref, g_ref, whi_ref, wlo_ref, b_ref, xn_ref, cw_ref):
    x = x_ref[...]
    ms = jnp.mean(x * x, axis=-1, keepdims=True)
    xn = x * lax.rsqrt(ms + EPS) * g_ref[...]
    hi = xn.astype(BF16)
    xn_ref[...] = hi
    lo = (xn - hi.astype(F32)).astype(BF16)
    logits = _dot(hi, whi_ref[...]) + _dot(lo, whi_ref[...]) + _dot(hi, wlo_ref[...]) + b_ref[...]
    lane = lax.broadcasted_iota(jnp.int32, logits.shape, 1)
    big = jnp.int32(1 << 20)
    is_g = jnp.logical_and(lane >= N_EXPERTS, lane < N_EXPERTS + N_GROUPS)
    lg = jnp.where(is_g, logits, NEG)
    mg = jnp.max(lg, axis=1, keepdims=True)
    p_top = 1.0 / jnp.sum(jnp.where(is_g, jnp.exp(lg - mg), 0.0), axis=1, keepdims=True)
    g_idx = jnp.min(jnp.where(jnp.logical_and(is_g, lg == mg), lane, big), axis=1, keepdims=True) - N_EXPERTS
    in_grp = jnp.logical_and(lane < N_EXPERTS, lane // EXPERTS_PER_GROUP == g_idx)
    le = jnp.where(in_grp, logits, NEG)
    v1 = jnp.max(le, axis=1, keepdims=True)
    i1 = jnp.min(jnp.where(jnp.logical_and(in_grp, le == v1), lane, big), axis=1, keepdims=True)
    rest = jnp.logical_and(in_grp, lane != i1)
    le2 = jnp.where(rest, logits, NEG)
    v2 = jnp.max(le2, axis=1, keepdims=True)
    i2 = jnp.min(jnp.where(jnp.logical_and(rest, le2 == v2), lane, big), axis=1, keepdims=True)
    e2 = jnp.exp(v2 - v1)
    w1 = p_top / (1.0 + e2)
    w2 = p_top * e2 / (1.0 + e2)
    cw_ref[...] = jnp.where(lane == i1, w1, 0.0) + jnp.where(lane == i2, w2, 0.0)


def _router(x, g, w_hi, w_lo, bias, tm):
    m, d = x.shape
    lanes = w_hi.shape[1]
    const = lambda shape: pl.BlockSpec(shape, lambda i: (0, 0))
    return pl.pallas_call(
        _router_body,
        grid=(m // tm,),
        in_specs=[pl.BlockSpec((tm, d), lambda i: (i, 0)), const((1, d)), const((d, lanes)), const((d, lanes)),
                  const((1, lanes))],
        out_specs=[pl.BlockSpec((tm, d), lambda i: (i, 0)), pl.BlockSpec((tm, lanes), lambda i: (i, 0))],
        out_shape=[jax.ShapeDtypeStruct((m, d), BF16), jax.ShapeDtypeStruct((m, lanes), F32)],
        compiler_params=_params("arbitrary"),
        name="router",
    )(x, g.reshape(1, d), w_hi, w_lo, bias)


def _moe_body(xn_ref, cw_ref, wg_ref, wu_ref, wd_ref, x_ref, gf_ref, y_ref, acc_ref):
    e = pl.program_id(1)

    @pl.when(e == 0)
    def _():
        acc_ref[...] = jnp.zeros_like(acc_ref)

    xn = xn_ref[...]
    h = _silu(_dot(xn, wg_ref[...])) * _dot(xn, wu_ref[...])
    cw = cw_ref[...]
    lane = lax.broadcasted_iota(jnp.int32, cw.shape, 1)
    w = jnp.sum(jnp.where(lane == e, cw, 0.0), axis=1, keepdims=True)
    acc_ref[...] += _dot((h * w).astype(BF16), wd_ref[...])

    @pl.when(e == pl.num_programs(1) - 1)
    def _():
        x = x_ref[...] + acc_ref[...]
        ms = jnp.mean(x * x, axis=-1, keepdims=True)
        y_ref[...] = x * lax.rsqrt(ms + EPS) * gf_ref[...]


def _moe(xn, cw, w_gate, w_up, w_down, x, g_final, tm):
    m, d = x.shape
    ne, _, ff = w_gate.shape
    lanes = cw.shape[1]
    row = lambda width: pl.BlockSpec((tm, width), lambda i, e: (i, 0))
    return pl.pallas_call(
        _moe_body,
        grid=(m // tm, ne),
        in_specs=[row(d), row(lanes),
                  pl.BlockSpec((None, d, ff), lambda i, e: (e, 0, 0)),
                  pl.BlockSpec((None, d, ff), lambda i, e: (e, 0, 0)),
                  pl.BlockSpec((None, ff, d), lambda i, e: (e, 0, 0)),
                  row(d), pl.BlockSpec((1, d), lambda i, e: (0, 0))],
        out_specs=row(d),
        out_shape=jax.ShapeDtypeStruct((m, d), F32),
        scratch_shapes=[pltpu.VMEM((tm, d), F32)],
        compiler_params=_params("arbitrary", "arbitrary"),
        name="moe",
    )(xn, cw, w_gate, w_up, w_down, x, g_final.reshape(1, d))


def _rope_tables(positions):
    half = HEAD_DIM // 2
    inv_freq = ROPE_THETA ** (-jnp.arange(half, dtype=F32) / half)
    ang = positions.astype(F32)[:, None] * inv_freq[None, :]
    cos, sin = jnp.cos(ang), jnp.sin(ang)
    return jnp.concatenate([cos, cos], axis=1), jnp.concatenate([-sin, sin], axis=1)


def kernel(x_prompt, x_sample, cache_swa1, cache_swa2, cache_swa3, state_hgrn, cache_mem_kv, mem_prompt,
           hgrn_lb_logits, norm_mix, w_in, w_proj_attn, w_proj_hgrn, w_out, hgrn_norm, norm_cross, norm_mem,
           w_cq, w_ckv, w_co, norm_ffn, w_rg, b_rg, w_re, b_re, w_e_gate, w_e_up, w_e_down, norm_final):
    bp, seq, d = x_prompt.shape
    bs, dseq, _ = x_sample.shape
    depth = w_in.shape[0]
    assert depth == 1, "single-layer trunk"
    past = cache_swa3.shape[2]
    mem_len = mem_prompt.shape[1]
    np_, ns = bp * seq, bs * dseq
    hw = HGRN_HEADS * HEAD_DIM
    col_hgrn = 3 * ATT_WIDTH
    col_ga = col_hgrn + 4 * hw
    col_gh = col_ga + d
    l = 0

    x_all = jnp.concatenate([x_prompt.reshape(np_, d), x_sample.reshape(ns, d)], axis=0)
    pos = jnp.concatenate([jnp.tile(jnp.arange(seq, dtype=jnp.int32), bp),
                           jnp.tile(past + jnp.arange(dseq, dtype=jnp.int32), bs)])
    cos2, sin2 = _rope_tables(pos)

    xn = _rmsnorm(x_all, norm_mix[l], BF16, 512)
    proj = _matmul(xn, w_in[l], F32, 512, 1536, name="proj_in")

    q_rot, kv_all = _rope(proj, cos2, sin2, 256)
    attn = _attn_prompt(q_rot, kv_all, bp, seq, 256)
    attn = _attn_sample(q_rot, kv_all, (cache_swa1[l], cache_swa2[l], cache_swa3[l]), attn, np_, bs, dseq)

    hgrn, st_p = _hgrn(proj, col_hgrn, hgrn_lb_logits, hgrn_norm[l], None, None, 0, bp, seq, 128)
    hgrn, st_s = _hgrn(proj, col_hgrn, hgrn_lb_logits, hgrn_norm[l], state_hgrn[l], hgrn, np_, bs, dseq, dseq)

    merged = _gated_merge(attn, hgrn, w_proj_attn[l], w_proj_hgrn[l], proj, col_ga, col_gh, 512, 512)
    x1 = _matmul(merged, w_out[l], F32, 512, 1024, res=x_all, name="proj_out")

    mem_n = _rmsnorm(mem_prompt.reshape(bp * mem_len, d), norm_mem[l], BF16, 256)
    mkv_p = _matmul(mem_n, w_ckv[l], F32, 256, KV_ROW, name="mem_kv")
    xc = _rmsnorm(x1, norm_cross[l], BF16, 512)
    qc = _matmul(xc, w_cq[l], F32, 512, ATT_OUT, name="cross_q")
    oc = _cross_attn(qc, mkv_p.reshape(bp, mem_len, KV_ROW), None, 0, bp, seq, 512)
    oc = _cross_attn(qc, cache_mem_kv[l].reshape(bs, mem_len, KV_SLAB, HEAD_DIM), oc, np_, bs, dseq, dseq)
    x2 = _matmul(oc, w_co[l], F32, 512, 1024, res=x1, name="cross_out")

    lanes = 128
    w_r = jnp.concatenate([w_re[l], w_rg[l], jnp.zeros((d, lanes - N_EXPERTS - N_GROUPS), F32)], axis=1)
    b_r = jnp.concatenate([b_re[l], b_rg[l], jnp.zeros((lanes - N_EXPERTS - N_GROUPS,), F32)]).reshape(1, lanes)
    w_r_hi = w_r.astype(BF16)
    w_r_lo = (w_r - w_r_hi.astype(F32)).astype(BF16)
    xf, cw = _router(x2, norm_ffn[l], w_r_hi, w_r_lo, b_r, 512)
    ff = w_e_gate.shape[-1]
    y = _moe(xf, cw,
             w_e_gate[l].reshape(N_EXPERTS, d, ff).astype(BF16),
             w_e_up[l].reshape(N_EXPERTS, d, ff).astype(BF16),
             w_e_down[l].reshape(N_EXPERTS, ff, d).astype(BF16),
             x2, norm_final, 512)

    swa_p = [_kv_rows(kv_all, g, 0, bp, seq, min(w, seq), min(w, seq, 256))[None]
             for g, (w, _) in enumerate(ATT_GROUPS)]
    swa_s = [_kv_rows(kv_all, g, np_, 1, ns, ns, 256).reshape(1, bs, dseq, 2, ATT_HEADS, HEAD_DIM)
             for g in range(len(ATT_GROUPS))]
    return (y[:np_].reshape(bp, seq, d), y[np_:].reshape(bs, dseq, d),
            swa_p[0], swa_p[1], swa_p[2], st_p[None],
            mkv_p.reshape(1, bp, mem_len, 2, ATT_HEADS, HEAD_DIM),
            swa_s[0], swa_s[1], swa_s[2], st_s[None])
```

```python
import functools

import jax
import jax.numpy as jnp
from jax import lax
from jax.experimental import pallas as pl
from jax.experimental.pallas import tpu as pltpu

F32 = jnp.float32
BF16 = jnp.bfloat16

ATT_GROUPS = ((128, 1), (512, 4), (2048, 16))
ATT_HEADS = 4
HEAD_DIM = 128
ATT_OUT = ATT_HEADS * HEAD_DIM
ATT_WIDTH = len(ATT_GROUPS) * ATT_OUT
KV_ROW = 2 * ATT_OUT
KV_SLAB = 2 * ATT_HEADS
HGRN_HEADS = 16
ROPE_THETA = 10000.0
EPS = 1e-6
N_EXPERTS = 32
N_GROUPS = 4
EXPERTS_PER_GROUP = 8
NEG = -1e30
ATT_BLOCK = 128
HGRN_SUB = 8
ROW_SLAB = 16
MOE_TILE = 256
VMEM_LIMIT = 56 * 1024 * 1024


def _params(*sem):
    return pltpu.CompilerParams(dimension_semantics=sem, vmem_limit_bytes=VMEM_LIMIT)


def _dot(a, b):
    return jnp.dot(a, b, preferred_element_type=F32)


def _dot_nt(a, b):
    return lax.dot_general(a, b, (((1,), (1,)), ((), ())), preferred_element_type=F32)


def _sigmoid(x):
    return 1.0 / (1.0 + jnp.exp(-x))


def _silu(x):
    return x * _sigmoid(x)


def _rmsnorm_body(x_ref, g_ref, o_ref):
    x = x_ref[...]
    ms = jnp.mean(x * x, axis=-1, keepdims=True)
    o_ref[...] = (x * lax.rsqrt(ms + EPS) * g_ref[...]).astype(o_ref.dtype)


def _rmsnorm(x, g, out_dtype, tm):
    m, d = x.shape
    return pl.pallas_call(
        _rmsnorm_body,
        grid=(m // tm,),
        in_specs=[pl.BlockSpec((tm, d), lambda i: (i, 0)), pl.BlockSpec((1, d), lambda i: (0, 0))],
        out_specs=pl.BlockSpec((tm, d), lambda i: (i, 0)),
        out_shape=jax.ShapeDtypeStruct((m, d), out_dtype),
        compiler_params=_params("arbitrary"),
        name="rmsnorm",
    )(x, g.reshape(1, d))


def _mm_body(*refs, has_res):
    if has_res:
        a_ref, w_ref, r_ref, o_ref, wb_ref = refs
    else:
        a_ref, w_ref, o_ref, wb_ref = refs

    @pl.when(pl.program_id(1) == 0)
    def _():
        wb_ref[...] = w_ref[...].astype(BF16)

    acc = _dot(a_ref[...].astype(BF16), wb_ref[...])
    if has_res:
        acc = acc + r_ref[...]
    o_ref[...] = acc.astype(o_ref.dtype)


def _matmul(a, w, out_dtype, tm, tn, res=None, name="matmul"):
    m, k = a.shape
    n = w.shape[1]
    in_specs = [pl.BlockSpec((tm, k), lambda j, i: (i, 0)), pl.BlockSpec((k, tn), lambda j, i: (0, j))]
    args = [a, w]
    if res is not None:
        in_specs.append(pl.BlockSpec((tm, tn), lambda j, i: (i, j)))
        args.append(res)
    return pl.pallas_call(
        functools.partial(_mm_body, has_res=res is not None),
        grid=(n // tn, m // tm),
        in_specs=in_specs,
        out_specs=pl.BlockSpec((tm, tn), lambda j, i: (i, j)),
        out_shape=jax.ShapeDtypeStruct((m, n), out_dtype),
        scratch_shapes=[pltpu.VMEM((k, tn), BF16)],
        compiler_params=_params("arbitrary", "arbitrary"),
        name=name,
    )(*args)


def _rope_body(q_ref, k_ref, v_ref, cos_ref, sin_ref, qo_ref, kv_ref):
    cos = cos_ref[...]
    sin = sin_ref[...]
    scale = HEAD_DIM ** -0.5
    for j in range(ATT_WIDTH // HEAD_DIM):
        sl = slice(j * HEAD_DIM, (j + 1) * HEAD_DIM)
        q = q_ref[:, sl]
        qo_ref[:, sl] = (q * cos + pltpu.roll(q, HEAD_DIM // 2, axis=1) * sin) * scale
        k = k_ref[:, sl]
        g, h = divmod(j, ATT_HEADS)
        kv_ref[:, g * KV_ROW + h * HEAD_DIM: g * KV_ROW + (h + 1) * HEAD_DIM] = (
            k * cos + pltpu.roll(k, HEAD_DIM // 2, axis=1) * sin)
    for g in range(len(ATT_GROUPS)):
        kv_ref[:, g * KV_ROW + ATT_OUT: (g + 1) * KV_ROW] = v_ref[:, g * ATT_OUT: (g + 1) * ATT_OUT]


def _rope(proj, cos2, sin2, tm):
    n = proj.shape[0]
    blk = lambda c: pl.BlockSpec((tm, ATT_WIDTH), lambda i, c=c: (i, c))
    tab = pl.BlockSpec((tm, HEAD_DIM), lambda i: (i, 0))
    return pl.pallas_call(
        _rope_body,
        grid=(n // tm,),
        in_specs=[blk(0), blk(1), blk(2), tab, tab],
        out_specs=[pl.BlockSpec((tm, ATT_WIDTH), lambda i: (i, 0)),
                   pl.BlockSpec((tm, len(ATT_GROUPS) * KV_ROW), lambda i: (i, 0))],
        out_shape=[jax.ShapeDtypeStruct((n, ATT_WIDTH), F32),
                   jax.ShapeDtypeStruct((n, len(ATT_GROUPS) * KV_ROW), F32)],
        compiler_params=_params("arbitrary"),
        name="rope",
    )(proj, proj, proj, cos2, sin2)


def _prompt_key_blocks(bq):
    table = []
    for g, (w, _) in enumerate(ATT_GROUPS):
        nback = -(-w // bq)
        table.extend((g, back) for back in range(nback, -1, -1))
    return tuple(table)


def _attn_prompt_body(q_ref, k_ref, v_ref, o_ref, m_scr, l_scr, acc_scr, *, bq, table):
    i = pl.program_id(1)
    j = pl.program_id(2)
    ng = len(ATT_GROUPS)

    @pl.when(j == 0)
    def _():
        m_scr[...] = jnp.full(m_scr.shape, NEG, F32)
        l_scr[...] = jnp.zeros(l_scr.shape, F32)
        acc_scr[...] = jnp.zeros(acc_scr.shape, F32)

    row = lax.broadcasted_iota(jnp.int32, (bq, bq), 0)
    col = lax.broadcasted_iota(jnp.int32, (bq, bq), 1)
    for g, (w, d) in enumerate(ATT_GROUPS):
        j0 = min(jj for jj, (gg, _) in enumerate(table) if gg == g)
        nback = max(back for gg, back in table if gg == g)
        back = nback - (j - j0)

        @pl.when(jnp.logical_and(jnp.logical_and(j >= j0, j <= j0 + nback), i >= back))
        def _(g=g, w=w, d=d, back=back):
            dist = back * bq + row - col
            valid = jnp.logical_and(jnp.logical_and(dist >= 0, dist <= w), (dist & (d - 1)) == 0)
            bias = jnp.where(valid, 0.0, NEG)
            heads = range(ATT_HEADS)
            hs = [slice(h * HEAD_DIM, (h + 1) * HEAD_DIM) for h in heads]
            m_old = [m_scr[g * ATT_HEADS + h] for h in heads]
            l_old = [l_scr[g * ATT_HEADS + h] for h in heads]
            a_old = [acc_scr[g, :, hs[h]] for h in heads]
            qs = [q_ref[:, g * ATT_OUT + h * HEAD_DIM: g * ATT_OUT + (h + 1) * HEAD_DIM].astype(BF16) for h in heads]
            ss = [_dot_nt(qs[h], k_ref[:, hs[h]].astype(BF16)) + bias for h in heads]
            m_new = [jnp.maximum(m_old[h], jnp.max(ss[h], axis=1, keepdims=True)) for h in heads]
            ps = [jnp.exp(ss[h] - jnp.concatenate([m_new[h]] * (bq // HEAD_DIM), axis=1)) for h in heads]
            alpha = [jnp.exp(m_old[h] - m_new[h]) for h in heads]
            pv = [_dot(ps[h].astype(BF16), v_ref[:, hs[h]].astype(BF16)) for h in heads]
            for h in heads:
                m_scr[g * ATT_HEADS + h] = m_new[h]
                l_scr[g * ATT_HEADS + h] = alpha[h] * l_old[h] + jnp.sum(ps[h], axis=1, keepdims=True)
                acc_scr[g, :, hs[h]] = alpha[h] * a_old[h] + pv[h]

    @pl.when(j == pl.num_programs(2) - 1)
    def _():
        for h in range(ATT_HEADS):
            sl = slice(h * HEAD_DIM, (h + 1) * HEAD_DIM)
            lse = [m_scr[g * ATT_HEADS + h] + jnp.log(l_scr[g * ATT_HEADS + h]) for g in range(ng)]
            mx = functools.reduce(jnp.maximum, lse)
            ws = [jnp.exp(x - mx) for x in lse]
            num = sum(ws[g] / l_scr[g * ATT_HEADS + h] * acc_scr[g, :, sl] for g in range(ng))
            o_ref[:, sl] = num / sum(ws)


def _attn_prompt(q_rot, kv_all, batch, seq, bq):
    n = q_rot.shape[0]
    ng = len(ATT_GROUPS)
    nq = seq // bq
    table = _prompt_key_blocks(bq)
    groups = jnp.asarray([g for g, _ in table], jnp.int32)
    backs = jnp.asarray([b for _, b in table], jnp.int32)

    def kv_map(half):
        def index(b, i, j, g_ref, back_ref):
            return (b * nq + jnp.maximum(i - back_ref[j], 0), 2 * g_ref[j] + half)
        return index

    grid_spec = pltpu.PrefetchScalarGridSpec(
        num_scalar_prefetch=2,
        grid=(batch, nq, len(table)),
        in_specs=[pl.BlockSpec((bq, ATT_WIDTH), lambda b, i, j, g_ref, back_ref: (b * nq + i, 0)),
                  pl.BlockSpec((bq, ATT_OUT), kv_map(0)),
                  pl.BlockSpec((bq, ATT_OUT), kv_map(1))],
        out_specs=pl.BlockSpec((bq, ATT_OUT), lambda b, i, j, g_ref, back_ref: (b * nq + i, 0)),
        scratch_shapes=[pltpu.VMEM((ng * ATT_HEADS, bq, HEAD_DIM), F32),
                        pltpu.VMEM((ng * ATT_HEADS, bq, HEAD_DIM), F32),
                        pltpu.VMEM((ng, bq, ATT_OUT), F32)],
    )

    def body(g_ref, back_ref, *refs):
        del g_ref, back_ref
        _attn_prompt_body(*refs, bq=bq, table=table)

    return pl.pallas_call(
        body,
        grid_spec=grid_spec,
        out_shape=jax.ShapeDtypeStruct((n, ATT_OUT), F32),
        compiler_params=_params("arbitrary", "arbitrary", "arbitrary"),
        name="attn_prompt",
    )(groups, backs, q_rot, kv_all, kv_all)


def _head_lanes(ref2d, first, rows):
    return jnp.concatenate([ref2d[pl.ds(first + h, rows, stride=KV_SLAB), :] for h in range(ATT_HEADS)], axis=1)


def _attn_sample_body(q_ref, kvn_ref, *rest, tq):
    cache_refs, o_ref = rest[:-2], rest[-1]
    nrow = ATT_HEADS * tq
    nkey = ATT_BLOCK
    rid = lax.broadcasted_iota(jnp.int32, (nrow, 1), 0)
    i_row = rid % tq
    head_row = rid // tq
    lane_head = lax.broadcasted_iota(jnp.int32, (1, ATT_OUT), 1) // HEAD_DIM
    head_mask = head_row == lane_head
    key = lax.broadcasted_iota(jnp.int32, (1, nkey), 1)
    pad = jnp.zeros((nkey - tq, ATT_OUT), F32)
    outs, lses = [], []
    ci = 0
    for g, (_, d) in enumerate(ATT_GROUPS):
        qg = q_ref[:, g * ATT_OUT:(g + 1) * ATT_OUT]
        qrows = jnp.where(head_mask, jnp.concatenate([qg] * ATT_HEADS, axis=0), 0.0).astype(BF16)
        k_new = jnp.concatenate([kvn_ref[:, g * KV_ROW: g * KV_ROW + ATT_OUT], pad], axis=0)
        v_new = jnp.concatenate([kvn_ref[:, g * KV_ROW + ATT_OUT: (g + 1) * KV_ROW], pad], axis=0)
        valid_new = jnp.logical_and(jnp.logical_and(key < tq, key <= i_row), (key % d) == (i_row % d))
        blocks = [(k_new, v_new, valid_new)]
        for r in range(min(d, tq)):
            c2 = cache_refs[ci].reshape(nkey * KV_SLAB, HEAD_DIM)
            ci += 1
            valid = jnp.logical_and((i_row % d) == r, key >= i_row // d)
            blocks.append((_head_lanes(c2, 0, nkey), _head_lanes(c2, ATT_HEADS, nkey), valid))
        ss = [jnp.where(valid, _dot_nt(qrows, kb.astype(BF16)), NEG) for kb, _, valid in blocks]
        m = functools.reduce(jnp.maximum, [jnp.max(s, axis=1, keepdims=True) for s in ss])
        ps = [jnp.exp(s - m) for s in ss]
        l = sum(jnp.sum(p, axis=1, keepdims=True) for p in ps)
        acc = sum(_dot(p.astype(BF16), vb.astype(BF16)) for p, (_, vb, _) in zip(ps, blocks))
        o = acc / l
        lse = m + jnp.log(l)
        outs.append(jnp.concatenate(
            [o[h * tq:(h + 1) * tq, h * HEAD_DIM:(h + 1) * HEAD_DIM] for h in range(ATT_HEADS)], axis=1))
        lses.append(jnp.concatenate(
            [jnp.broadcast_to(lse[h * tq:(h + 1) * tq], (tq, HEAD_DIM)) for h in range(ATT_HEADS)], axis=1))
    mx = jnp.maximum(jnp.maximum(lses[0], lses[1]), lses[2])
    ws = [jnp.exp(x - mx) for x in lses]
    o_ref[...] = (ws[0] * outs[0] + ws[1] * outs[1] + ws[2] * outs[2]) / (ws[0] + ws[1] + ws[2])


def _attn_sample(q_rot, kv_all, caches, attn_buf, row0, batch, tq):
    ng = len(ATT_GROUPS)
    blk0 = row0 // tq
    views, specs = [], []
    for (w, d), c in zip(ATT_GROUPS, caches):
        assert c.shape[1] == w and w // d == ATT_BLOCK, "window buffers must hold exactly one window"
        view = c.reshape(batch, w // d, d, KV_SLAB, HEAD_DIM)
        for r in range(min(d, tq)):
            views.append(view)
            specs.append(pl.BlockSpec((None, w // d, None, KV_SLAB, HEAD_DIM), lambda b, r=r: (b, 0, r, 0, 0)))
    n_in = 2 + len(views)
    return pl.pallas_call(
        functools.partial(_attn_sample_body, tq=tq),
        grid=(batch,),
        in_specs=[pl.BlockSpec((tq, ATT_WIDTH), lambda b: (blk0 + b, 0)),
                  pl.BlockSpec((tq, ng * KV_ROW), lambda b: (blk0 + b, 0))] + specs
                 + [pl.BlockSpec(memory_space=pl.ANY)],
        out_specs=pl.BlockSpec((tq, ATT_OUT), lambda b: (blk0 + b, 0)),
        out_shape=jax.ShapeDtypeStruct(attn_buf.shape, F32),
        input_output_aliases={n_in: 0},
        compiler_params=_params("arbitrary"),
        name="attn_sample",
    )(q_rot, kv_all, *views, attn_buf)


def _kv_rows_body(kv_ref, o_ref):
    tm = kv_ref.shape[0]
    for s in range(KV_SLAB):
        o_ref[pl.ds(s, tm, stride=KV_SLAB), :] = kv_ref[:, s * HEAD_DIM:(s + 1) * HEAD_DIM]


def _kv_rows(kv_all, g, row0, batch, seq, keep, tm):
    nt = keep // tm
    rb0 = (row0 + seq - keep) // tm
    per_seq = seq // tm
    out = pl.pallas_call(
        _kv_rows_body,
        grid=(batch, nt),
        in_specs=[pl.BlockSpec((tm, KV_ROW), lambda b, t: (rb0 + b * per_seq + t, g))],
        out_specs=pl.BlockSpec((tm * KV_SLAB, HEAD_DIM), lambda b, t: (b * nt + t, 0)),
        out_shape=jax.ShapeDtypeStruct((batch * keep * KV_SLAB, HEAD_DIM), F32),
        compiler_params=_params("arbitrary", "arbitrary"),
        name=f"kv_rows_g{g}_{tm}",
    )(kv_all)
    return out.reshape(batch, keep, 2, ATT_HEADS, HEAD_DIM)


def _cumsum_rows(x):
    c = x.shape[0]
    row = lax.broadcasted_iota(jnp.int32, (c, 1), 0)
    sh = 1
    while sh < c:
        x = x + jnp.where(row >= sh, pltpu.roll(x, sh, axis=0), 0.0)
        sh *= 2
    return x


def _bcast_rows(b, first, period):
    c, w = b.shape
    parts = [jnp.broadcast_to(b[p * period + first: p * period + first + 1, :], (period, w))
             for p in range(c // period)]
    return parts[0] if len(parts) == 1 else jnp.concatenate(parts, axis=0)


def _hgrn_body(*refs, chunk, has_s0, has_alias):
    refs = list(refs)
    qh_ref, fh_ref, ih_ref, og_ref, lbl_ref, gn_ref = refs[:6]
    s0_ref = refs[6] if has_s0 else None
    o_ref, so_ref, st_ref = refs[-3:]
    c = chunk
    nh = ATT_HEADS
    t = pl.program_id(2)

    @pl.when(t == 0)
    def _():
        for h in range(nh):
            if has_s0:
                st_ref[h] = s0_ref[h].T
            else:
                st_ref[h] = jnp.zeros((HEAD_DIM, HEAD_DIM), F32)

    lbl = lbl_ref[...]
    e = jnp.exp(lbl - jnp.max(lbl, axis=0, keepdims=True))
    lb = e[0:1, :] / jnp.sum(e, axis=0, keepdims=True)
    f = lb + (1.0 - lb) * _sigmoid(fh_ref[...])
    kk = 1.0 - f
    qf = _silu(qh_ref[...])
    b = _cumsum_rows(jnp.log(f))
    vv = ih_ref[...]

    row = lax.broadcasted_iota(jnp.int32, (c, c), 0)
    col = lax.broadcasted_iota(jnp.int32, (c, c), 1)
    rid = lax.broadcasted_iota(jnp.int32, (c, 1), 0)

    a = [jnp.zeros((c, c), F32) for _ in range(nh)]
    s = c // 2
    while s >= HGRN_SUB:
        ref = _bcast_rows(b, s - 1, 2 * s)
        second = (rid % (2 * s)) >= s
        dlt = b - ref
        ee = jnp.exp(jnp.where(second, dlt, -dlt))
        ql = jnp.where(second, qf * ee, 0.0).astype(BF16)
        kl = jnp.where(second, 0.0, kk * ee).astype(BF16)
        same = (row // (2 * s)) == (col // (2 * s))
        for h in range(nh):
            sl = slice(h * HEAD_DIM, (h + 1) * HEAD_DIM)
            a[h] = a[h] + jnp.where(same, _dot_nt(ql[:, sl], kl[:, sl]), 0.0)
        s //= 2
    sub = min(HGRN_SUB, c)
    ref = _bcast_rows(b, 0, sub)
    dlt = b - ref
    qd = (qf * jnp.exp(dlt)).astype(BF16)
    kd = (kk * jnp.exp(jnp.minimum(-dlt, 80.0))).astype(BF16)
    diag = jnp.logical_and((row // sub) == (col // sub), col <= row)
    qe = (qf * jnp.exp(b)).astype(BF16)
    b_last = b[c - 1:c, :]
    kend = (kk * jnp.exp(b_last - b)).astype(BF16)
    dec = jnp.exp(b_last)
    gn = gn_ref[...]
    og = og_ref[...]
    for h in range(nh):
        sl = slice(h * HEAD_DIM, (h + 1) * HEAD_DIM)
        ah = a[h] + jnp.where(diag, _dot_nt(qd[:, sl], kd[:, sl]), 0.0)
        v = vv[:, sl]
        vb = v.astype(BF16)
        st = st_ref[h]
        o = _dot_nt(qe[:, sl], st.astype(BF16)) + _dot(ah.astype(BF16), vb)
        st_new = st * dec[:, sl] + lax.dot_general(vb, kend[:, sl], (((0,), (0,)), ((), ())),
                                                   preferred_element_type=F32)
        st_ref[h] = st_new
        ms = jnp.mean(o * o, axis=-1, keepdims=True)
        o_ref[:, sl] = o * lax.rsqrt(ms + EPS) * gn * _silu(og[:, sl])

        @pl.when(t == pl.num_programs(2) - 1)
        def _(h=h, st_new=st_new):
            so_ref[h] = st_new.T


def _hgrn(proj, col0, lb_logits, gnorm, s0, out_buf, row0, batch, seq, chunk):
    n = proj.shape[0]
    nt = seq // chunk
    hb = HGRN_HEADS // ATT_HEADS
    width = HGRN_HEADS * HEAD_DIM
    cb0 = col0 // ATT_OUT
    rb0 = row0 // chunk

    def col(j):
        return pl.BlockSpec((chunk, ATT_OUT), lambda b, h, t, j=j: (rb0 + b * nt + t, cb0 + j * hb + h))

    in_specs = [col(0), col(1), col(2), col(3),
                pl.BlockSpec((lb_logits.shape[0], ATT_OUT), lambda b, h, t: (0, h)),
                pl.BlockSpec((1, HEAD_DIM), lambda b, h, t: (0, 0))]
    args = [proj, proj, proj, proj, lb_logits, gnorm.reshape(1, HEAD_DIM)]
    state_spec = pl.BlockSpec((None, ATT_HEADS, HEAD_DIM, HEAD_DIM), lambda b, h, t: (b, h, 0, 0))
    if s0 is not None:
        in_specs.append(state_spec)
        args.append(s0)
    aliases = {}
    if out_buf is not None:
        aliases = {len(args): 0}
        in_specs.append(pl.BlockSpec(memory_space=pl.ANY))
        args.append(out_buf)
    return pl.pallas_call(
        functools.partial(_hgrn_body, chunk=chunk, has_s0=s0 is not None, has_alias=out_buf is not None),
        grid=(batch, hb, nt),
        in_specs=in_specs,
        out_specs=[pl.BlockSpec((chunk, ATT_OUT), lambda b, h, t: (rb0 + b * nt + t, h)), state_spec],
        out_shape=[jax.ShapeDtypeStruct((n, width), F32),
                   jax.ShapeDtypeStruct((batch, HGRN_HEADS, HEAD_DIM, HEAD_DIM), F32)],
        scratch_shapes=[pltpu.VMEM((ATT_HEADS, HEAD_DIM, HEAD_DIM), F32)],
        input_output_aliases=aliases,
        compiler_params=_params("arbitrary", "arbitrary", "arbitrary"),
        name=f"hgrn_c{chunk}",
    )(*args)


def _gated_merge_body(att_ref, hg_ref, wa_ref, wh_ref, ga_ref, gh_ref, o_ref, wab_ref, whb_ref):
    @pl.when(pl.program_id(1) == 0)
    def _():
        wab_ref[...] = wa_ref[...].astype(BF16)
        whb_ref[...] = wh_ref[...].astype(BF16)

    pa = _dot(att_ref[...].astype(BF16), wab_ref[...])
    ph = _dot(hg_ref[...].astype(BF16), whb_ref[...])
    o_ref[...] = (_sigmoid(ga_ref[...]) * pa + _sigmoid(gh_ref[...]) * ph).astype(o_ref.dtype)


def _gated_merge(attn, hgrn, w_pa, w_ph, proj, col_ga, col_gh, tm, tn):
    m = attn.shape[0]
    n = w_pa.shape[1]
    ka, kh = w_pa.shape[0], w_ph.shape[0]
    ca, ch = col_ga // tn, col_gh // tn
    return pl.pallas_call(
        _gated_merge_body,
        grid=(n // tn, m // tm),
        in_specs=[pl.BlockSpec((tm, ka), lambda j, i: (i, 0)),
                  pl.BlockSpec((tm, kh), lambda j, i: (i, 0)),
                  pl.BlockSpec((ka, tn), lambda j, i: (0, j)),
                  pl.BlockSpec((kh, tn), lambda j, i: (0, j)),
                  pl.BlockSpec((tm, tn), lambda j, i: (i, ca + j)),
                  pl.BlockSpec((tm, tn), lambda j, i: (i, ch + j))],
        out_specs=pl.BlockSpec((tm, tn), lambda j, i: (i, j)),
        out_shape=jax.ShapeDtypeStruct((m, n), BF16),
        scratch_shapes=[pltpu.VMEM((ka, tn), BF16), pltpu.VMEM((kh, tn), BF16)],
        compiler_params=_params("arbitrary", "arbitrary"),
        name="gated_merge",
    )(attn, hgrn, w_pa, w_ph, proj, proj)


def _cross_body(q_ref, kv_ref, *rest, slab_rows):
    o_ref = rest[-1]
    scale = HEAD_DIM ** -0.5
    if slab_rows:
        mem_len = kv_ref.shape[0]
        kv2 = kv_ref.reshape(mem_len * KV_SLAB, HEAD_DIM)
    for h in range(ATT_HEADS):
        sl = slice(h * HEAD_DIM, (h + 1) * HEAD_DIM)
        q = (q_ref[:, sl] * scale).astype(BF16)
        if slab_rows:
            k = kv2[pl.ds(h, mem_len, stride=KV_SLAB), :].astype(BF16)
            v = kv2[pl.ds(ATT_HEADS + h, mem_len, stride=KV_SLAB), :].astype(BF16)
        else:
            k = kv_ref[:, sl].astype(BF16)
            v = kv_ref[:, ATT_OUT + h * HEAD_DIM: ATT_OUT + (h + 1) * HEAD_DIM].astype(BF16)
        s = _dot_nt(q, k)
        p = jnp.exp(s - jnp.max(s, axis=1, keepdims=True))
        o_ref[:, sl] = _dot(p.astype(BF16), v) / jnp.sum(p, axis=1, keepdims=True)


def _cross_attn(q, mem_kv, out_buf, row0, batch, seq, tq):
    n = q.shape[0]
    nq = seq // tq
    rb0 = row0 // tq
    slab_rows = mem_kv.ndim == 4
    kv_spec = pl.BlockSpec((None,) + mem_kv.shape[1:], lambda b, i: (b,) + (0,) * (mem_kv.ndim - 1))
    in_specs = [pl.BlockSpec((tq, ATT_OUT), lambda b, i: (rb0 + b * nq + i, 0)), kv_spec]
    args = [q, mem_kv]
    aliases = {}
    if out_buf is not None:
        aliases = {2: 0}
        in_specs.append(pl.BlockSpec(memory_space=pl.ANY))
        args.append(out_buf)
    return pl.pallas_call(
        functools.partial(_cross_body, slab_rows=slab_rows),
        grid=(batch, nq),
        in_specs=in_specs,
        out_specs=pl.BlockSpec((tq, ATT_OUT), lambda b, i: (rb0 + b * nq + i, 0)),
        out_shape=jax.ShapeDtypeStruct((n, ATT_OUT), F32),
        input_output_aliases=aliases,
        compiler_params=_params("arbitrary", "arbitrary"),
        name=f"cross_attn_t{tq}",
    )(*args)


def _router_body(x_ref, g_ref, whi_ref, wlo_ref, b_ref, tri_ref, xs_ref, route_ref, cnt_ref, base_scr):
    tm = x_ref.shape[0]

    @pl.when(pl.program_id(0) == 0)
    def _():
        base_scr[...] = jnp.zeros_like(base_scr)

    x = x_ref[...]
    ms = jnp.mean(x * x, axis=-1, keepdims=True)
    xn = x * lax.rsqrt(ms + EPS) * g_ref[...]
    for c in range(ROW_SLAB):
        xs_ref[pl.ds(c, tm, stride=ROW_SLAB), :] = xn[:, c * HEAD_DIM:(c + 1) * HEAD_DIM]
    hi = xn.astype(BF16)
    lo = (xn - hi.astype(F32)).astype(BF16)
    logits = _dot(hi, whi_ref[...]) + _dot(lo, whi_ref[...]) + _dot(hi, wlo_ref[...]) + b_ref[...]
    lane = lax.broadcasted_iota(jnp.int32, logits.shape, 1)
    big = jnp.int32(1 << 20)
    is_g = jnp.logical_and(lane >= N_EXPERTS, lane < N_EXPERTS + N_GROUPS)
    lg = jnp.where(is_g, logits, NEG)
    mg = jnp.max(lg, axis=1, keepdims=True)
    p_top = 1.0 / jnp.sum(jnp.where(is_g, jnp.exp(lg - mg), 0.0), axis=1, keepdims=True)
    g_idx = jnp.min(jnp.where(jnp.logical_and(is_g, lg == mg), lane, big), axis=1, keepdims=True) - N_EXPERTS
    in_grp = jnp.logical_and(lane < N_EXPERTS, lane // EXPERTS_PER_GROUP == g_idx)
    le = jnp.where(in_grp, logits, NEG)
    v1 = jnp.max(le, axis=1, keepdims=True)
    i1 = jnp.min(jnp.where(jnp.logical_and(in_grp, le == v1), lane, big), axis=1, keepdims=True)
    rest = jnp.logical_and(in_grp, lane != i1)
    le2 = jnp.where(rest, logits, NEG)
    v2 = jnp.max(le2, axis=1, keepdims=True)
    i2 = jnp.min(jnp.where(jnp.logical_and(rest, le2 == v2), lane, big), axis=1, keepdims=True)
    e2 = jnp.exp(v2 - v1)
    w1 = p_top / (1.0 + e2)
    w2 = p_top * e2 / (1.0 + e2)
    hit1, hit2 = lane == i1, lane == i2
    hits = jnp.where(jnp.logical_or(hit1, hit2), 1.0, 0.0)
    before = _dot(tri_ref[...], hits.astype(BF16)) + base_scr[...]
    r1 = jnp.sum(jnp.where(hit1, before, 0.0), axis=1, keepdims=True)
    r2 = jnp.sum(jnp.where(hit2, before, 0.0), axis=1, keepdims=True)
    base_scr[...] += jnp.sum(hits, axis=0, keepdims=True)
    cnt_ref[...] = base_scr[...]
    cols = (i1.astype(F32), i2.astype(F32), w1, w2, r1, r2)
    route = jnp.zeros(logits.shape, F32)
    for c, val in enumerate(cols):
        route = jnp.where(lane == c, val, route)
    route_ref[...] = route


def _router(x, g, w_hi, w_lo, bias, tm):
    m, d = x.shape
    lanes = w_hi.shape[1]
    const = lambda shape: pl.BlockSpec(shape, lambda i: (0, 0))
    tri = jnp.tril(jnp.ones((tm, tm), F32), -1).astype(BF16)
    return pl.pallas_call(
        _router_body,
        grid=(m // tm,),
        in_specs=[pl.BlockSpec((tm, d), lambda i: (i, 0)), const((1, d)), const((d, lanes)), const((d, lanes)),
                  const((1, lanes)), const((tm, tm))],
        out_specs=[pl.BlockSpec((tm * ROW_SLAB, HEAD_DIM), lambda i: (i, 0)),
                   pl.BlockSpec((tm, lanes), lambda i: (i, 0)), const((1, lanes))],
        out_shape=[jax.ShapeDtypeStruct((m * ROW_SLAB, HEAD_DIM), F32), jax.ShapeDtypeStruct((m, lanes), F32),
                   jax.ShapeDtypeStruct((1, lanes), F32)],
        scratch_shapes=[pltpu.VMEM((1, lanes), F32)],
        compiler_params=_params("arbitrary"),
        name="router",
    )(x, g.reshape(1, d), w_hi, w_lo, bias, tri)


def _row_gather_body(idx_ref, src_ref, dst_ref, sem, *, tg):
    t = pl.program_id(0)
    slot = t % 2

    def issue(p, carry):
        r = pl.multiple_of(idx_ref[0, p] * ROW_SLAB, ROW_SLAB)
        o = pl.multiple_of((t * tg + p) * ROW_SLAB, ROW_SLAB)
        pltpu.make_async_copy(src_ref.at[pl.ds(r, ROW_SLAB), :], dst_ref.at[pl.ds(o, ROW_SLAB), :],
                              sem.at[slot]).start()
        return carry

    lax.fori_loop(0, tg, issue, 0)

    def wait_step(step, s):
        o = pl.multiple_of(step * tg * ROW_SLAB, ROW_SLAB)
        pltpu.make_async_copy(src_ref.at[pl.ds(0, tg * ROW_SLAB), :], dst_ref.at[pl.ds(o, tg * ROW_SLAB), :],
                              sem.at[s]).wait()

    @pl.when(t > 0)
    def _():
        wait_step(t - 1, 1 - slot)

    @pl.when(t == pl.num_programs(0) - 1)
    def _():
        wait_step(t, slot)


def _row_gather(src, idx, tg):
    p = idx.shape[0]
    return pl.pallas_call(
        functools.partial(_row_gather_body, tg=tg),
        grid=(p // tg,),
        in_specs=[pl.BlockSpec((None, 1, tg), lambda t: (t, 0, 0), memory_space=pltpu.SMEM),
                  pl.BlockSpec(memory_space=pl.ANY)],
        out_specs=pl.BlockSpec(memory_space=pl.ANY),
        out_shape=jax.ShapeDtypeStruct((p * ROW_SLAB, HEAD_DIM), F32),
        scratch_shapes=[pltpu.SemaphoreType.DMA((2,))],
        compiler_params=_params("arbitrary"),
        name=f"row_gather_{p}",
    )(idx.reshape(p // tg, 1, tg), src)


def _slab_to_rows(ref, rows):
    return jnp.concatenate([ref[pl.ds(c, rows, stride=ROW_SLAB), :] for c in range(ROW_SLAB)], axis=1)


def _experts_body(te_ref, na_ref, xs_ref, wg_ref, wu_ref, wd_ref, o_ref, wgb, wub, wdb, *, te_rows):
    t = pl.program_id(0)
    active = t < na_ref[0]
    changed = jnp.logical_or(t == 0, te_ref[t] != te_ref[jnp.maximum(t - 1, 0)])

    @pl.when(jnp.logical_and(active, changed))
    def _():
        wgb[...] = wg_ref[...].astype(BF16)
        wub[...] = wu_ref[...].astype(BF16)
        wdb[...] = wd_ref[...].astype(BF16)

    @pl.when(active)
    def _():
        x = _slab_to_rows(xs_ref, te_rows).astype(BF16)
        h = _silu(_dot(x, wgb[...])) * _dot(x, wub[...])
        o = _dot(h.astype(BF16), wdb[...])
        for c in range(ROW_SLAB):
            o_ref[pl.ds(c, te_rows, stride=ROW_SLAB), :] = o[:, c * HEAD_DIM:(c + 1) * HEAD_DIM]

    @pl.when(jnp.logical_not(active))
    def _():
        o_ref[...] = jnp.zeros_like(o_ref)


def _experts(xs, tile_expert, n_active, w_gate, w_up, w_down, te_rows):
    ne, d, ff = w_gate.shape
    nt = tile_expert.shape[0]
    rows = pl.BlockSpec((te_rows * ROW_SLAB, HEAD_DIM), lambda t, te, na: (t, 0))
    grid_spec = pltpu.PrefetchScalarGridSpec(
        num_scalar_prefetch=2,
        grid=(nt,),
        in_specs=[rows,
                  pl.BlockSpec((None, d, ff), lambda t, te, na: (te[t], 0, 0)),
                  pl.BlockSpec((None, d, ff), lambda t, te, na: (te[t], 0, 0)),
                  pl.BlockSpec((None, ff, d), lambda t, te, na: (te[t], 0, 0))],
        out_specs=rows,
        scratch_shapes=[pltpu.VMEM((d, ff), BF16), pltpu.VMEM((d, ff), BF16), pltpu.VMEM((ff, d), BF16)],
    )
    return pl.pallas_call(
        functools.partial(_experts_body, te_rows=te_rows),
        grid_spec=grid_spec,
        out_shape=jax.ShapeDtypeStruct(xs.shape, F32),
        compiler_params=_params("arbitrary"),
        name="experts",
    )(tile_expert, n_active, xs, w_gate, w_up, w_down)


def _moe_out_body(x_ref, route_ref, d1_ref, d2_ref, gf_ref, yp_ref, ys_ref, *, n_prompt_tiles):
    i = pl.program_id(0)
    tm = x_ref.shape[0]
    route = route_ref[...]
    w1, w2 = route[:, 2:3], route[:, 3:4]
    x = x_ref[...] + w1 * _slab_to_rows(d1_ref, tm) + w2 * _slab_to_rows(d2_ref, tm)
    ms = jnp.mean(x * x, axis=-1, keepdims=True)
    y = x * lax.rsqrt(ms + EPS) * gf_ref[...]

    @pl.when(i < n_prompt_tiles)
    def _():
        yp_ref[...] = y

    @pl.when(i >= n_prompt_tiles)
    def _():
        ys_ref[...] = y


def _moe_out(x, route, dd, g_final, n_prompt, tm):
    m, d = x.shape
    nt = m // tm
    npt = n_prompt // tm
    lanes = route.shape[1]
    slab = lambda off: pl.BlockSpec((tm * ROW_SLAB, HEAD_DIM), lambda i, off=off: (off + i, 0))
    return pl.pallas_call(
        functools.partial(_moe_out_body, n_prompt_tiles=npt),
        grid=(nt,),
        in_specs=[pl.BlockSpec((tm, d), lambda i: (i, 0)), pl.BlockSpec((tm, lanes), lambda i: (i, 0)),
                  slab(0), slab(nt), pl.BlockSpec((1, d), lambda i: (0, 0))],
        out_specs=[pl.BlockSpec((tm, d), lambda i: (jnp.minimum(i, npt - 1), 0)),
                   pl.BlockSpec((tm, d), lambda i: (jnp.maximum(i - npt, 0), 0))],
        out_shape=[jax.ShapeDtypeStruct((n_prompt, d), F32), jax.ShapeDtypeStruct((m - n_prompt, d), F32)],
        compiler_params=_params("arbitrary"),
        name="moe_out",
    )(x, route, dd, dd, g_final.reshape(1, d))


def _moe(x, g_ffn, w_rg, b_rg, w_re, b_re, w_gate, w_up, w_down, g_final, n_prompt):
    m, d = x.shape
    lanes = HEAD_DIM
    npad = lanes - N_EXPERTS - N_GROUPS
    w_r = jnp.concatenate([w_re, w_rg, jnp.zeros((d, npad), F32)], axis=1)
    b_r = jnp.concatenate([b_re, b_rg, jnp.zeros((npad,), F32)]).reshape(1, lanes)
    w_r_hi = w_r.astype(BF16)
    w_r_lo = (w_r - w_r_hi.astype(F32)).astype(BF16)
    xs, route, counts = _router(x, g_ffn, w_r_hi, w_r_lo, b_r, 512)

    te = MOE_TILE
    n_tiles = -(-(2 * m + N_EXPERTS * (te - 1)) // te)
    n_tiles += n_tiles % 2
    counts = counts[0, :N_EXPERTS].astype(jnp.int32)
    padded = (counts + te - 1) // te * te
    pad_end = jnp.cumsum(padded)
    pad_off = pad_end - padded
    experts = jnp.arange(N_EXPERTS, dtype=jnp.int32)

    def dest(col_e, col_r):
        e = route[:, col_e].astype(jnp.int32)
        off = jnp.sum(jnp.where(e[:, None] == experts[None, :], pad_off[None, :], 0), axis=1)
        return off + route[:, col_r].astype(jnp.int32)

    pos = jnp.concatenate([dest(0, 4), dest(1, 5)])
    tok = jnp.tile(jnp.arange(m, dtype=jnp.int32), 2)
    src_tok = jnp.zeros((n_tiles * te,), jnp.int32).at[pos].set(tok)
    tile_start = jnp.arange(n_tiles, dtype=jnp.int32) * te
    tile_expert = jnp.minimum(jnp.sum(tile_start[:, None] >= pad_end[None, :], axis=1), N_EXPERTS - 1)
    n_active = (pad_end[-1] // te).reshape(1)

    xs_sorted = _row_gather(xs, src_tok, 2 * te)
    ys_sorted = _experts(xs_sorted, tile_expert.astype(jnp.int32), n_active.astype(jnp.int32),
                         w_gate, w_up, w_down, te)
    dd = _row_gather(ys_sorted, pos, 512)
    return _moe_out(x, route, dd, g_final, n_prompt, 256)


def _rope_tables(positions):
    half = HEAD_DIM // 2
    inv_freq = ROPE_THETA ** (-jnp.arange(half, dtype=F32) / half)
    ang = positions.astype(F32)[:, None] * inv_freq[None, :]
    cos, sin = jnp.cos(ang), jnp.sin(ang)
    return jnp.concatenate([cos, cos], axis=1), jnp.concatenate([-sin, sin], axis=1)


def kernel(x_prompt, x_sample, cache_swa1, cache_swa2, cache_swa3, state_hgrn, cache_mem_kv, mem_prompt,
           hgrn_lb_logits, norm_mix, w_in, w_proj_attn, w_proj_hgrn, w_out, hgrn_norm, norm_cross, norm_mem,
           w_cq, w_ckv, w_co, norm_ffn, w_rg, b_rg, w_re, b_re, w_e_gate, w_e_up, w_e_down, norm_final):
    bp, seq, d = x_prompt.shape
    bs, dseq, _ = x_sample.shape
    depth = w_in.shape[0]
    assert depth == 1, "single-layer trunk"
    past = cache_swa3.shape[2]
    mem_len = mem_prompt.shape[1]
    np_, ns = bp * seq, bs * dseq
    hw = HGRN_HEADS * HEAD_DIM
    col_hgrn = 3 * ATT_WIDTH
    col_ga = col_hgrn + 4 * hw
    col_gh = col_ga + d
    l = 0

    x_all = jnp.concatenate([x_prompt.reshape(np_, d), x_sample.reshape(ns, d)], axis=0)
    pos = jnp.concatenate([jnp.tile(jnp.arange(seq, dtype=jnp.int32), bp),
                           jnp.tile(past + jnp.arange(dseq, dtype=jnp.int32), bs)])
    cos2, sin2 = _rope_tables(pos)

    xn = _rmsnorm(x_all, norm_mix[l], BF16, 512)
    proj = _matmul(xn, w_in[l], F32, 512, 1536, name="proj_in")

    q_rot, kv_all = _rope(proj, cos2, sin2, 256)
    attn = _attn_prompt(q_rot, kv_all, bp, seq, 256)
    attn = _attn_sample(q_rot, kv_all, (cache_swa1[l], cache_swa2[l], cache_swa3[l]), attn, np_, bs, dseq)

    hgrn, st_p = _hgrn(proj, col_hgrn, hgrn_lb_logits, hgrn_norm[l], None, None, 0, bp, seq, 128)
    hgrn, st_s = _hgrn(proj, col_hgrn, hgrn_lb_logits, hgrn_norm[l], state_hgrn[l], hgrn, np_, bs, dseq, dseq)

    merged = _gated_merge(attn, hgrn, w_proj_attn[l], w_proj_hgrn[l], proj, col_ga, col_gh, 512, 512)
    x1 = _matmul(merged, w_out[l], F32, 512, 1024, res=x_all, name="proj_out")

    mem_n = _rmsnorm(mem_prompt.reshape(bp * mem_len, d), norm_mem[l], BF16, 256)
    mkv_p = _matmul(mem_n, w_ckv[l], F32, 256, KV_ROW, name="mem_kv")
    xc = _rmsnorm(x1, norm_cross[l], BF16, 512)
    qc = _matmul(xc, w_cq[l], F32, 512, ATT_OUT, name="cross_q")
    oc = _cross_attn(qc, mkv_p.reshape(bp, mem_len, KV_ROW), None, 0, bp, seq, 512)
    oc = _cross_attn(qc, cache_mem_kv[l].reshape(bs, mem_len, KV_SLAB, HEAD_DIM), oc, np_, bs, dseq, dseq)
    x2 = _matmul(oc, w_co[l], F32, 512, 1024, res=x1, name="cross_out")

    ff = w_e_gate.shape[-1]
    y_p, y_s = _moe(x2, norm_ffn[l], w_rg[l], b_rg[l], w_re[l], b_re[l],
                    w_e_gate[l].reshape(N_EXPERTS, d, ff), w_e_up[l].reshape(N_EXPERTS, d, ff),
                    w_e_down[l].reshape(N_EXPERTS, ff, d), norm_final, np_)

    swa_p = [_kv_rows(kv_all, g, 0, bp, seq, min(w, seq), min(w, seq, 256))[None]
             for g, (w, _) in enumerate(ATT_GROUPS)]
    swa_s = [_kv_rows(kv_all, g, np_, 1, ns, ns, 256).reshape(1, bs, dseq, 2, ATT_HEADS, HEAD_DIM)
             for g in range(len(ATT_GROUPS))]
    return (y_p.reshape(bp, seq, d), y_s.reshape(bs, dseq, d),
            swa_p[0], swa_p[1], swa_p[2], st_p[None],
            mkv_p.reshape(1, bp, mem_len, 2, ATT_HEADS, HEAD_DIM),
            swa_s[0], swa_s[1], swa_s[2], st_s[None])
```

```python
import functools

import jax
import jax.numpy as jnp
from jax import lax
from jax.experimental import pallas as pl
from jax.experimental.pallas import tpu as pltpu

F32 = jnp.float32
BF16 = jnp.bfloat16

ATT_GROUPS = ((128, 1), (512, 4), (2048, 16))
ATT_HEADS = 4
HEAD_DIM = 128
ATT_OUT = ATT_HEADS * HEAD_DIM
ATT_WIDTH = len(ATT_GROUPS) * ATT_OUT
KV_ROW = 2 * ATT_OUT
KV_SLAB = 2 * ATT_HEADS
HGRN_HEADS = 16
ROPE_THETA = 10000.0
EPS = 1e-6
N_EXPERTS = 32
N_GROUPS = 4
EXPERTS_PER_GROUP = 8
NEG = -1e30
ATT_BLOCK = 128
HGRN_SUB = 8
ROW_SLAB = 16
MOE_TILE = 256
VMEM_LIMIT = 56 * 1024 * 1024


def _params(*sem):
    return pltpu.CompilerParams(dimension_semantics=sem, vmem_limit_bytes=VMEM_LIMIT)


def _dot(a, b):
    return jnp.dot(a, b, preferred_element_type=F32)


def _dot_nt(a, b):
    return lax.dot_general(a, b, (((1,), (1,)), ((), ())), preferred_element_type=F32)


def _sigmoid(x):
    return 1.0 / (1.0 + jnp.exp(-x))


def _silu(x):
    return x * _sigmoid(x)


def _rmsnorm_body(x_ref, g_ref, o_ref):
    x = x_ref[...]
    ms = jnp.mean(x * x, axis=-1, keepdims=True)
    o_ref[...] = (x * lax.rsqrt(ms + EPS) * g_ref[...]).astype(o_ref.dtype)


def _rmsnorm(x, g, out_dtype, tm):
    m, d = x.shape
    return pl.pallas_call(
        _rmsnorm_body,
        grid=(m // tm,),
        in_specs=[pl.BlockSpec((tm, d), lambda i: (i, 0)), pl.BlockSpec((1, d), lambda i: (0, 0))],
        out_specs=pl.BlockSpec((tm, d), lambda i: (i, 0)),
        out_shape=jax.ShapeDtypeStruct((m, d), out_dtype),
        compiler_params=_params("arbitrary"),
        name="rmsnorm",
    )(x, g.reshape(1, d))


def _mm_body(*refs, has_res):
    if has_res:
        a_ref, w_ref, r_ref, o_ref, wb_ref = refs
    else:
        a_ref, w_ref, o_ref, wb_ref = refs

    @pl.when(pl.program_id(1) == 0)
    def _():
        wb_ref[...] = w_ref[...].astype(BF16)

    acc = _dot(a_ref[...].astype(BF16), wb_ref[...])
    if has_res:
        acc = acc + r_ref[...]
    o_ref[...] = acc.astype(o_ref.dtype)


def _matmul(a, w, out_dtype, tm, tn, res=None, name="matmul"):
    m, k = a.shape
    n = w.shape[1]
    in_specs = [pl.BlockSpec((tm, k), lambda j, i: (i, 0)), pl.BlockSpec((k, tn), lambda j, i: (0, j))]
    args = [a, w]
    if res is not None:
        in_specs.append(pl.BlockSpec((tm, tn), lambda j, i: (i, j)))
        args.append(res)
    return pl.pallas_call(
        functools.partial(_mm_body, has_res=res is not None),
        grid=(n // tn, m // tm),
        in_specs=in_specs,
        out_specs=pl.BlockSpec((tm, tn), lambda j, i: (i, j)),
        out_shape=jax.ShapeDtypeStruct((m, n), out_dtype),
        scratch_shapes=[pltpu.VMEM((k, tn), BF16)],
        compiler_params=_params("arbitrary", "arbitrary"),
        name=name,
    )(*args)


def _rope_body(q_ref, k_ref, v_ref, cos_ref, sin_ref, qo_ref, kv_ref):
    cos = cos_ref[...]
    sin = sin_ref[...]
    scale = HEAD_DIM ** -0.5
    for j in range(ATT_WIDTH // HEAD_DIM):
        sl = slice(j * HEAD_DIM, (j + 1) * HEAD_DIM)
        q = q_ref[:, sl]
        qo_ref[:, sl] = (q * cos + pltpu.roll(q, HEAD_DIM // 2, axis=1) * sin) * scale
        k = k_ref[:, sl]
        g, h = divmod(j, ATT_HEADS)
        kv_ref[:, g * KV_ROW + h * HEAD_DIM: g * KV_ROW + (h + 1) * HEAD_DIM] = (
            k * cos + pltpu.roll(k, HEAD_DIM // 2, axis=1) * sin)
    for g in range(len(ATT_GROUPS)):
        kv_ref[:, g * KV_ROW + ATT_OUT: (g + 1) * KV_ROW] = v_ref[:, g * ATT_OUT: (g + 1) * ATT_OUT]


def _rope(proj, cos2, sin2, tm):
    n = proj.shape[0]
    blk = lambda c: pl.BlockSpec((tm, ATT_WIDTH), lambda i, c=c: (i, c))
    tab = pl.BlockSpec((tm, HEAD_DIM), lambda i: (i, 0))
    return pl.pallas_call(
        _rope_body,
        grid=(n // tm,),
        in_specs=[blk(0), blk(1), blk(2), tab, tab],
        out_specs=[pl.BlockSpec((tm, ATT_WIDTH), lambda i: (i, 0)),
                   pl.BlockSpec((tm, len(ATT_GROUPS) * KV_ROW), lambda i: (i, 0))],
        out_shape=[jax.ShapeDtypeStruct((n, ATT_WIDTH), F32),
                   jax.ShapeDtypeStruct((n, len(ATT_GROUPS) * KV_ROW), F32)],
        compiler_params=_params("arbitrary"),
        name="rope",
    )(proj, proj, proj, cos2, sin2)


def _prompt_key_blocks(bq):
    table = []
    for g, (w, _) in enumerate(ATT_GROUPS):
        nback = -(-w // bq)
        table.extend((g, back) for back in range(nback, -1, -1))
    return tuple(table)


def _attn_prompt_body(q_ref, k_ref, v_ref, o_ref, m_scr, l_scr, acc_scr, *, bq, table):
    i = pl.program_id(1)
    j = pl.program_id(2)
    ng = len(ATT_GROUPS)

    @pl.when(j == 0)
    def _():
        m_scr[...] = jnp.full(m_scr.shape, NEG, F32)
        l_scr[...] = jnp.zeros(l_scr.shape, F32)
        acc_scr[...] = jnp.zeros(acc_scr.shape, F32)

    row = lax.broadcasted_iota(jnp.int32, (bq, bq), 0)
    col = lax.broadcasted_iota(jnp.int32, (bq, bq), 1)
    for g, (w, d) in enumerate(ATT_GROUPS):
        j0 = min(jj for jj, (gg, _) in enumerate(table) if gg == g)
        nback = max(back for gg, back in table if gg == g)
        back = nback - (j - j0)

        @pl.when(jnp.logical_and(jnp.logical_and(j >= j0, j <= j0 + nback), i >= back))
        def _(g=g, w=w, d=d, back=back):
            dist = back * bq + row - col
            valid = jnp.logical_and(jnp.logical_and(dist >= 0, dist <= w), (dist & (d - 1)) == 0)
            bias = jnp.where(valid, 0.0, NEG)
            heads = range(ATT_HEADS)
            hs = [slice(h * HEAD_DIM, (h + 1) * HEAD_DIM) for h in heads]
            m_old = [m_scr[g * ATT_HEADS + h] for h in heads]
            l_old = [l_scr[g * ATT_HEADS + h] for h in heads]
            a_old = [acc_scr[g, :, hs[h]] for h in heads]
            qs = [q_ref[:, g * ATT_OUT + h * HEAD_DIM: g * ATT_OUT + (h + 1) * HEAD_DIM].astype(BF16) for h in heads]
            ss = [_dot_nt(qs[h], k_ref[:, hs[h]].astype(BF16)) + bias for h in heads]
            m_new = [jnp.maximum(m_old[h], jnp.max(ss[h], axis=1, keepdims=True)) for h in heads]
            ps = [jnp.exp(ss[h] - jnp.concatenate([m_new[h]] * (bq // HEAD_DIM), axis=1)) for h in heads]
            alpha = [jnp.exp(m_old[h] - m_new[h]) for h in heads]
            pv = [_dot(ps[h].astype(BF16), v_ref[:, hs[h]].astype(BF16)) for h in heads]
            for h in heads:
                m_scr[g * ATT_HEADS + h] = m_new[h]
                l_scr[g * ATT_HEADS + h] = alpha[h] * l_old[h] + jnp.sum(ps[h], axis=1, keepdims=True)
                acc_scr[g, :, hs[h]] = alpha[h] * a_old[h] + pv[h]

    @pl.when(j == pl.num_programs(2) - 1)
    def _():
        for h in range(ATT_HEADS):
            sl = slice(h * HEAD_DIM, (h + 1) * HEAD_DIM)
            lse = [m_scr[g * ATT_HEADS + h] + jnp.log(l_scr[g * ATT_HEADS + h]) for g in range(ng)]
            mx = functools.reduce(jnp.maximum, lse)
            ws = [jnp.exp(x - mx) for x in lse]
            num = sum(ws[g] / l_scr[g * ATT_HEADS + h] * acc_scr[g, :, sl] for g in range(ng))
            o_ref[:, sl] = num / sum(ws)


def _attn_prompt(q_rot, kv_all, batch, seq, bq):
    n = q_rot.shape[0]
    ng = len(ATT_GROUPS)
    nq = seq // bq
    table = _prompt_key_blocks(bq)
    groups = jnp.asarray([g for g, _ in table], jnp.int32)
    backs = jnp.asarray([b for _, b in table], jnp.int32)

    def kv_map(half):
        def index(b, i, j, g_ref, back_ref):
            return (b * nq + jnp.maximum(i - back_ref[j], 0), 2 * g_ref[j] + half)
        return index

    grid_spec = pltpu.PrefetchScalarGridSpec(
        num_scalar_prefetch=2,
        grid=(batch, nq, len(table)),
        in_specs=[pl.BlockSpec((bq, ATT_WIDTH), lambda b, i, j, g_ref, back_ref: (b * nq + i, 0)),
                  pl.BlockSpec((bq, ATT_OUT), kv_map(0)),
                  pl.BlockSpec((bq, ATT_OUT), kv_map(1))],
        out_specs=pl.BlockSpec((bq, ATT_OUT), lambda b, i, j, g_ref, back_ref: (b * nq + i, 0)),
        scratch_shapes=[pltpu.VMEM((ng * ATT_HEADS, bq, HEAD_DIM), F32),
                        pltpu.VMEM((ng * ATT_HEADS, bq, HEAD_DIM), F32),
                        pltpu.VMEM((ng, bq, ATT_OUT), F32)],
    )

    def body(g_ref, back_ref, *refs):
        del g_ref, back_ref
        _attn_prompt_body(*refs, bq=bq, table=table)

    return pl.pallas_call(
        body,
        grid_spec=grid_spec,
        out_shape=jax.ShapeDtypeStruct((n, ATT_OUT), F32),
        compiler_params=_params("arbitrary", "arbitrary", "arbitrary"),
        name="attn_prompt",
    )(groups, backs, q_rot, kv_all, kv_all)


def _head_lanes(ref2d, first, rows):
    return jnp.concatenate([ref2d[pl.ds(first + h, rows, stride=KV_SLAB), :] for h in range(ATT_HEADS)], axis=1)


def _attn_sample_body(q_ref, kvn_ref, *rest, tq):
    cache_refs, o_ref = rest[:-2], rest[-1]
    nrow = ATT_HEADS * tq
    nkey = ATT_BLOCK
    rid = lax.broadcasted_iota(jnp.int32, (nrow, 1), 0)
    i_row = rid % tq
    head_row = rid // tq
    lane_head = lax.broadcasted_iota(jnp.int32, (1, ATT_OUT), 1) // HEAD_DIM
    head_mask = head_row == lane_head
    key = lax.broadcasted_iota(jnp.int32, (1, nkey), 1)
    pad = jnp.zeros((nkey - tq, ATT_OUT), F32)
    outs, lses = [], []
    ci = 0
    for g, (_, d) in enumerate(ATT_GROUPS):
        qg = q_ref[:, g * ATT_OUT:(g + 1) * ATT_OUT]
        qrows = jnp.where(head_mask, jnp.concatenate([qg] * ATT_HEADS, axis=0), 0.0).astype(BF16)
        k_new = jnp.concatenate([kvn_ref[:, g * KV_ROW: g * KV_ROW + ATT_OUT], pad], axis=0)
        v_new = jnp.concatenate([kvn_ref[:, g * KV_ROW + ATT_OUT: (g + 1) * KV_ROW], pad], axis=0)
        valid_new = jnp.logical_and(jnp.logical_and(key < tq, key <= i_row), (key % d) == (i_row % d))
        blocks = [(k_new, v_new, valid_new)]
        for r in range(min(d, tq)):
            c2 = cache_refs[ci].reshape(nkey * KV_SLAB, HEAD_DIM)
            ci += 1
            valid = jnp.logical_and((i_row % d) == r, key >= i_row // d)
            blocks.append((_head_lanes(c2, 0, nkey), _head_lanes(c2, ATT_HEADS, nkey), valid))
        ss = [jnp.where(valid, _dot_nt(qrows, kb.astype(BF16)), NEG) for kb, _, valid in blocks]
        m = functools.reduce(jnp.maximum, [jnp.max(s, axis=1, keepdims=True) for s in ss])
        ps = [jnp.exp(s - m) for s in ss]
        l = sum(jnp.sum(p, axis=1, keepdims=True) for p in ps)
        acc = sum(_dot(p.astype(BF16), vb.astype(BF16)) for p, (_, vb, _) in zip(ps, blocks))
        o = acc / l
        lse = m + jnp.log(l)
        outs.append(jnp.concatenate(
            [o[h * tq:(h + 1) * tq, h * HEAD_DIM:(h + 1) * HEAD_DIM] for h in range(ATT_HEADS)], axis=1))
        lses.append(jnp.concatenate(
            [jnp.broadcast_to(lse[h * tq:(h + 1) * tq], (tq, HEAD_DIM)) for h in range(ATT_HEADS)], axis=1))
    mx = jnp.maximum(jnp.maximum(lses[0], lses[1]), lses[2])
    ws = [jnp.exp(x - mx) for x in lses]
    o_ref[...] = (ws[0] * outs[0] + ws[1] * outs[1] + ws[2] * outs[2]) / (ws[0] + ws[1] + ws[2])


def _attn_sample(q_rot, kv_all, caches, attn_buf, row0, batch, tq):
    ng = len(ATT_GROUPS)
    blk0 = row0 // tq
    views, specs = [], []
    for (w, d), c in zip(ATT_GROUPS, caches):
        assert c.shape[1] == w and w // d == ATT_BLOCK, "window buffers must hold exactly one window"
        view = c.reshape(batch, w // d, d, KV_SLAB, HEAD_DIM)
        for r in range(min(d, tq)):
            views.append(view)
            specs.append(pl.BlockSpec((None, w // d, None, KV_SLAB, HEAD_DIM), lambda b, r=r: (b, 0, r, 0, 0)))
    n_in = 2 + len(views)
    return pl.pallas_call(
        functools.partial(_attn_sample_body, tq=tq),
        grid=(batch,),
        in_specs=[pl.BlockSpec((tq, ATT_WIDTH), lambda b: (blk0 + b, 0)),
                  pl.BlockSpec((tq, ng * KV_ROW), lambda b: (blk0 + b, 0))] + specs
                 + [pl.BlockSpec(memory_space=pl.ANY)],
        out_specs=pl.BlockSpec((tq, ATT_OUT), lambda b: (blk0 + b, 0)),
        out_shape=jax.ShapeDtypeStruct(attn_buf.shape, F32),
        input_output_aliases={n_in: 0},
        compiler_params=_params("arbitrary"),
        name="attn_sample",
    )(q_rot, kv_all, *views, attn_buf)


def _kv_rows_body(kv_ref, o_ref):
    tm = kv_ref.shape[0]
    for s in range(KV_SLAB):
        o_ref[pl.ds(s, tm, stride=KV_SLAB), :] = kv_ref[:, s * HEAD_DIM:(s + 1) * HEAD_DIM]


def _kv_rows(kv_all, g, row0, batch, seq, keep, tm):
    nt = keep // tm
    rb0 = (row0 + seq - keep) // tm
    per_seq = seq // tm
    out = pl.pallas_call(
        _kv_rows_body,
        grid=(batch, nt),
        in_specs=[pl.BlockSpec((tm, KV_ROW), lambda b, t: (rb0 + b * per_seq + t, g))],
        out_specs=pl.BlockSpec((tm * KV_SLAB, HEAD_DIM), lambda b, t: (b * nt + t, 0)),
        out_shape=jax.ShapeDtypeStruct((batch * keep * KV_SLAB, HEAD_DIM), F32),
        compiler_params=_params("arbitrary", "arbitrary"),
        name=f"kv_rows_g{g}_{tm}",
    )(kv_all)
    return out.reshape(batch, keep, 2, ATT_HEADS, HEAD_DIM)


def _cumsum_rows(x, seg):
    row = lax.broadcasted_iota(jnp.int32, (x.shape[0], 1), 0) % seg
    sh = 1
    while sh < seg:
        x = x + jnp.where(row >= sh, pltpu.roll(x, sh, axis=0), 0.0)
        sh *= 2
    return x


def _bcast_rows(b, first, period):
    c, w = b.shape
    parts = [jnp.broadcast_to(b[p * period + first: p * period + first + 1, :], (period, w))
             for p in range(c // period)]
    return parts[0] if len(parts) == 1 else jnp.concatenate(parts, axis=0)


def _hgrn_body(*refs, chunk, nseq, has_s0):
    refs = list(refs)
    qh_ref, fh_ref, ih_ref, og_ref, lbl_ref, gn_ref = refs[:6]
    s0_ref = refs[6] if has_s0 else None
    o_ref, so_ref, st_ref = refs[-3:]
    c = chunk
    nh = ATT_HEADS
    t = pl.program_id(2)

    @pl.when(t == 0)
    def _():
        for q in range(nseq):
            for h in range(nh):
                if has_s0:
                    st_ref[q * nh + h] = s0_ref[q, h].T
                else:
                    st_ref[q * nh + h] = jnp.zeros((HEAD_DIM, HEAD_DIM), F32)

    lbl = lbl_ref[...]
    e = jnp.exp(lbl - jnp.max(lbl, axis=0, keepdims=True))
    lb = e[0:1, :] / jnp.sum(e, axis=0, keepdims=True)
    f = lb + (1.0 - lb) * _sigmoid(fh_ref[...])
    kk = 1.0 - f
    qf = _silu(qh_ref[...])
    b = _cumsum_rows(jnp.log(f), c)
    vv = ih_ref[...]

    row = lax.broadcasted_iota(jnp.int32, (c, c), 0)
    col = lax.broadcasted_iota(jnp.int32, (c, c), 1)
    rid = lax.broadcasted_iota(jnp.int32, (nseq * c, 1), 0)
    seqs = [slice(q * c, (q + 1) * c) for q in range(nseq)]
    heads = [slice(h * HEAD_DIM, (h + 1) * HEAD_DIM) for h in range(nh)]

    a = [[jnp.zeros((c, c), F32) for _ in range(nh)] for _ in range(nseq)]
    s = c // 2
    while s >= HGRN_SUB:
        ref = _bcast_rows(b, s - 1, 2 * s)
        second = (rid % (2 * s)) >= s
        dlt = b - ref
        ee = jnp.exp(jnp.where(second, dlt, -dlt))
        ql = jnp.where(second, qf * ee, 0.0).astype(BF16)
        kl = jnp.where(second, 0.0, kk * ee).astype(BF16)
        same = (row // (2 * s)) == (col // (2 * s))
        for q in range(nseq):
            for h in range(nh):
                a[q][h] = a[q][h] + jnp.where(same, _dot_nt(ql[seqs[q], heads[h]], kl[seqs[q], heads[h]]), 0.0)
        s //= 2
    sub = min(HGRN_SUB, c)
    ref = _bcast_rows(b, 0, sub)
    dlt = b - ref
    qd = (qf * jnp.exp(dlt)).astype(BF16)
    kd = (kk * jnp.exp(jnp.minimum(-dlt, 80.0))).astype(BF16)
    diag = jnp.logical_and((row // sub) == (col // sub), col <= row)
    qe = (qf * jnp.exp(b)).astype(BF16)
    b_last = _bcast_rows(b, c - 1, c)
    kend = (kk * jnp.exp(b_last - b)).astype(BF16)
    dec = jnp.exp(b_last)
    gn = gn_ref[...]
    gate = _silu(og_ref[...])
    vb = vv.astype(BF16)
    for q in range(nseq):
        for h in range(nh):
            rs, sl = seqs[q], heads[h]
            ah = a[q][h] + jnp.where(diag, _dot_nt(qd[rs, sl], kd[rs, sl]), 0.0)
            st = st_ref[q * nh + h]
            o = _dot_nt(qe[rs, sl], st.astype(BF16)) + _dot(ah.astype(BF16), vb[rs, sl])
            st_new = st * dec[q * c:q * c + 1, sl] + lax.dot_general(
                vb[rs, sl], kend[rs, sl], (((0,), (0,)), ((), ())), preferred_element_type=F32)
            st_ref[q * nh + h] = st_new
            ms = jnp.mean(o * o, axis=-1, keepdims=True)
            o_ref[rs, sl] = (o * lax.rsqrt(ms + EPS) * gn * gate[rs, sl]).astype(o_ref.dtype)

            @pl.when(t == pl.num_programs(2) - 1)
            def _(q=q, h=h, st_new=st_new):
                so_ref[q, h] = st_new.T


def _hgrn(proj, col0, lb_logits, gnorm, s0, out_buf, row0, batch, seq, chunk, nseq):
    n = proj.shape[0]
    nt = seq // chunk
    assert nseq == 1 or nt == 1, "several sequences per step only when each is a single chunk"
    hb = HGRN_HEADS // ATT_HEADS
    width = HGRN_HEADS * HEAD_DIM
    cb0 = col0 // ATT_OUT
    rows = nseq * chunk
    rb0 = row0 // rows

    def col(j):
        return pl.BlockSpec((rows, ATT_OUT), lambda b, h, t, j=j: (rb0 + b * nt + t, cb0 + j * hb + h))

    in_specs = [col(0), col(1), col(2), col(3),
                pl.BlockSpec((lb_logits.shape[0], ATT_OUT), lambda b, h, t: (0, h)),
                pl.BlockSpec((1, HEAD_DIM), lambda b, h, t: (0, 0))]
    args = [proj, proj, proj, proj, lb_logits, gnorm.reshape(1, HEAD_DIM)]
    state_spec = pl.BlockSpec((nseq, ATT_HEADS, HEAD_DIM, HEAD_DIM), lambda b, h, t: (b, h, 0, 0))
    if s0 is not None:
        in_specs.append(state_spec)
        args.append(s0)
    aliases = {}
    if out_buf is not None:
        aliases = {len(args): 0}
        in_specs.append(pl.BlockSpec(memory_space=pl.ANY))
        args.append(out_buf)
    return pl.pallas_call(
        functools.partial(_hgrn_body, chunk=chunk, nseq=nseq, has_s0=s0 is not None),
        grid=(batch // nseq, hb, nt),
        in_specs=in_specs,
        out_specs=[pl.BlockSpec((rows, ATT_OUT), lambda b, h, t: (rb0 + b * nt + t, h)), state_spec],
        out_shape=[jax.ShapeDtypeStruct((n, width), F32),
                   jax.ShapeDtypeStruct((batch, HGRN_HEADS, HEAD_DIM, HEAD_DIM), F32)],
        scratch_shapes=[pltpu.VMEM((nseq * ATT_HEADS, HEAD_DIM, HEAD_DIM), F32)],
        input_output_aliases=aliases,
        compiler_params=_params("arbitrary", "arbitrary", "arbitrary"),
        name=f"hgrn_c{chunk}",
    )(*args)


def _gated_merge_body(att_ref, hg_ref, wa_ref, wh_ref, ga_ref, gh_ref, o_ref, wab_ref, whb_ref):
    @pl.when(pl.program_id(1) == 0)
    def _():
        wab_ref[...] = wa_ref[...].astype(BF16)
        whb_ref[...] = wh_ref[...].astype(BF16)

    pa = _dot(att_ref[...].astype(BF16), wab_ref[...])
    ph = _dot(hg_ref[...].astype(BF16), whb_ref[...])
    o_ref[...] = (_sigmoid(ga_ref[...]) * pa + _sigmoid(gh_ref[...]) * ph).astype(o_ref.dtype)


def _gated_merge(attn, hgrn, w_pa, w_ph, proj, col_ga, col_gh, tm, tn):
    m = attn.shape[0]
    n = w_pa.shape[1]
    ka, kh = w_pa.shape[0], w_ph.shape[0]
    assert col_ga % tn == 0 and col_gh % tn == 0, "gate columns must start on a column-tile boundary"
    ca, ch = col_ga // tn, col_gh // tn
    return pl.pallas_call(
        _gated_merge_body,
        grid=(n // tn, m // tm),
        in_specs=[pl.BlockSpec((tm, ka), lambda j, i: (i, 0)),
                  pl.BlockSpec((tm, kh), lambda j, i: (i, 0)),
                  pl.BlockSpec((ka, tn), lambda j, i: (0, j)),
                  pl.BlockSpec((kh, tn), lambda j, i: (0, j)),
                  pl.BlockSpec((tm, tn), lambda j, i: (i, ca + j)),
                  pl.BlockSpec((tm, tn), lambda j, i: (i, ch + j))],
        out_specs=pl.BlockSpec((tm, tn), lambda j, i: (i, j)),
        out_shape=jax.ShapeDtypeStruct((m, n), BF16),
        scratch_shapes=[pltpu.VMEM((ka, tn), BF16), pltpu.VMEM((kh, tn), BF16)],
        compiler_params=_params("arbitrary", "arbitrary"),
        name="gated_merge",
    )(attn, hgrn, w_pa, w_ph, proj, proj)


def _cross_body(q_ref, kv_ref, *rest, slab_rows, nseq):
    o_ref = rest[-1]
    scale = HEAD_DIM ** -0.5
    tq = q_ref.shape[0] // nseq
    mem_len = kv_ref.shape[1]
    for b in range(nseq):
        rs = slice(b * tq, (b + 1) * tq)
        kv = kv_ref.at[b]
        if slab_rows:
            kv = kv.reshape(mem_len * KV_SLAB, HEAD_DIM)
        for h in range(ATT_HEADS):
            sl = slice(h * HEAD_DIM, (h + 1) * HEAD_DIM)
            q = (q_ref[rs, sl] * scale).astype(BF16)
            if slab_rows:
                k = kv[pl.ds(h, mem_len, stride=KV_SLAB), :].astype(BF16)
                v = kv[pl.ds(ATT_HEADS + h, mem_len, stride=KV_SLAB), :].astype(BF16)
            else:
                k = kv[:, sl].astype(BF16)
                v = kv[:, ATT_OUT + h * HEAD_DIM: ATT_OUT + (h + 1) * HEAD_DIM].astype(BF16)
            s = _dot_nt(q, k)
            p = jnp.exp(s - jnp.max(s, axis=1, keepdims=True))
            o_ref[rs, sl] = _dot(p.astype(BF16), v) / jnp.sum(p, axis=1, keepdims=True)


def _cross_attn(q, mem_kv, out_buf, row0, batch, seq, tq, nseq):
    n = q.shape[0]
    nq = seq // tq
    assert nseq == 1 or nq == 1
    rows = nseq * tq
    rb0 = row0 // rows
    slab_rows = mem_kv.ndim == 4
    kv_spec = pl.BlockSpec((nseq,) + mem_kv.shape[1:], lambda b, i: (b,) + (0,) * (mem_kv.ndim - 1))
    in_specs = [pl.BlockSpec((rows, ATT_OUT), lambda b, i: (rb0 + b * nq + i, 0)), kv_spec]
    args = [q, mem_kv]
    aliases = {}
    if out_buf is not None:
        aliases = {2: 0}
        in_specs.append(pl.BlockSpec(memory_space=pl.ANY))
        args.append(out_buf)
    return pl.pallas_call(
        functools.partial(_cross_body, slab_rows=slab_rows, nseq=nseq),
        grid=(batch // nseq, nq),
        in_specs=in_specs,
        out_specs=pl.BlockSpec((rows, ATT_OUT), lambda b, i: (rb0 + b * nq + i, 0)),
        out_shape=jax.ShapeDtypeStruct((n, ATT_OUT), F32),
        input_output_aliases=aliases,
        compiler_params=_params("arbitrary", "arbitrary"),
        name=f"cross_attn_t{tq}",
    )(*args)


def _router_body(x_ref, g_ref, whi_ref, wlo_ref, b_ref, tri_ref, xs_ref, route_ref, cnt_ref, base_scr):
    tm = x_ref.shape[0]

    @pl.when(pl.program_id(0) == 0)
    def _():
        base_scr[...] = jnp.zeros_like(base_scr)

    x = x_ref[...]
    ms = jnp.mean(x * x, axis=-1, keepdims=True)
    xn = x * lax.rsqrt(ms + EPS) * g_ref[...]
    for c in range(ROW_SLAB):
        xs_ref[pl.ds(c, tm, stride=ROW_SLAB), :] = xn[:, c * HEAD_DIM:(c + 1) * HEAD_DIM]
    hi = xn.astype(BF16)
    lo = (xn - hi.astype(F32)).astype(BF16)
    logits = _dot(hi, whi_ref[...]) + _dot(lo, whi_ref[...]) + _dot(hi, wlo_ref[...]) + b_ref[...]
    lane = lax.broadcasted_iota(jnp.int32, logits.shape, 1)
    big = jnp.int32(1 << 20)
    is_g = jnp.logical_and(lane >= N_EXPERTS, lane < N_EXPERTS + N_GROUPS)
    lg = jnp.where(is_g, logits, NEG)
    mg = jnp.max(lg, axis=1, keepdims=True)
    p_top = 1.0 / jnp.sum(jnp.where(is_g, jnp.exp(lg - mg), 0.0), axis=1, keepdims=True)
    g_idx = jnp.min(jnp.where(jnp.logical_and(is_g, lg == mg), lane, big), axis=1, keepdims=True) - N_EXPERTS
    in_grp = jnp.logical_and(lane < N_EXPERTS, lane // EXPERTS_PER_GROUP == g_idx)
    le = jnp.where(in_grp, logits, NEG)
    v1 = jnp.max(le, axis=1, keepdims=True)
    i1 = jnp.min(jnp.where(jnp.logical_and(in_grp, le == v1), lane, big), axis=1, keepdims=True)
    rest = jnp.logical_and(in_grp, lane != i1)
    le2 = jnp.where(rest, logits, NEG)
    v2 = jnp.max(le2, axis=1, keepdims=True)
    i2 = jnp.min(jnp.where(jnp.logical_and(rest, le2 == v2), lane, big), axis=1, keepdims=True)
    e2 = jnp.exp(v2 - v1)
    w1 = p_top / (1.0 + e2)
    w2 = p_top * e2 / (1.0 + e2)
    hit1, hit2 = lane == i1, lane == i2
    hits = jnp.where(jnp.logical_or(hit1, hit2), 1.0, 0.0)
    before = _dot(tri_ref[...], hits.astype(BF16)) + base_scr[...]
    r1 = jnp.sum(jnp.where(hit1, before, 0.0), axis=1, keepdims=True)
    r2 = jnp.sum(jnp.where(hit2, before, 0.0), axis=1, keepdims=True)
    base_scr[...] += jnp.sum(hits, axis=0, keepdims=True)
    cnt_ref[...] = base_scr[...]
    cols = (i1.astype(F32), i2.astype(F32), w1, w2, r1, r2)
    route = jnp.zeros(logits.shape, F32)
    for c, val in enumerate(cols):
        route = jnp.where(lane == c, val, route)
    route_ref[...] = route


def _router(x, g, w_hi, w_lo, bias, tm):
    m, d = x.shape
    lanes = w_hi.shape[1]
    const = lambda shape: pl.BlockSpec(shape, lambda i: (0, 0))
    tri = jnp.tril(jnp.ones((tm, tm), F32), -1).astype(BF16)
    return pl.pallas_call(
        _router_body,
        grid=(m // tm,),
        in_specs=[pl.BlockSpec((tm, d), lambda i: (i, 0)), const((1, d)), const((d, lanes)), const((d, lanes)),
                  const((1, lanes)), const((tm, tm))],
        out_specs=[pl.BlockSpec((tm * ROW_SLAB, HEAD_DIM), lambda i: (i, 0)),
                   pl.BlockSpec((tm, lanes), lambda i: (i, 0)), const((1, lanes))],
        out_shape=[jax.ShapeDtypeStruct((m * ROW_SLAB, HEAD_DIM), F32), jax.ShapeDtypeStruct((m, lanes), F32),
                   jax.ShapeDtypeStruct((1, lanes), F32)],
        scratch_shapes=[pltpu.VMEM((1, lanes), F32)],
        compiler_params=_params("arbitrary"),
        name="router",
    )(x, g.reshape(1, d), w_hi, w_lo, bias, tri)


def _gather_rows(idx_ref, src_ref, dst_ref, sem, n):
    def issue(p, carry):
        r = pl.multiple_of(idx_ref[0, p] * ROW_SLAB, ROW_SLAB)
        o = pl.multiple_of(p * ROW_SLAB, ROW_SLAB)
        pltpu.make_async_copy(src_ref.at[pl.ds(r, ROW_SLAB), :], dst_ref.at[pl.ds(o, ROW_SLAB), :], sem).start()
        return carry

    lax.fori_loop(0, n, issue, 0, unroll=8)


def _wait_rows(src_ref, dst_ref, sem):
    pltpu.make_async_copy(src_ref.at[pl.ds(0, dst_ref.shape[0]), :], dst_ref, sem).wait()


def _slab_to_rows(ref, rows, first=0):
    return jnp.concatenate([ref[pl.ds(first * ROW_SLAB + c, rows, stride=ROW_SLAB), :] for c in range(ROW_SLAB)],
                           axis=1)


def _tile_idx_specs(tile, steps):
    def cur(t, *_):
        return (t, 0, 0)

    def nxt(t, *_):
        return (jnp.minimum(t + 1, steps - 1), 0, 0)

    return [pl.BlockSpec((None, 1, tile), cur, memory_space=pltpu.SMEM),
            pl.BlockSpec((None, 1, tile), nxt, memory_space=pltpu.SMEM)]


def _experts_body(te_ref, na_ref, idx_ref, idx_next_ref, xs_ref, wg_ref, wu_ref, wd_ref, o_ref,
                  xb0, xb1, sem, wgb, wub, wdb, *, te_rows):
    t = pl.program_id(0)
    na = na_ref[0]
    active = t < na
    bufs = (xb0, xb1)

    @pl.when(jnp.logical_and(t == 0, active))
    def _():
        _gather_rows(idx_ref, xs_ref, xb0, sem.at[0], te_rows)

    for s in range(2):
        @pl.when(jnp.logical_and(t + 1 < na, (t + 1) % 2 == s))
        def _(s=s):
            _gather_rows(idx_next_ref, xs_ref, bufs[s], sem.at[s], te_rows)

    changed = jnp.logical_or(t == 0, te_ref[t] != te_ref[jnp.maximum(t - 1, 0)])

    @pl.when(jnp.logical_and(active, changed))
    def _():
        wgb[...] = wg_ref[...].astype(BF16)
        wub[...] = wu_ref[...].astype(BF16)
        wdb[...] = wd_ref[...].astype(BF16)

    for s in range(2):
        @pl.when(jnp.logical_and(active, t % 2 == s))
        def _(s=s):
            _wait_rows(xs_ref, bufs[s], sem.at[s])
            x = _slab_to_rows(bufs[s], te_rows).astype(BF16)
            h = _silu(_dot(x, wgb[...])) * _dot(x, wub[...])
            o = _dot(h.astype(BF16), wdb[...])
            for c in range(ROW_SLAB):
                o_ref[pl.ds(c, te_rows, stride=ROW_SLAB), :] = o[:, c * HEAD_DIM:(c + 1) * HEAD_DIM]

    @pl.when(jnp.logical_not(active))
    def _():
        o_ref[...] = jnp.zeros_like(o_ref)


def _experts(xs, src_tok, tile_expert, n_active, w_gate, w_up, w_down, te_rows):
    ne, d, ff = w_gate.shape
    nt = tile_expert.shape[0]
    wspec = lambda shape: pl.BlockSpec((None,) + shape, lambda t, te, na: (te[t], 0, 0))
    grid_spec = pltpu.PrefetchScalarGridSpec(
        num_scalar_prefetch=2,
        grid=(nt,),
        in_specs=_tile_idx_specs(te_rows, nt) + [pl.BlockSpec(memory_space=pl.ANY),
                                                  wspec((d, ff)), wspec((d, ff)), wspec((ff, d))],
        out_specs=pl.BlockSpec((te_rows * ROW_SLAB, HEAD_DIM), lambda t, te, na: (t, 0)),
        scratch_shapes=[pltpu.VMEM((te_rows * ROW_SLAB, HEAD_DIM), F32),
                        pltpu.VMEM((te_rows * ROW_SLAB, HEAD_DIM), F32),
                        pltpu.SemaphoreType.DMA((2,)),
                        pltpu.VMEM((d, ff), BF16), pltpu.VMEM((d, ff), BF16), pltpu.VMEM((ff, d), BF16)],
    )
    idx = src_tok.reshape(nt, 1, te_rows)
    return pl.pallas_call(
        functools.partial(_experts_body, te_rows=te_rows),
        grid_spec=grid_spec,
        out_shape=jax.ShapeDtypeStruct((nt * te_rows * ROW_SLAB, HEAD_DIM), F32),
        compiler_params=_params("arbitrary"),
        name="experts",
    )(tile_expert, n_active, idx, idx, xs, w_gate, w_up, w_down)


def _moe_out_body(idx_ref, idx_next_ref, x_ref, route_ref, ys_ref, gf_ref, yp_ref, ysm_ref, db0, db1, sem,
                  *, n_prompt_tiles):
    i = pl.program_id(0)
    nt = pl.num_programs(0)
    tm = x_ref.shape[0]
    bufs = (db0, db1)

    @pl.when(i == 0)
    def _():
        _gather_rows(idx_ref, ys_ref, db0, sem.at[0], 2 * tm)

    for s in range(2):
        @pl.when(jnp.logical_and(i + 1 < nt, (i + 1) % 2 == s))
        def _(s=s):
            _gather_rows(idx_next_ref, ys_ref, bufs[s], sem.at[s], 2 * tm)

    route = route_ref[...]
    w1, w2 = route[:, 2:3], route[:, 3:4]
    for s in range(2):
        @pl.when(i % 2 == s)
        def _(s=s):
            _wait_rows(ys_ref, bufs[s], sem.at[s])
            x = x_ref[...] + w1 * _slab_to_rows(bufs[s], tm) + w2 * _slab_to_rows(bufs[s], tm, first=tm)
            ms = jnp.mean(x * x, axis=-1, keepdims=True)
            y = x * lax.rsqrt(ms + EPS) * gf_ref[...]

            @pl.when(i < n_prompt_tiles)
            def _():
                yp_ref[...] = y

            @pl.when(i >= n_prompt_tiles)
            def _():
                ysm_ref[...] = y


def _moe_out(x, route, ys_sorted, pos1, pos2, g_final, n_prompt, tm):
    m, d = x.shape
    nt = m // tm
    npt = n_prompt // tm
    lanes = route.shape[1]
    idx = jnp.concatenate([pos1.reshape(nt, tm), pos2.reshape(nt, tm)], axis=1).reshape(nt, 1, 2 * tm)
    return pl.pallas_call(
        functools.partial(_moe_out_body, n_prompt_tiles=npt),
        grid=(nt,),
        in_specs=_tile_idx_specs(2 * tm, nt) + [
            pl.BlockSpec((tm, d), lambda i: (i, 0)), pl.BlockSpec((tm, lanes), lambda i: (i, 0)),
            pl.BlockSpec(memory_space=pl.ANY), pl.BlockSpec((1, d), lambda i: (0, 0))],
        out_specs=[pl.BlockSpec((tm, d), lambda i: (jnp.minimum(i, npt - 1), 0)),
                   pl.BlockSpec((tm, d), lambda i: (jnp.maximum(i - npt, 0), 0))],
        out_shape=[jax.ShapeDtypeStruct((n_prompt, d), F32), jax.ShapeDtypeStruct((m - n_prompt, d), F32)],
        scratch_shapes=[pltpu.VMEM((2 * tm * ROW_SLAB, HEAD_DIM), F32),
                        pltpu.VMEM((2 * tm * ROW_SLAB, HEAD_DIM), F32),
                        pltpu.SemaphoreType.DMA((2,))],
        compiler_params=_params("arbitrary"),
        name="moe_out",
    )(idx, idx, x, route, ys_sorted, g_final.reshape(1, d))


def _moe(x, g_ffn, w_rg, b_rg, w_re, b_re, w_gate, w_up, w_down, g_final, n_prompt):
    m, d = x.shape
    lanes = HEAD_DIM
    npad = lanes - N_EXPERTS - N_GROUPS
    w_r = jnp.concatenate([w_re, w_rg, jnp.zeros((d, npad), F32)], axis=1)
    b_r = jnp.concatenate([b_re, b_rg, jnp.zeros((npad,), F32)]).reshape(1, lanes)
    w_r_hi = w_r.astype(BF16)
    w_r_lo = (w_r - w_r_hi.astype(F32)).astype(BF16)
    xs, route, counts = _router(x, g_ffn, w_r_hi, w_r_lo, b_r, 512)

    te = MOE_TILE
    n_tiles = -(-(2 * m + N_EXPERTS * (te - 1)) // te)
    n_tiles += n_tiles % 2
    counts = counts[0, :N_EXPERTS].astype(jnp.int32)
    padded = (counts + te - 1) // te * te
    pad_end = jnp.cumsum(padded)
    pad_off = pad_end - padded
    experts = jnp.arange(N_EXPERTS, dtype=jnp.int32)

    def dest(col_e, col_r):
        e = route[:, col_e].astype(jnp.int32)
        off = jnp.sum(jnp.where(e[:, None] == experts[None, :], pad_off[None, :], 0), axis=1)
        return off + route[:, col_r].astype(jnp.int32)

    pos1, pos2 = dest(0, 4), dest(1, 5)
    tok = jnp.tile(jnp.arange(m, dtype=jnp.int32), 2)
    src_tok = jnp.zeros((n_tiles * te,), jnp.int32).at[jnp.concatenate([pos1, pos2])].set(tok)
    tile_start = jnp.arange(n_tiles, dtype=jnp.int32) * te
    tile_expert = jnp.minimum(jnp.sum(tile_start[:, None] >= pad_end[None, :], axis=1), N_EXPERTS - 1)
    n_active = (pad_end[-1] // te).reshape(1)

    ys_sorted = _experts(xs, src_tok, tile_expert.astype(jnp.int32), n_active.astype(jnp.int32),
                         w_gate, w_up, w_down, te)
    return _moe_out(x, route, ys_sorted, pos1, pos2, g_final, n_prompt, 256)


def _rope_tables(positions):
    half = HEAD_DIM // 2
    inv_freq = ROPE_THETA ** (-jnp.arange(half, dtype=F32) / half)
    ang = positions.astype(F32)[:, None] * inv_freq[None, :]
    cos, sin = jnp.cos(ang), jnp.sin(ang)
    return jnp.concatenate([cos, cos], axis=1), jnp.concatenate([-sin, sin], axis=1)


def kernel(x_prompt, x_sample, cache_swa1, cache_swa2, cache_swa3, state_hgrn, cache_mem_kv, mem_prompt,
           hgrn_lb_logits, norm_mix, w_in, w_proj_attn, w_proj_hgrn, w_out, hgrn_norm, norm_cross, norm_mem,
           w_cq, w_ckv, w_co, norm_ffn, w_rg, b_rg, w_re, b_re, w_e_gate, w_e_up, w_e_down, norm_final):
    bp, seq, d = x_prompt.shape
    bs, dseq, _ = x_sample.shape
    depth = w_in.shape[0]
    assert depth == 1, "single-layer trunk"
    past = cache_swa3.shape[2]
    mem_len = mem_prompt.shape[1]
    np_, ns = bp * seq, bs * dseq
    hw = HGRN_HEADS * HEAD_DIM
    col_hgrn = 3 * ATT_WIDTH
    col_ga = col_hgrn + 4 * hw
    col_gh = col_ga + d
    l = 0

    x_all = jnp.concatenate([x_prompt.reshape(np_, d), x_sample.reshape(ns, d)], axis=0)
    pos = jnp.concatenate([jnp.tile(jnp.arange(seq, dtype=jnp.int32), bp),
                           jnp.tile(past + jnp.arange(dseq, dtype=jnp.int32), bs)])
    cos2, sin2 = _rope_tables(pos)

    xn = _rmsnorm(x_all, norm_mix[l], BF16, 512)
    proj = _matmul(xn, w_in[l], F32, 512, 1536, name="proj_in")

    q_rot, kv_all = _rope(proj, cos2, sin2, 256)
    attn = _attn_prompt(q_rot, kv_all, bp, seq, 256)
    attn = _attn_sample(q_rot, kv_all, (cache_swa1[l], cache_swa2[l], cache_swa3[l]), attn, np_, bs, dseq)

    hgrn, st_p = _hgrn(proj, col_hgrn, hgrn_lb_logits, hgrn_norm[l], None, None, 0, bp, seq, 128, 1)
    hgrn, st_s = _hgrn(proj, col_hgrn, hgrn_lb_logits, hgrn_norm[l], state_hgrn[l], hgrn, np_, bs, dseq, dseq, 4)

    merged = _gated_merge(attn, hgrn, w_proj_attn[l], w_proj_hgrn[l], proj, col_ga, col_gh, 512, 512)
    x1 = _matmul(merged, w_out[l], F32, 512, 1024, res=x_all, name="proj_out")

    mem_n = _rmsnorm(mem_prompt.reshape(bp * mem_len, d), norm_mem[l], BF16, 256)
    mkv_p = _matmul(mem_n, w_ckv[l], F32, 256, KV_ROW, name="mem_kv")
    xc = _rmsnorm(x1, norm_cross[l], BF16, 512)
    qc = _matmul(xc, w_cq[l], F32, 512, ATT_OUT, name="cross_q")
    oc = _cross_attn(qc, mkv_p.reshape(bp, mem_len, KV_ROW), None, 0, bp, seq, 512, 1)
    oc = _cross_attn(qc, cache_mem_kv[l].reshape(bs, mem_len, KV_SLAB, HEAD_DIM), oc, np_, bs, dseq, dseq, 4)
    x2 = _matmul(oc, w_co[l], F32, 512, 1024, res=x1, name="cross_out")

    ff = w_e_gate.shape[-1]
    y_p, y_s = _moe(x2, norm_ffn[l], w_rg[l], b_rg[l], w_re[l], b_re[l],
                    w_e_gate[l].reshape(N_EXPERTS, d, ff), w_e_up[l].reshape(N_EXPERTS, d, ff),
                    w_e_down[l].reshape(N_EXPERTS, ff, d), norm_final, np_)

    swa_p = [_kv_rows(kv_all, g, 0, bp, seq, min(w, seq), min(w, seq, 256))[None]
             for g, (w, _) in enumerate(ATT_GROUPS)]
    swa_s = [_kv_rows(kv_all, g, np_, 1, ns, ns, 256).reshape(1, bs, dseq, 2, ATT_HEADS, HEAD_DIM)
             for g in range(len(ATT_GROUPS))]
    return (y_p.reshape(bp, seq, d), y_s.reshape(bs, dseq, d),
            swa_p[0], swa_p[1], swa_p[2], st_p[None],
            mkv_p.reshape(1, bp, mem_len, 2, ATT_HEADS, HEAD_DIM),
            swa_s[0], swa_s[1], swa_s[2], st_s[None])
```

```python
import functools

import jax
import jax.numpy as jnp
from jax import lax
from jax.experimental import pallas as pl
from jax.experimental.pallas import tpu as pltpu

F32 = jnp.float32
BF16 = jnp.bfloat16

ATT_GROUPS = ((128, 1), (512, 4), (2048, 16))
ATT_HEADS = 4
HEAD_DIM = 128
ATT_OUT = ATT_HEADS * HEAD_DIM
ATT_WIDTH = len(ATT_GROUPS) * ATT_OUT
KV_ROW = 2 * ATT_OUT
KV_SLAB = 2 * ATT_HEADS
HGRN_HEADS = 16
ROPE_THETA = 10000.0
EPS = 1e-6
N_EXPERTS = 32
N_GROUPS = 4
EXPERTS_PER_GROUP = 8
NEG = -1e30
ATT_BLOCK = 128
HGRN_SUB = 8
ROW_SLAB = 16
MOE_TILE = 256
VMEM_LIMIT = 56 * 1024 * 1024


def _params(*sem):
    return pltpu.CompilerParams(dimension_semantics=sem, vmem_limit_bytes=VMEM_LIMIT)


def _dot(a, b):
    return jnp.dot(a, b, preferred_element_type=F32)


def _dot_nt(a, b):
    return lax.dot_general(a, b, (((1,), (1,)), ((), ())), preferred_element_type=F32)


def _sigmoid(x):
    return 1.0 / (1.0 + jnp.exp(-x))


def _silu(x):
    return x * _sigmoid(x)


def _rmsnorm_body(x_ref, g_ref, o_ref):
    x = x_ref[...]
    ms = jnp.mean(x * x, axis=-1, keepdims=True)
    o_ref[...] = (x * lax.rsqrt(ms + EPS) * g_ref[...]).astype(o_ref.dtype)


def _rmsnorm(x, g, out_dtype, tm):
    m, d = x.shape
    return pl.pallas_call(
        _rmsnorm_body,
        grid=(m // tm,),
        in_specs=[pl.BlockSpec((tm, d), lambda i: (i, 0)), pl.BlockSpec((1, d), lambda i: (0, 0))],
        out_specs=pl.BlockSpec((tm, d), lambda i: (i, 0)),
        out_shape=jax.ShapeDtypeStruct((m, d), out_dtype),
        compiler_params=_params("arbitrary"),
        name="rmsnorm",
    )(x, g.reshape(1, d))


def _mm_body(*refs, has_res, has_rope):
    refs = list(refs)
    a_ref, w_ref = refs[:2]
    o_ref, wb_ref = refs[-2:]
    extra = refs[2:-2]
    j = pl.program_id(0)

    @pl.when(pl.program_id(1) == 0)
    def _():
        wb_ref[...] = w_ref[...].astype(BF16)

    acc = _dot(a_ref[...].astype(BF16), wb_ref[...])
    if has_res:
        acc = acc + extra[0][...]
    if not has_rope:
        o_ref[...] = acc.astype(o_ref.dtype)
        return

    cos_ref, sin_ref = extra[-2:]

    @pl.when(j >= 2)
    def _():
        o_ref[...] = acc.astype(o_ref.dtype)

    @pl.when(j < 2)
    def _():
        cos = cos_ref[...]
        sin = sin_ref[...]
        scale = jnp.where(j == 0, HEAD_DIM ** -0.5, 1.0)
        for h in range(ATT_WIDTH // HEAD_DIM):
            sl = slice(h * HEAD_DIM, (h + 1) * HEAD_DIM)
            x = acc[:, sl]
            o_ref[:, sl] = ((x * cos + pltpu.roll(x, HEAD_DIM // 2, axis=1) * sin) * scale).astype(o_ref.dtype)


def _matmul(a, w, out_dtype, tm, tn, res=None, rope=None, name="matmul"):
    m, k = a.shape
    n = w.shape[1]
    in_specs = [pl.BlockSpec((tm, k), lambda j, i: (i, 0)), pl.BlockSpec((k, tn), lambda j, i: (0, j))]
    args = [a, w]
    if res is not None:
        in_specs.append(pl.BlockSpec((tm, tn), lambda j, i: (i, j)))
        args.append(res)
    if rope is not None:
        assert tn == ATT_WIDTH
        in_specs.extend([pl.BlockSpec((tm, HEAD_DIM), lambda j, i: (i, 0))] * 2)
        args.extend(rope)
    return pl.pallas_call(
        functools.partial(_mm_body, has_res=res is not None, has_rope=rope is not None),
        grid=(n // tn, m // tm),
        in_specs=in_specs,
        out_specs=pl.BlockSpec((tm, tn), lambda j, i: (i, j)),
        out_shape=jax.ShapeDtypeStruct((m, n), out_dtype),
        scratch_shapes=[pltpu.VMEM((k, tn), BF16)],
        compiler_params=_params("arbitrary", "arbitrary"),
        name=name,
    )(*args)


def _prompt_key_blocks(bq):
    table = []
    for g, (w, _) in enumerate(ATT_GROUPS):
        nback = -(-w // bq)
        table.extend((g, back) for back in range(nback, -1, -1))
    return tuple(table)


def _attn_prompt_body(q_ref, k_ref, v_ref, o_ref, m_scr, l_scr, acc_scr, *, bq, table):
    i = pl.program_id(1)
    j = pl.program_id(2)
    ng = len(ATT_GROUPS)

    @pl.when(j == 0)
    def _():
        m_scr[...] = jnp.full(m_scr.shape, NEG, F32)
        l_scr[...] = jnp.zeros(l_scr.shape, F32)
        acc_scr[...] = jnp.zeros(acc_scr.shape, F32)

    row = lax.broadcasted_iota(jnp.int32, (bq, bq), 0)
    col = lax.broadcasted_iota(jnp.int32, (bq, bq), 1)
    for g, (w, d) in enumerate(ATT_GROUPS):
        j0 = min(jj for jj, (gg, _) in enumerate(table) if gg == g)
        nback = max(back for gg, back in table if gg == g)
        back = nback - (j - j0)

        @pl.when(jnp.logical_and(jnp.logical_and(j >= j0, j <= j0 + nback), i >= back))
        def _(g=g, w=w, d=d, back=back):
            dist = back * bq + row - col
            valid = jnp.logical_and(jnp.logical_and(dist >= 0, dist <= w), (dist & (d - 1)) == 0)
            bias = jnp.where(valid, 0.0, NEG)
            heads = range(ATT_HEADS)
            hs = [slice(h * HEAD_DIM, (h + 1) * HEAD_DIM) for h in heads]
            m_old = [m_scr[g * ATT_HEADS + h] for h in heads]
            l_old = [l_scr[g * ATT_HEADS + h] for h in heads]
            a_old = [acc_scr[g, :, hs[h]] for h in heads]
            qs = [q_ref[:, g * ATT_OUT + h * HEAD_DIM: g * ATT_OUT + (h + 1) * HEAD_DIM].astype(BF16) for h in heads]
            ss = [_dot_nt(qs[h], k_ref[:, hs[h]].astype(BF16)) + bias for h in heads]
            m_new = [jnp.maximum(m_old[h], jnp.max(ss[h], axis=1, keepdims=True)) for h in heads]
            ps = [jnp.exp(ss[h] - jnp.concatenate([m_new[h]] * (bq // HEAD_DIM), axis=1)) for h in heads]
            alpha = [jnp.exp(m_old[h] - m_new[h]) for h in heads]
            pv = [_dot(ps[h].astype(BF16), v_ref[:, hs[h]].astype(BF16)) for h in heads]
            for h in heads:
                m_scr[g * ATT_HEADS + h] = m_new[h]
                l_scr[g * ATT_HEADS + h] = alpha[h] * l_old[h] + jnp.sum(ps[h], axis=1, keepdims=True)
                acc_scr[g, :, hs[h]] = alpha[h] * a_old[h] + pv[h]

    @pl.when(j == pl.num_programs(2) - 1)
    def _():
        for h in range(ATT_HEADS):
            sl = slice(h * HEAD_DIM, (h + 1) * HEAD_DIM)
            lse = [m_scr[g * ATT_HEADS + h] + jnp.log(l_scr[g * ATT_HEADS + h]) for g in range(ng)]
            mx = functools.reduce(jnp.maximum, lse)
            ws = [jnp.exp(x - mx) for x in lse]
            num = sum(ws[g] / l_scr[g * ATT_HEADS + h] * acc_scr[g, :, sl] for g in range(ng))
            o_ref[:, sl] = num / sum(ws)


def _attn_prompt(proj, batch, seq, bq):
    n = proj.shape[0]
    ng = len(ATT_GROUPS)
    nq = seq // bq
    table = _prompt_key_blocks(bq)
    groups = jnp.asarray([g for g, _ in table], jnp.int32)
    backs = jnp.asarray([b for _, b in table], jnp.int32)

    def kv_map(part):
        def index(b, i, j, g_ref, back_ref):
            return (b * nq + jnp.maximum(i - back_ref[j], 0), part * ng + g_ref[j])
        return index

    grid_spec = pltpu.PrefetchScalarGridSpec(
        num_scalar_prefetch=2,
        grid=(batch, nq, len(table)),
        in_specs=[pl.BlockSpec((bq, ATT_WIDTH), lambda b, i, j, g_ref, back_ref: (b * nq + i, 0)),
                  pl.BlockSpec((bq, ATT_OUT), kv_map(1)),
                  pl.BlockSpec((bq, ATT_OUT), kv_map(2))],
        out_specs=pl.BlockSpec((bq, ATT_OUT), lambda b, i, j, g_ref, back_ref: (b * nq + i, 0)),
        scratch_shapes=[pltpu.VMEM((ng * ATT_HEADS, bq, HEAD_DIM), F32),
                        pltpu.VMEM((ng * ATT_HEADS, bq, HEAD_DIM), F32),
                        pltpu.VMEM((ng, bq, ATT_OUT), F32)],
    )

    def body(g_ref, back_ref, *refs):
        del g_ref, back_ref
        _attn_prompt_body(*refs, bq=bq, table=table)

    return pl.pallas_call(
        body,
        grid_spec=grid_spec,
        out_shape=jax.ShapeDtypeStruct((n, ATT_OUT), F32),
        compiler_params=_params("arbitrary", "arbitrary", "arbitrary"),
        name="attn_prompt",
    )(groups, backs, proj, proj, proj)


def _head_lanes(ref2d, first, rows):
    return jnp.concatenate([ref2d[pl.ds(first + h, rows, stride=KV_SLAB), :] for h in range(ATT_HEADS)], axis=1)


def _attn_sample_body(q_ref, kn_ref, vn_ref, *rest, tq):
    cache_refs, o_ref = rest[:-2], rest[-1]
    nrow = ATT_HEADS * tq
    nkey = ATT_BLOCK
    rid = lax.broadcasted_iota(jnp.int32, (nrow, 1), 0)
    i_row = rid % tq
    head_row = rid // tq
    lane_head = lax.broadcasted_iota(jnp.int32, (1, ATT_OUT), 1) // HEAD_DIM
    head_mask = head_row == lane_head
    key = lax.broadcasted_iota(jnp.int32, (1, nkey), 1)
    pad = jnp.zeros((nkey - tq, ATT_OUT), F32)
    outs, lses = [], []
    ci = 0
    for g, (_, d) in enumerate(ATT_GROUPS):
        qg = q_ref[:, g * ATT_OUT:(g + 1) * ATT_OUT]
        qrows = jnp.where(head_mask, jnp.concatenate([qg] * ATT_HEADS, axis=0), 0.0).astype(BF16)
        k_new = jnp.concatenate([kn_ref[:, g * ATT_OUT:(g + 1) * ATT_OUT], pad], axis=0)
        v_new = jnp.concatenate([vn_ref[:, g * ATT_OUT:(g + 1) * ATT_OUT], pad], axis=0)
        valid_new = jnp.logical_and(jnp.logical_and(key < tq, key <= i_row), (key % d) == (i_row % d))
        blocks = [(k_new, v_new, valid_new)]
        for r in range(min(d, tq)):
            c2 = cache_refs[ci].reshape(nkey * KV_SLAB, HEAD_DIM)
            ci += 1
            valid = jnp.logical_and((i_row % d) == r, key >= i_row // d)
            blocks.append((_head_lanes(c2, 0, nkey), _head_lanes(c2, ATT_HEADS, nkey), valid))
        ss = [jnp.where(valid, _dot_nt(qrows, kb.astype(BF16)), NEG) for kb, _, valid in blocks]
        m = functools.reduce(jnp.maximum, [jnp.max(s, axis=1, keepdims=True) for s in ss])
        ps = [jnp.exp(s - m) for s in ss]
        l = sum(jnp.sum(p, axis=1, keepdims=True) for p in ps)
        acc = sum(_dot(p.astype(BF16), vb.astype(BF16)) for p, (_, vb, _) in zip(ps, blocks))
        o = acc / l
        lse = m + jnp.log(l)
        outs.append(jnp.concatenate(
            [o[h * tq:(h + 1) * tq, h * HEAD_DIM:(h + 1) * HEAD_DIM] for h in range(ATT_HEADS)], axis=1))
        lses.append(jnp.concatenate(
            [jnp.broadcast_to(lse[h * tq:(h + 1) * tq], (tq, HEAD_DIM)) for h in range(ATT_HEADS)], axis=1))
    mx = jnp.maximum(jnp.maximum(lses[0], lses[1]), lses[2])
    ws = [jnp.exp(x - mx) for x in lses]
    o_ref[...] = (ws[0] * outs[0] + ws[1] * outs[1] + ws[2] * outs[2]) / (ws[0] + ws[1] + ws[2])


def _attn_sample(proj, caches, attn_buf, row0, batch, tq):
    blk0 = row0 // tq
    views, specs = [], []
    for (w, d), c in zip(ATT_GROUPS, caches):
        assert c.shape[1] == w and w // d == ATT_BLOCK, "window buffers must hold exactly one window"
        view = c.reshape(batch, w // d, d, KV_SLAB, HEAD_DIM)
        for r in range(min(d, tq)):
            views.append(view)
            specs.append(pl.BlockSpec((None, w // d, None, KV_SLAB, HEAD_DIM), lambda b, r=r: (b, 0, r, 0, 0)))
    n_in = 3 + len(views)
    qkv = [pl.BlockSpec((tq, ATT_WIDTH), lambda b, part=part: (blk0 + b, part)) for part in range(3)]
    return pl.pallas_call(
        functools.partial(_attn_sample_body, tq=tq),
        grid=(batch,),
        in_specs=qkv + specs + [pl.BlockSpec(memory_space=pl.ANY)],
        out_specs=pl.BlockSpec((tq, ATT_OUT), lambda b: (blk0 + b, 0)),
        out_shape=jax.ShapeDtypeStruct(attn_buf.shape, F32),
        input_output_aliases={n_in: 0},
        compiler_params=_params("arbitrary"),
        name="attn_sample",
    )(proj, proj, proj, *views, attn_buf)


def _kv_rows_body(k_ref, v_ref, o_ref):
    tm = k_ref.shape[0]
    for h in range(ATT_HEADS):
        sl = slice(h * HEAD_DIM, (h + 1) * HEAD_DIM)
        o_ref[pl.ds(h, tm, stride=KV_SLAB), :] = k_ref[:, sl]
        o_ref[pl.ds(ATT_HEADS + h, tm, stride=KV_SLAB), :] = v_ref[:, sl]


def _kv_rows(proj, g, row0, batch, seq, keep, tm):
    ng = len(ATT_GROUPS)
    nt = keep // tm
    rb0 = (row0 + seq - keep) // tm
    per_seq = seq // tm
    out = pl.pallas_call(
        _kv_rows_body,
        grid=(batch, nt),
        in_specs=[pl.BlockSpec((tm, ATT_OUT), lambda b, t: (rb0 + b * per_seq + t, ng + g)),
                  pl.BlockSpec((tm, ATT_OUT), lambda b, t: (rb0 + b * per_seq + t, 2 * ng + g))],
        out_specs=pl.BlockSpec((tm * KV_SLAB, HEAD_DIM), lambda b, t: (b * nt + t, 0)),
        out_shape=jax.ShapeDtypeStruct((batch * keep * KV_SLAB, HEAD_DIM), F32),
        compiler_params=_params("arbitrary", "arbitrary"),
        name=f"kv_rows_g{g}_{tm}",
    )(proj, proj)
    return out.reshape(batch, keep, 2, ATT_HEADS, HEAD_DIM)


def _cumsum_rows(x, seg):
    row = lax.broadcasted_iota(jnp.int32, (x.shape[0], 1), 0) % seg
    sh = 1
    while sh < seg:
        x = x + jnp.where(row >= sh, pltpu.roll(x, sh, axis=0), 0.0)
        sh *= 2
    return x


def _bcast_rows(b, first, period):
    c, w = b.shape
    parts = [jnp.broadcast_to(b[p * period + first: p * period + first + 1, :], (period, w))
             for p in range(c // period)]
    return parts[0] if len(parts) == 1 else jnp.concatenate(parts, axis=0)


def _hgrn_body(*refs, chunk, nseq, has_s0, chained):
    refs = list(refs)
    qh_ref, fh_ref, ih_ref, og_ref, lbl_ref, gn_ref = refs[:6]
    s0_ref = refs[6] if has_s0 else None
    o_ref, so_ref, st_ref = refs[-3:]
    c = chunk
    nh = ATT_HEADS
    t = pl.program_id(2)

    if chained:
        @pl.when(t == 0)
        def _():
            for h in range(nh):
                st_ref[h] = s0_ref[0, h].T if has_s0 else jnp.zeros((HEAD_DIM, HEAD_DIM), F32)

    lbl = lbl_ref[...]
    e = jnp.exp(lbl - jnp.max(lbl, axis=0, keepdims=True))
    lb = e[0:1, :] / jnp.sum(e, axis=0, keepdims=True)
    f = lb + (1.0 - lb) * _sigmoid(fh_ref[...])
    kk = 1.0 - f
    qf = _silu(qh_ref[...])
    b = _cumsum_rows(jnp.log(f), c)
    vv = ih_ref[...]

    row = lax.broadcasted_iota(jnp.int32, (c, c), 0)
    col = lax.broadcasted_iota(jnp.int32, (c, c), 1)
    rid = lax.broadcasted_iota(jnp.int32, (nseq * c, 1), 0)
    seqs = [slice(q * c, (q + 1) * c) for q in range(nseq)]
    heads = [slice(h * HEAD_DIM, (h + 1) * HEAD_DIM) for h in range(nh)]

    a = [[jnp.zeros((c, c), F32) for _ in range(nh)] for _ in range(nseq)]
    s = c // 2
    while s >= HGRN_SUB:
        ref = _bcast_rows(b, s - 1, 2 * s)
        second = (rid % (2 * s)) >= s
        dlt = b - ref
        ee = jnp.exp(jnp.where(second, dlt, -dlt))
        ql = jnp.where(second, qf * ee, 0.0).astype(BF16)
        kl = jnp.where(second, 0.0, kk * ee).astype(BF16)
        same = (row // (2 * s)) == (col // (2 * s))
        for q in range(nseq):
            for h in range(nh):
                a[q][h] = a[q][h] + jnp.where(same, _dot_nt(ql[seqs[q], heads[h]], kl[seqs[q], heads[h]]), 0.0)
        s //= 2
    sub = min(HGRN_SUB, c)
    ref = _bcast_rows(b, 0, sub)
    dlt = b - ref
    qd = (qf * jnp.exp(dlt)).astype(BF16)
    kd = (kk * jnp.exp(jnp.minimum(-dlt, 80.0))).astype(BF16)
    diag = jnp.logical_and((row // sub) == (col // sub), col <= row)
    qe = (qf * jnp.exp(b)).astype(BF16)
    b_last = _bcast_rows(b, c - 1, c)
    kend = (kk * jnp.exp(b_last - b)).astype(BF16)
    dec = jnp.exp(b_last)
    gn = gn_ref[...]
    gate = _silu(og_ref[...])
    vb = vv.astype(BF16)
    eye = (lax.broadcasted_iota(jnp.int32, (HEAD_DIM, HEAD_DIM), 0)
           == lax.broadcasted_iota(jnp.int32, (HEAD_DIM, HEAD_DIM), 1))
    tn_dims = (((0,), (0,)), ((), ()))
    for h in range(nh):
        sl = heads[h]
        st = st_ref[h] if chained else None
        for q in range(nseq):
            rs = seqs[q]
            ah = a[q][h] + jnp.where(diag, _dot_nt(qd[rs, sl], kd[rs, sl]), 0.0)
            intra = _dot(ah.astype(BF16), vb[rs, sl])
            dec_q = dec[q * c:q * c + 1, sl]
            if chained:
                o = _dot_nt(qe[rs, sl], st.astype(BF16)) + intra
                st = st * dec_q + lax.dot_general(vb[rs, sl], kend[rs, sl], tn_dims, preferred_element_type=F32)
            else:
                s_kv = s0_ref[q, h]
                o = _dot(qe[rs, sl], s_kv.astype(BF16)) + intra
                dec_col = jnp.sum(jnp.where(eye, jnp.broadcast_to(dec_q, (HEAD_DIM, HEAD_DIM)), 0.0),
                                  axis=1, keepdims=True)
                so_ref[q, h] = s_kv * dec_col + lax.dot_general(kend[rs, sl], vb[rs, sl], tn_dims,
                                                                preferred_element_type=F32)
            ms = jnp.mean(o * o, axis=-1, keepdims=True)
            o_ref[rs, sl] = (o * lax.rsqrt(ms + EPS) * gn * gate[rs, sl]).astype(o_ref.dtype)
        if chained:
            st_ref[h] = st

            @pl.when(t == pl.num_programs(2) - 1)
            def _(h=h, st=st):
                so_ref[0, h] = st.T


def _hgrn(proj, col0, lb_logits, gnorm, s0, out_buf, row0, batch, seq, chunk, nseq):
    n = proj.shape[0]
    chained = seq != chunk
    assert chained or s0 is not None
    nt = seq // (chunk * nseq) if chained else 1
    nstate = 1 if chained else nseq
    hb = HGRN_HEADS // ATT_HEADS
    width = HGRN_HEADS * HEAD_DIM
    cb0 = col0 // ATT_OUT
    rows = nseq * chunk
    rb0 = row0 // rows

    def col(j):
        return pl.BlockSpec((rows, ATT_OUT), lambda b, h, t, j=j: (rb0 + b * nt + t, cb0 + j * hb + h))

    in_specs = [col(0), col(1), col(2), col(3),
                pl.BlockSpec((lb_logits.shape[0], ATT_OUT), lambda b, h, t: (0, h)),
                pl.BlockSpec((1, HEAD_DIM), lambda b, h, t: (0, 0))]
    args = [proj, proj, proj, proj, lb_logits, gnorm.reshape(1, HEAD_DIM)]
    state_spec = pl.BlockSpec((nstate, ATT_HEADS, HEAD_DIM, HEAD_DIM), lambda b, h, t: (b, h, 0, 0))
    if s0 is not None:
        in_specs.append(state_spec)
        args.append(s0)
    aliases = {}
    if out_buf is not None:
        aliases = {len(args): 0}
        in_specs.append(pl.BlockSpec(memory_space=pl.ANY))
        args.append(out_buf)
    return pl.pallas_call(
        functools.partial(_hgrn_body, chunk=chunk, nseq=nseq, has_s0=s0 is not None, chained=chained),
        grid=(batch // nstate, hb, nt),
        in_specs=in_specs,
        out_specs=[pl.BlockSpec((rows, ATT_OUT), lambda b, h, t: (rb0 + b * nt + t, h)), state_spec],
        out_shape=[jax.ShapeDtypeStruct((n, width), BF16),
                   jax.ShapeDtypeStruct((batch, HGRN_HEADS, HEAD_DIM, HEAD_DIM), F32)],
        scratch_shapes=[pltpu.VMEM((ATT_HEADS, HEAD_DIM, HEAD_DIM), F32)],
        input_output_aliases=aliases,
        compiler_params=_params("arbitrary", "arbitrary", "arbitrary"),
        name=f"hgrn_c{chunk}",
    )(*args)


def _gated_merge_body(att_ref, hg_ref, wa_ref, wh_ref, *rest, ngate):
    ga_refs, gh_refs = rest[:ngate], rest[ngate:2 * ngate]
    o_ref, wab_ref, whb_ref = rest[2 * ngate:]

    @pl.when(pl.program_id(1) == 0)
    def _():
        wab_ref[...] = wa_ref[...].astype(BF16)
        whb_ref[...] = wh_ref[...].astype(BF16)

    pa = _dot(att_ref[...].astype(BF16), wab_ref[...])
    ph = _dot(hg_ref[...].astype(BF16), whb_ref[...])
    ga = jnp.concatenate([r[...] for r in ga_refs], axis=1)
    gh = jnp.concatenate([r[...] for r in gh_refs], axis=1)
    o_ref[...] = (_sigmoid(ga) * pa + _sigmoid(gh) * ph).astype(o_ref.dtype)


def _gated_merge(attn, hgrn, w_pa, w_ph, proj, col_ga, col_gh, tm, tn, tg):
    m = attn.shape[0]
    n = w_pa.shape[1]
    ka, kh = w_pa.shape[0], w_ph.shape[0]
    assert col_ga % tg == 0 and col_gh % tg == 0 and tn % tg == 0, "gate columns must start on a gate block"
    ngate = tn // tg
    gate = lambda c0, u: pl.BlockSpec((tm, tg), lambda j, i: (i, c0 // tg + j * ngate + u))
    gates = [gate(col_ga, u) for u in range(ngate)] + [gate(col_gh, u) for u in range(ngate)]
    return pl.pallas_call(
        functools.partial(_gated_merge_body, ngate=ngate),
        grid=(n // tn, m // tm),
        in_specs=[pl.BlockSpec((tm, ka), lambda j, i: (i, 0)),
                  pl.BlockSpec((tm, kh), lambda j, i: (i, 0)),
                  pl.BlockSpec((ka, tn), lambda j, i: (0, j)),
                  pl.BlockSpec((kh, tn), lambda j, i: (0, j))] + gates,
        out_specs=pl.BlockSpec((tm, tn), lambda j, i: (i, j)),
        out_shape=jax.ShapeDtypeStruct((m, n), BF16),
        scratch_shapes=[pltpu.VMEM((ka, tn), BF16), pltpu.VMEM((kh, tn), BF16)],
        compiler_params=_params("arbitrary", "arbitrary"),
        name="gated_merge",
    )(attn, hgrn, w_pa, w_ph, *([proj] * (2 * ngate)))


def _cross_body(q_ref, kv_ref, *rest, slab_rows, nseq):
    o_ref = rest[-1]
    scale = HEAD_DIM ** -0.5
    tq = q_ref.shape[0] // nseq
    mem_len = kv_ref.shape[1]
    for b in range(nseq):
        rs = slice(b * tq, (b + 1) * tq)
        kv = kv_ref.at[b]
        if slab_rows:
            kv = kv.reshape(mem_len * KV_SLAB, HEAD_DIM)
        for h in range(ATT_HEADS):
            sl = slice(h * HEAD_DIM, (h + 1) * HEAD_DIM)
            q = (q_ref[rs, sl] * scale).astype(BF16)
            if slab_rows:
                k = kv[pl.ds(h, mem_len, stride=KV_SLAB), :].astype(BF16)
                v = kv[pl.ds(ATT_HEADS + h, mem_len, stride=KV_SLAB), :].astype(BF16)
            else:
                k = kv[:, sl].astype(BF16)
                v = kv[:, ATT_OUT + h * HEAD_DIM: ATT_OUT + (h + 1) * HEAD_DIM].astype(BF16)
            s = _dot_nt(q, k)
            p = jnp.exp(s - jnp.max(s, axis=1, keepdims=True))
            o_ref[rs, sl] = _dot(p.astype(BF16), v) / jnp.sum(p, axis=1, keepdims=True)


def _cross_attn(q, mem_kv, out_buf, row0, batch, seq, tq, nseq):
    n = q.shape[0]
    nq = seq // tq
    assert nseq == 1 or nq == 1
    rows = nseq * tq
    rb0 = row0 // rows
    slab_rows = mem_kv.ndim == 4
    kv_spec = pl.BlockSpec((nseq,) + mem_kv.shape[1:], lambda b, i: (b,) + (0,) * (mem_kv.ndim - 1))
    in_specs = [pl.BlockSpec((rows, ATT_OUT), lambda b, i: (rb0 + b * nq + i, 0)), kv_spec]
    args = [q, mem_kv]
    aliases = {}
    if out_buf is not None:
        aliases = {2: 0}
        in_specs.append(pl.BlockSpec(memory_space=pl.ANY))
        args.append(out_buf)
    return pl.pallas_call(
        functools.partial(_cross_body, slab_rows=slab_rows, nseq=nseq),
        grid=(batch // nseq, nq),
        in_specs=in_specs,
        out_specs=pl.BlockSpec((rows, ATT_OUT), lambda b, i: (rb0 + b * nq + i, 0)),
        out_shape=jax.ShapeDtypeStruct((n, ATT_OUT), F32),
        input_output_aliases=aliases,
        compiler_params=_params("arbitrary", "arbitrary"),
        name=f"cross_attn_t{tq}",
    )(*args)


def _router_body(x_ref, g_ref, whi_ref, wlo_ref, b_ref, tri_ref, xs_ref, route_ref, cnt_ref, base_scr):
    tm = x_ref.shape[0]

    @pl.when(pl.program_id(0) == 0)
    def _():
        base_scr[...] = jnp.zeros_like(base_scr)

    x = x_ref[...]
    ms = jnp.mean(x * x, axis=-1, keepdims=True)
    xn = x * lax.rsqrt(ms + EPS) * g_ref[...]
    for c in range(ROW_SLAB):
        xs_ref[pl.ds(c, tm, stride=ROW_SLAB), :] = xn[:, c * HEAD_DIM:(c + 1) * HEAD_DIM]
    hi = xn.astype(BF16)
    lo = (xn - hi.astype(F32)).astype(BF16)
    logits = _dot(hi, whi_ref[...]) + _dot(lo, whi_ref[...]) + _dot(hi, wlo_ref[...]) + b_ref[...]
    lane = lax.broadcasted_iota(jnp.int32, logits.shape, 1)
    big = jnp.int32(1 << 20)
    is_g = jnp.logical_and(lane >= N_EXPERTS, lane < N_EXPERTS + N_GROUPS)
    lg = jnp.where(is_g, logits, NEG)
    mg = jnp.max(lg, axis=1, keepdims=True)
    p_top = 1.0 / jnp.sum(jnp.where(is_g, jnp.exp(lg - mg), 0.0), axis=1, keepdims=True)
    g_idx = jnp.min(jnp.where(jnp.logical_and(is_g, lg == mg), lane, big), axis=1, keepdims=True) - N_EXPERTS
    in_grp = jnp.logical_and(lane < N_EXPERTS, lane // EXPERTS_PER_GROUP == g_idx)
    le = jnp.where(in_grp, logits, NEG)
    v1 = jnp.max(le, axis=1, keepdims=True)
    i1 = jnp.min(jnp.where(jnp.logical_and(in_grp, le == v1), lane, big), axis=1, keepdims=True)
    rest = jnp.logical_and(in_grp, lane != i1)
    le2 = jnp.where(rest, logits, NEG)
    v2 = jnp.max(le2, axis=1, keepdims=True)
    i2 = jnp.min(jnp.where(jnp.logical_and(rest, le2 == v2), lane, big), axis=1, keepdims=True)
    e2 = jnp.exp(v2 - v1)
    w1 = p_top / (1.0 + e2)
    w2 = p_top * e2 / (1.0 + e2)
    hit1, hit2 = lane == i1, lane == i2
    hits = jnp.where(jnp.logical_or(hit1, hit2), 1.0, 0.0)
    before = _dot(tri_ref[...], hits.astype(BF16)) + base_scr[...]
    r1 = jnp.sum(jnp.where(hit1, before, 0.0), axis=1, keepdims=True)
    r2 = jnp.sum(jnp.where(hit2, before, 0.0), axis=1, keepdims=True)
    base_scr[...] += jnp.sum(hits, axis=0, keepdims=True)
    cnt_ref[...] = base_scr[...]
    cols = (i1.astype(F32), i2.astype(F32), w1, w2, r1, r2)
    route = jnp.zeros(logits.shape, F32)
    for c, val in enumerate(cols):
        route = jnp.where(lane == c, val, route)
    route_ref[...] = route


def _router(x, g, w_hi, w_lo, bias, tm):
    m, d = x.shape
    lanes = w_hi.shape[1]
    const = lambda shape: pl.BlockSpec(shape, lambda i: (0, 0))
    tri = jnp.tril(jnp.ones((tm, tm), F32), -1).astype(BF16)
    return pl.pallas_call(
        _router_body,
        grid=(m // tm,),
        in_specs=[pl.BlockSpec((tm, d), lambda i: (i, 0)), const((1, d)), const((d, lanes)), const((d, lanes)),
                  const((1, lanes)), const((tm, tm))],
        out_specs=[pl.BlockSpec((tm * ROW_SLAB, HEAD_DIM), lambda i: (i, 0)),
                   pl.BlockSpec((tm, lanes), lambda i: (i, 0)), const((1, lanes))],
        out_shape=[jax.ShapeDtypeStruct((m * ROW_SLAB, HEAD_DIM), F32), jax.ShapeDtypeStruct((m, lanes), F32),
                   jax.ShapeDtypeStruct((1, lanes), F32)],
        scratch_shapes=[pltpu.VMEM((1, lanes), F32)],
        compiler_params=_params("arbitrary"),
        name="router",
    )(x, g.reshape(1, d), w_hi, w_lo, bias, tri)


def _gather_rows(idx_ref, src_ref, dst_ref, sem, n):
    def issue(p, carry):
        r = pl.multiple_of(idx_ref[0, p] * ROW_SLAB, ROW_SLAB)
        o = pl.multiple_of(p * ROW_SLAB, ROW_SLAB)
        pltpu.make_async_copy(src_ref.at[pl.ds(r, ROW_SLAB), :], dst_ref.at[pl.ds(o, ROW_SLAB), :], sem).start()
        return carry

    lax.fori_loop(0, n, issue, 0, unroll=8)


def _wait_rows(src_ref, dst_ref, sem):
    pltpu.make_async_copy(src_ref.at[pl.ds(0, dst_ref.shape[0]), :], dst_ref, sem).wait()


def _slab_to_rows(ref, rows, first=0):
    return jnp.concatenate([ref[pl.ds(first * ROW_SLAB + c, rows, stride=ROW_SLAB), :] for c in range(ROW_SLAB)],
                           axis=1)


def _tile_idx_specs(tile, steps):
    def cur(t, *_):
        return (t, 0, 0)

    def nxt(t, *_):
        return (jnp.minimum(t + 1, steps - 1), 0, 0)

    return [pl.BlockSpec((None, 1, tile), cur, memory_space=pltpu.SMEM),
            pl.BlockSpec((None, 1, tile), nxt, memory_space=pltpu.SMEM)]


def _experts_body(te_ref, na_ref, idx_ref, idx_next_ref, xs_ref, wg_ref, wu_ref, wd_ref, o_ref,
                  xb0, xb1, sem, wgb, wub, wdb, *, te_rows):
    t = pl.program_id(0)
    na = na_ref[0]
    active = t < na
    bufs = (xb0, xb1)

    @pl.when(jnp.logical_and(t == 0, active))
    def _():
        _gather_rows(idx_ref, xs_ref, xb0, sem.at[0], te_rows)

    for s in range(2):
        @pl.when(jnp.logical_and(t + 1 < na, (t + 1) % 2 == s))
        def _(s=s):
            _gather_rows(idx_next_ref, xs_ref, bufs[s], sem.at[s], te_rows)

    changed = jnp.logical_or(t == 0, te_ref[t] != te_ref[jnp.maximum(t - 1, 0)])

    @pl.when(jnp.logical_and(active, changed))
    def _():
        wgb[...] = wg_ref[...].astype(BF16)
        wub[...] = wu_ref[...].astype(BF16)
        wdb[...] = wd_ref[...].astype(BF16)

    for s in range(2):
        @pl.when(jnp.logical_and(active, t % 2 == s))
        def _(s=s):
            _wait_rows(xs_ref, bufs[s], sem.at[s])
            x = _slab_to_rows(bufs[s], te_rows).astype(BF16)
            h = _silu(_dot(x, wgb[...])) * _dot(x, wub[...])
            o = _dot(h.astype(BF16), wdb[...])
            for c in range(ROW_SLAB):
                o_ref[pl.ds(c, te_rows, stride=ROW_SLAB), :] = o[:, c * HEAD_DIM:(c + 1) * HEAD_DIM]

    @pl.when(jnp.logical_not(active))
    def _():
        o_ref[...] = jnp.zeros_like(o_ref)


def _experts(xs, src_tok, tile_expert, n_active, w_gate, w_up, w_down, te_rows):
    ne, d, ff = w_gate.shape
    nt = tile_expert.shape[0]
    wspec = lambda shape: pl.BlockSpec((None,) + shape, lambda t, te, na: (te[t], 0, 0))
    grid_spec = pltpu.PrefetchScalarGridSpec(
        num_scalar_prefetch=2,
        grid=(nt,),
        in_specs=_tile_idx_specs(te_rows, nt) + [pl.BlockSpec(memory_space=pl.ANY),
                                                  wspec((d, ff)), wspec((d, ff)), wspec((ff, d))],
        out_specs=pl.BlockSpec((te_rows * ROW_SLAB, HEAD_DIM), lambda t, te, na: (t, 0)),
        scratch_shapes=[pltpu.VMEM((te_rows * ROW_SLAB, HEAD_DIM), F32),
                        pltpu.VMEM((te_rows * ROW_SLAB, HEAD_DIM), F32),
                        pltpu.SemaphoreType.DMA((2,)),
                        pltpu.VMEM((d, ff), BF16), pltpu.VMEM((d, ff), BF16), pltpu.VMEM((ff, d), BF16)],
    )
    idx = src_tok.reshape(nt, 1, te_rows)
    return pl.pallas_call(
        functools.partial(_experts_body, te_rows=te_rows),
        grid_spec=grid_spec,
        out_shape=jax.ShapeDtypeStruct((nt * te_rows * ROW_SLAB, HEAD_DIM), F32),
        compiler_params=_params("arbitrary"),
        name="experts",
    )(tile_expert, n_active, idx, idx, xs, w_gate, w_up, w_down)


def _moe_out_body(idx_ref, idx_next_ref, x_ref, route_ref, ys_ref, gf_ref, yp_ref, ysm_ref, db0, db1, sem,
                  *, n_prompt_tiles):
    i = pl.program_id(0)
    nt = pl.num_programs(0)
    tm = x_ref.shape[0]
    bufs = (db0, db1)

    @pl.when(i == 0)
    def _():
        _gather_rows(idx_ref, ys_ref, db0, sem.at[0], 2 * tm)

    for s in range(2):
        @pl.when(jnp.logical_and(i + 1 < nt, (i + 1) % 2 == s))
        def _(s=s):
            _gather_rows(idx_next_ref, ys_ref, bufs[s], sem.at[s], 2 * tm)

    route = route_ref[...]
    w1, w2 = route[:, 2:3], route[:, 3:4]
    for s in range(2):
        @pl.when(i % 2 == s)
        def _(s=s):
            _wait_rows(ys_ref, bufs[s], sem.at[s])
            x = x_ref[...] + w1 * _slab_to_rows(bufs[s], tm) + w2 * _slab_to_rows(bufs[s], tm, first=tm)
            ms = jnp.mean(x * x, axis=-1, keepdims=True)
            y = x * lax.rsqrt(ms + EPS) * gf_ref[...]

            @pl.when(i < n_prompt_tiles)
            def _():
                yp_ref[...] = y

            @pl.when(i >= n_prompt_tiles)
            def _():
                ysm_ref[...] = y


def _moe_out(x, route, ys_sorted, pos1, pos2, g_final, n_prompt, tm):
    m, d = x.shape
    nt = m // tm
    npt = n_prompt // tm
    lanes = route.shape[1]
    idx = jnp.concatenate([pos1.reshape(nt, tm), pos2.reshape(nt, tm)], axis=1).reshape(nt, 1, 2 * tm)
    return pl.pallas_call(
        functools.partial(_moe_out_body, n_prompt_tiles=npt),
        grid=(nt,),
        in_specs=_tile_idx_specs(2 * tm, nt) + [
            pl.BlockSpec((tm, d), lambda i: (i, 0)), pl.BlockSpec((tm, lanes), lambda i: (i, 0)),
            pl.BlockSpec(memory_space=pl.ANY), pl.BlockSpec((1, d), lambda i: (0, 0))],
        out_specs=[pl.BlockSpec((tm, d), lambda i: (jnp.minimum(i, npt - 1), 0)),
                   pl.BlockSpec((tm, d), lambda i: (jnp.maximum(i - npt, 0), 0))],
        out_shape=[jax.ShapeDtypeStruct((n_prompt, d), F32), jax.ShapeDtypeStruct((m - n_prompt, d), F32)],
        scratch_shapes=[pltpu.VMEM((2 * tm * ROW_SLAB, HEAD_DIM), F32),
                        pltpu.VMEM((2 * tm * ROW_SLAB, HEAD_DIM), F32),
                        pltpu.SemaphoreType.DMA((2,))],
        compiler_params=_params("arbitrary"),
        name="moe_out",
    )(idx, idx, x, route, ys_sorted, g_final.reshape(1, d))


def _moe(x, g_ffn, w_rg, b_rg, w_re, b_re, w_gate, w_up, w_down, g_final, n_prompt):
    m, d = x.shape
    lanes = HEAD_DIM
    npad = lanes - N_EXPERTS - N_GROUPS
    w_r = jnp.concatenate([w_re, w_rg, jnp.zeros((d, npad), F32)], axis=1)
    b_r = jnp.concatenate([b_re, b_rg, jnp.zeros((npad,), F32)]).reshape(1, lanes)
    w_r_hi = w_r.astype(BF16)
    w_r_lo = (w_r - w_r_hi.astype(F32)).astype(BF16)
    xs, route, counts = _router(x, g_ffn, w_r_hi, w_r_lo, b_r, 512)

    te = MOE_TILE
    n_tiles = -(-(2 * m + N_EXPERTS * (te - 1)) // te)
    n_tiles += n_tiles % 2
    counts = counts[0, :N_EXPERTS].astype(jnp.int32)
    padded = (counts + te - 1) // te * te
    pad_end = jnp.cumsum(padded)
    pad_off = pad_end - padded
    experts = jnp.arange(N_EXPERTS, dtype=jnp.int32)

    def dest(col_e, col_r):
        e = route[:, col_e].astype(jnp.int32)
        off = jnp.sum(jnp.where(e[:, None] == experts[None, :], pad_off[None, :], 0), axis=1)
        return off + route[:, col_r].astype(jnp.int32)

    pos1, pos2 = dest(0, 4), dest(1, 5)
    tok = jnp.tile(jnp.arange(m, dtype=jnp.int32), 2)
    src_tok = jnp.zeros((n_tiles * te,), jnp.int32).at[jnp.concatenate([pos1, pos2])].set(tok)
    tile_start = jnp.arange(n_tiles, dtype=jnp.int32) * te
    tile_expert = jnp.minimum(jnp.sum(tile_start[:, None] >= pad_end[None, :], axis=1), N_EXPERTS - 1)
    n_active = (pad_end[-1] // te).reshape(1)

    ys_sorted = _experts(xs, src_tok, tile_expert.astype(jnp.int32), n_active.astype(jnp.int32),
                         w_gate, w_up, w_down, te)
    return _moe_out(x, route, ys_sorted, pos1, pos2, g_final, n_prompt, 256)


def _rope_tables(positions):
    half = HEAD_DIM // 2
    inv_freq = ROPE_THETA ** (-jnp.arange(half, dtype=F32) / half)
    ang = positions.astype(F32)[:, None] * inv_freq[None, :]
    cos, sin = jnp.cos(ang), jnp.sin(ang)
    return jnp.concatenate([cos, cos], axis=1), jnp.concatenate([-sin, sin], axis=1)


def kernel(x_prompt, x_sample, cache_swa1, cache_swa2, cache_swa3, state_hgrn, cache_mem_kv, mem_prompt,
           hgrn_lb_logits, norm_mix, w_in, w_proj_attn, w_proj_hgrn, w_out, hgrn_norm, norm_cross, norm_mem,
           w_cq, w_ckv, w_co, norm_ffn, w_rg, b_rg, w_re, b_re, w_e_gate, w_e_up, w_e_down, norm_final):
    bp, seq, d = x_prompt.shape
    bs, dseq, _ = x_sample.shape
    depth = w_in.shape[0]
    assert depth == 1, "single-layer trunk"
    past = cache_swa3.shape[2]
    mem_len = mem_prompt.shape[1]
    np_, ns = bp * seq, bs * dseq
    hw = HGRN_HEADS * HEAD_DIM
    col_hgrn = 3 * ATT_WIDTH
    col_ga = col_hgrn + 4 * hw
    col_gh = col_ga + d
    l = 0

    x_all = jnp.concatenate([x_prompt.reshape(np_, d), x_sample.reshape(ns, d)], axis=0)
    pos = jnp.concatenate([jnp.tile(jnp.arange(seq, dtype=jnp.int32), bp),
                           jnp.tile(past + jnp.arange(dseq, dtype=jnp.int32), bs)])
    cos2, sin2 = _rope_tables(pos)

    xn = _rmsnorm(x_all, norm_mix[l], BF16, 512)
    proj = _matmul(xn, w_in[l], F32, 512, ATT_WIDTH, rope=(cos2, sin2), name="proj_in")

    attn = _attn_prompt(proj, bp, seq, 256)
    attn = _attn_sample(proj, (cache_swa1[l], cache_swa2[l], cache_swa3[l]), attn, np_, bs, dseq)

    hgrn, st_p = _hgrn(proj, col_hgrn, hgrn_lb_logits, hgrn_norm[l], None, None, 0, bp, seq, 128, 2)
    hgrn, st_s = _hgrn(proj, col_hgrn, hgrn_lb_logits, hgrn_norm[l], state_hgrn[l], hgrn, np_, bs, dseq, dseq, 4)

    merged = _gated_merge(attn, hgrn, w_proj_attn[l], w_proj_hgrn[l], proj, col_ga, col_gh, 512, 1024, 512)
    x1 = _matmul(merged, w_out[l], F32, 512, 1024, res=x_all, name="proj_out")

    mem_n = _rmsnorm(mem_prompt.reshape(bp * mem_len, d), norm_mem[l], BF16, 256)
    mkv_p = _matmul(mem_n, w_ckv[l], F32, 256, KV_ROW, name="mem_kv")
    xc = _rmsnorm(x1, norm_cross[l], BF16, 512)
    qc = _matmul(xc, w_cq[l], F32, 512, ATT_OUT, name="cross_q")
    oc = _cross_attn(qc, mkv_p.reshape(bp, mem_len, KV_ROW), None, 0, bp, seq, 512, 1)
    oc = _cross_attn(qc, cache_mem_kv[l].reshape(bs, mem_len, KV_SLAB, HEAD_DIM), oc, np_, bs, dseq, dseq, 4)
    x2 = _matmul(oc, w_co[l], F32, 512, 1024, res=x1, name="cross_out")

    ff = w_e_gate.shape[-1]
    y_p, y_s = _moe(x2, norm_ffn[l], w_rg[l], b_rg[l], w_re[l], b_re[l],
                    w_e_gate[l].reshape(N_EXPERTS, d, ff), w_e_up[l].reshape(N_EXPERTS, d, ff),
                    w_e_down[l].reshape(N_EXPERTS, ff, d), norm_final, np_)

    swa_p = [_kv_rows(proj, g, 0, bp, seq, min(w, seq), min(w, seq, 256))[None]
             for g, (w, _) in enumerate(ATT_GROUPS)]
    swa_s = [_kv_rows(proj, g, np_, 1, ns, ns, 256).reshape(1, bs, dseq, 2, ATT_HEADS, HEAD_DIM)
             for g in range(len(ATT_GROUPS))]
    return (y_p.reshape(bp, seq, d), y_s.reshape(bs, dseq, d),
            swa_p[0], swa_p[1], swa_p[2], st_p[None],
            mkv_p.reshape(1, bp, mem_len, 2, ATT_HEADS, HEAD_DIM),
            swa_s[0], swa_s[1], swa_s[2], st_s[None])
```

```python
import functools

import jax
import jax.numpy as jnp
from jax import lax
from jax.experimental import pallas as pl
from jax.experimental.pallas import tpu as pltpu

F32 = jnp.float32
BF16 = jnp.bfloat16

ATT_GROUPS = ((128, 1), (512, 4), (2048, 16))
ATT_HEADS = 4
HEAD_DIM = 128
ATT_OUT = ATT_HEADS * HEAD_DIM
ATT_WIDTH = len(ATT_GROUPS) * ATT_OUT
KV_ROW = 2 * ATT_OUT
KV_SLAB = 2 * ATT_HEADS
HGRN_HEADS = 16
ROPE_THETA = 10000.0
EPS = 1e-6
N_EXPERTS = 32
N_GROUPS = 4
EXPERTS_PER_GROUP = 8
NEG = -1e30
ATT_BLOCK = 128
HGRN_SUB = 8
ROW_SLAB = 16
MOE_TILE = 512
VMEM_LIMIT = 56 * 1024 * 1024


def _params(*sem):
    return pltpu.CompilerParams(dimension_semantics=sem, vmem_limit_bytes=VMEM_LIMIT)


def _dot(a, b):
    return jnp.dot(a, b, preferred_element_type=F32)


def _dot_nt(a, b):
    return lax.dot_general(a, b, (((1,), (1,)), ((), ())), preferred_element_type=F32)


def _sigmoid(x):
    return 1.0 / (1.0 + jnp.exp(-x))


def _silu(x):
    return x * _sigmoid(x)


def _rmsnorm_body(x_ref, g_ref, o_ref):
    x = x_ref[...]
    ms = jnp.mean(x * x, axis=-1, keepdims=True)
    o_ref[...] = (x * lax.rsqrt(ms + EPS) * g_ref[...]).astype(o_ref.dtype)


def _rmsnorm(x, g, out_dtype, tm):
    m, d = x.shape
    return pl.pallas_call(
        _rmsnorm_body,
        grid=(m // tm,),
        in_specs=[pl.BlockSpec((tm, d), lambda i: (i, 0)), pl.BlockSpec((1, d), lambda i: (0, 0))],
        out_specs=pl.BlockSpec((tm, d), lambda i: (i, 0)),
        out_shape=jax.ShapeDtypeStruct((m, d), out_dtype),
        compiler_params=_params("arbitrary"),
        name="rmsnorm",
    )(x, g.reshape(1, d))


def _mm_body(*refs, has_norm, has_res, has_rope):
    refs = list(refs)
    a_ref, w_ref = refs[:2]
    o_ref, wb_ref = refs[-2:]
    extra = refs[2:-2]
    j = pl.program_id(0)

    @pl.when(pl.program_id(1) == 0)
    def _():
        wb_ref[...] = w_ref[...].astype(BF16)

    a = a_ref[...]
    if has_norm:
        g_ref = extra.pop(0)
        ms = jnp.mean(a * a, axis=-1, keepdims=True)
        a = a * lax.rsqrt(ms + EPS) * g_ref[...]
    a = a.astype(BF16)
    if not has_rope:
        acc = _dot(a, wb_ref[...])
        if has_res:
            acc = acc + extra[0][...]
        o_ref[...] = acc.astype(o_ref.dtype)
        return

    cos_ref, sin_ref = extra[-2:]

    @pl.when(j >= 2)
    def _():
        o_ref[...] = _dot(a, wb_ref[...]).astype(o_ref.dtype)

    @pl.when(j < 2)
    def _():
        cos = cos_ref[...]
        sin = sin_ref[...]
        scale = jnp.where(j == 0, HEAD_DIM ** -0.5, 1.0)
        for p in range(0, ATT_WIDTH, 2 * HEAD_DIM):
            acc = _dot(a, wb_ref[:, p:p + 2 * HEAD_DIM])
            for u in range(2):
                x = acc[:, u * HEAD_DIM:(u + 1) * HEAD_DIM]
                sl = slice(p + u * HEAD_DIM, p + (u + 1) * HEAD_DIM)
                o_ref[:, sl] = ((x * cos + pltpu.roll(x, HEAD_DIM // 2, axis=1) * sin) * scale).astype(o_ref.dtype)


def _matmul(a, w, out_dtype, tm, tn, norm=None, res=None, rope=None, name="matmul"):
    m, k = a.shape
    n = w.shape[1]
    in_specs = [pl.BlockSpec((tm, k), lambda j, i: (i, 0)), pl.BlockSpec((k, tn), lambda j, i: (0, j))]
    args = [a, w]
    if norm is not None:
        in_specs.append(pl.BlockSpec((1, k), lambda j, i: (0, 0)))
        args.append(norm.reshape(1, k))
    if res is not None:
        in_specs.append(pl.BlockSpec((tm, tn), lambda j, i: (i, j)))
        args.append(res)
    if rope is not None:
        assert tn == ATT_WIDTH
        in_specs.extend([pl.BlockSpec((tm, HEAD_DIM), lambda j, i: (i, 0))] * 2)
        args.extend(rope)
    return pl.pallas_call(
        functools.partial(_mm_body, has_norm=norm is not None, has_res=res is not None,
                          has_rope=rope is not None),
        grid=(n // tn, m // tm),
        in_specs=in_specs,
        out_specs=pl.BlockSpec((tm, tn), lambda j, i: (i, j)),
        out_shape=jax.ShapeDtypeStruct((m, n), out_dtype),
        scratch_shapes=[pltpu.VMEM((k, tn), BF16)],
        compiler_params=_params("arbitrary", "arbitrary"),
        name=name,
    )(*args)


def _prompt_key_blocks(bq):
    table = []
    for g, (w, _) in enumerate(ATT_GROUPS):
        nback = -(-w // bq)
        table.extend((g, back) for back in range(nback, -1, -1))
    return tuple(table)


def _attn_prompt_body(q_ref, k_ref, v_ref, o_ref, m_scr, l_scr, acc_scr, *, bq, table):
    i = pl.program_id(1)
    j = pl.program_id(2)
    ng = len(ATT_GROUPS)

    @pl.when(j == 0)
    def _():
        m_scr[...] = jnp.full(m_scr.shape, NEG, F32)
        l_scr[...] = jnp.zeros(l_scr.shape, F32)
        acc_scr[...] = jnp.zeros(acc_scr.shape, F32)

    row = lax.broadcasted_iota(jnp.int32, (bq, bq), 0)
    col = lax.broadcasted_iota(jnp.int32, (bq, bq), 1)
    for g, (w, d) in enumerate(ATT_GROUPS):
        j0 = min(jj for jj, (gg, _) in enumerate(table) if gg == g)
        nback = max(back for gg, back in table if gg == g)
        back = nback - (j - j0)

        @pl.when(jnp.logical_and(jnp.logical_and(j >= j0, j <= j0 + nback), i >= back))
        def _(g=g, w=w, d=d, back=back):
            dist = back * bq + row - col
            valid = jnp.logical_and(jnp.logical_and(dist >= 0, dist <= w), (dist & (d - 1)) == 0)
            bias = jnp.where(valid, 0.0, NEG)
            heads = range(ATT_HEADS)
            hs = [slice(h * HEAD_DIM, (h + 1) * HEAD_DIM) for h in heads]
            m_old = [m_scr[g * ATT_HEADS + h] for h in heads]
            l_old = [l_scr[g * ATT_HEADS + h] for h in heads]
            a_old = [acc_scr[g, :, hs[h]] for h in heads]
            qs = [q_ref[:, g * ATT_OUT + h * HEAD_DIM: g * ATT_OUT + (h + 1) * HEAD_DIM].astype(BF16) for h in heads]
            ss = [_dot_nt(qs[h], k_ref[:, hs[h]].astype(BF16)) + bias for h in heads]
            m_new = [jnp.maximum(m_old[h], jnp.max(ss[h], axis=1, keepdims=True)) for h in heads]
            ps = [jnp.exp(ss[h] - jnp.concatenate([m_new[h]] * (bq // HEAD_DIM), axis=1)) for h in heads]
            alpha = [jnp.exp(m_old[h] - m_new[h]) for h in heads]
            pv = [_dot(ps[h].astype(BF16), v_ref[:, hs[h]].astype(BF16)) for h in heads]
            for h in heads:
                m_scr[g * ATT_HEADS + h] = m_new[h]
                l_scr[g * ATT_HEADS + h] = alpha[h] * l_old[h] + jnp.sum(ps[h], axis=1, keepdims=True)
                acc_scr[g, :, hs[h]] = alpha[h] * a_old[h] + pv[h]

    @pl.when(j == pl.num_programs(2) - 1)
    def _():
        for h in range(ATT_HEADS):
            sl = slice(h * HEAD_DIM, (h + 1) * HEAD_DIM)
            lse = [m_scr[g * ATT_HEADS + h] + jnp.log(l_scr[g * ATT_HEADS + h]) for g in range(ng)]
            mx = functools.reduce(jnp.maximum, lse)
            ws = [jnp.exp(x - mx) for x in lse]
            num = sum(ws[g] / l_scr[g * ATT_HEADS + h] * acc_scr[g, :, sl] for g in range(ng))
            o_ref[:, sl] = num / sum(ws)


def _attn_prompt(proj, batch, seq, bq):
    n = proj.shape[0]
    ng = len(ATT_GROUPS)
    nq = seq // bq
    table = _prompt_key_blocks(bq)
    groups = jnp.asarray([g for g, _ in table], jnp.int32)
    backs = jnp.asarray([b for _, b in table], jnp.int32)

    def kv_map(part):
        def index(b, i, j, g_ref, back_ref):
            return (b * nq + jnp.maximum(i - back_ref[j], 0), part * ng + g_ref[j])
        return index

    grid_spec = pltpu.PrefetchScalarGridSpec(
        num_scalar_prefetch=2,
        grid=(batch, nq, len(table)),
        in_specs=[pl.BlockSpec((bq, ATT_WIDTH), lambda b, i, j, g_ref, back_ref: (b * nq + i, 0)),
                  pl.BlockSpec((bq, ATT_OUT), kv_map(1)),
                  pl.BlockSpec((bq, ATT_OUT), kv_map(2))],
        out_specs=pl.BlockSpec((bq, ATT_OUT), lambda b, i, j, g_ref, back_ref: (b * nq + i, 0)),
        scratch_shapes=[pltpu.VMEM((ng * ATT_HEADS, bq, HEAD_DIM), F32),
                        pltpu.VMEM((ng * ATT_HEADS, bq, HEAD_DIM), F32),
                        pltpu.VMEM((ng, bq, ATT_OUT), F32)],
    )

    def body(g_ref, back_ref, *refs):
        del g_ref, back_ref
        _attn_prompt_body(*refs, bq=bq, table=table)

    return pl.pallas_call(
        body,
        grid_spec=grid_spec,
        out_shape=jax.ShapeDtypeStruct((n, ATT_OUT), F32),
        compiler_params=_params("arbitrary", "arbitrary", "arbitrary"),
        name="attn_prompt",
    )(groups, backs, proj, proj, proj)


def _head_lanes(ref2d, first, rows):
    return jnp.concatenate([ref2d[pl.ds(first + h, rows, stride=KV_SLAB), :] for h in range(ATT_HEADS)], axis=1)


def _attn_sample_body(q_ref, kn_ref, vn_ref, *rest, tq):
    cache_refs, o_ref = rest[:-2], rest[-1]
    nrow = ATT_HEADS * tq
    nkey = ATT_BLOCK
    rid = lax.broadcasted_iota(jnp.int32, (nrow, 1), 0)
    i_row = rid % tq
    head_row = rid // tq
    lane_head = lax.broadcasted_iota(jnp.int32, (1, ATT_OUT), 1) // HEAD_DIM
    head_mask = head_row == lane_head
    key = lax.broadcasted_iota(jnp.int32, (1, nkey), 1)
    pad = jnp.zeros((nkey - tq, ATT_OUT), F32)
    outs, lses = [], []
    ci = 0
    for g, (_, d) in enumerate(ATT_GROUPS):
        qg = q_ref[:, g * ATT_OUT:(g + 1) * ATT_OUT]
        qrows = jnp.where(head_mask, jnp.concatenate([qg] * ATT_HEADS, axis=0), 0.0).astype(BF16)
        k_new = jnp.concatenate([kn_ref[:, g * ATT_OUT:(g + 1) * ATT_OUT], pad], axis=0)
        v_new = jnp.concatenate([vn_ref[:, g * ATT_OUT:(g + 1) * ATT_OUT], pad], axis=0)
        valid_new = jnp.logical_and(jnp.logical_and(key < tq, key <= i_row), (key % d) == (i_row % d))
        blocks = [(k_new, v_new, valid_new)]
        for r in range(min(d, tq)):
            c2 = cache_refs[ci].reshape(nkey * KV_SLAB, HEAD_DIM)
            ci += 1
            valid = jnp.logical_and((i_row % d) == r, key >= i_row // d)
            blocks.append((_head_lanes(c2, 0, nkey), _head_lanes(c2, ATT_HEADS, nkey), valid))
        ss = [jnp.where(valid, _dot_nt(qrows, kb.astype(BF16)), NEG) for kb, _, valid in blocks]
        m = functools.reduce(jnp.maximum, [jnp.max(s, axis=1, keepdims=True) for s in ss])
        ps = [jnp.exp(s - m) for s in ss]
        l = sum(jnp.sum(p, axis=1, keepdims=True) for p in ps)
        acc = sum(_dot(p.astype(BF16), vb.astype(BF16)) for p, (_, vb, _) in zip(ps, blocks))
        o = acc / l
        lse = m + jnp.log(l)
        outs.append(jnp.concatenate(
            [o[h * tq:(h + 1) * tq, h * HEAD_DIM:(h + 1) * HEAD_DIM] for h in range(ATT_HEADS)], axis=1))
        lses.append(jnp.concatenate(
            [jnp.broadcast_to(lse[h * tq:(h + 1) * tq], (tq, HEAD_DIM)) for h in range(ATT_HEADS)], axis=1))
    mx = jnp.maximum(jnp.maximum(lses[0], lses[1]), lses[2])
    ws = [jnp.exp(x - mx) for x in lses]
    o_ref[...] = (ws[0] * outs[0] + ws[1] * outs[1] + ws[2] * outs[2]) / (ws[0] + ws[1] + ws[2])


def _attn_sample(proj, caches, attn_buf, row0, batch, tq):
    blk0 = row0 // tq
    views, specs = [], []
    for (w, d), c in zip(ATT_GROUPS, caches):
        assert c.shape[1] == w and w // d == ATT_BLOCK, "window buffers must hold exactly one window"
        view = c.reshape(batch, w // d, d, KV_SLAB, HEAD_DIM)
        for r in range(min(d, tq)):
            views.append(view)
            specs.append(pl.BlockSpec((None, w // d, None, KV_SLAB, HEAD_DIM), lambda b, r=r: (b, 0, r, 0, 0)))
    n_in = 3 + len(views)
    qkv = [pl.BlockSpec((tq, ATT_WIDTH), lambda b, part=part: (blk0 + b, part)) for part in range(3)]
    return pl.pallas_call(
        functools.partial(_attn_sample_body, tq=tq),
        grid=(batch,),
        in_specs=qkv + specs + [pl.BlockSpec(memory_space=pl.ANY)],
        out_specs=pl.BlockSpec((tq, ATT_OUT), lambda b: (blk0 + b, 0)),
        out_shape=jax.ShapeDtypeStruct(attn_buf.shape, F32),
        input_output_aliases={n_in: 0},
        compiler_params=_params("arbitrary"),
        name="attn_sample",
    )(proj, proj, proj, *views, attn_buf)


def _kv_rows_body(k_ref, v_ref, o_ref):
    tm = k_ref.shape[0]
    for h in range(ATT_HEADS):
        sl = slice(h * HEAD_DIM, (h + 1) * HEAD_DIM)
        o_ref[pl.ds(h, tm, stride=KV_SLAB), :] = k_ref[:, sl]
        o_ref[pl.ds(ATT_HEADS + h, tm, stride=KV_SLAB), :] = v_ref[:, sl]


def _kv_rows(proj, g, row0, batch, seq, keep, tm):
    ng = len(ATT_GROUPS)
    nt = keep // tm
    rb0 = (row0 + seq - keep) // tm
    per_seq = seq // tm
    out = pl.pallas_call(
        _kv_rows_body,
        grid=(batch, nt),
        in_specs=[pl.BlockSpec((tm, ATT_OUT), lambda b, t: (rb0 + b * per_seq + t, ng + g)),
                  pl.BlockSpec((tm, ATT_OUT), lambda b, t: (rb0 + b * per_seq + t, 2 * ng + g))],
        out_specs=pl.BlockSpec((tm * KV_SLAB, HEAD_DIM), lambda b, t: (b * nt + t, 0)),
        out_shape=jax.ShapeDtypeStruct((batch * keep * KV_SLAB, HEAD_DIM), F32),
        compiler_params=_params("arbitrary", "arbitrary"),
        name=f"kv_rows_g{g}_{tm}",
    )(proj, proj)
    return out.reshape(batch, keep, 2, ATT_HEADS, HEAD_DIM)


def _cumsum_rows(x, seg):
    row = lax.broadcasted_iota(jnp.int32, (x.shape[0], 1), 0) % seg
    sh = 1
    while sh < seg:
        x = x + jnp.where(row >= sh, pltpu.roll(x, sh, axis=0), 0.0)
        sh *= 2
    return x


def _bcast_rows(b, first, period):
    c, w = b.shape
    parts = [jnp.broadcast_to(b[p * period + first: p * period + first + 1, :], (period, w))
             for p in range(c // period)]
    return parts[0] if len(parts) == 1 else jnp.concatenate(parts, axis=0)


def _hgrn_body(*refs, chunk, nseq, has_s0, chained):
    refs = list(refs)
    qh_ref, fh_ref, ih_ref, og_ref, lbl_ref, gn_ref = refs[:6]
    s0_ref = refs[6] if has_s0 else None
    o_ref, so_ref, st_ref = refs[-3:]
    c = chunk
    nh = ATT_HEADS
    t = pl.program_id(2)

    if chained:
        @pl.when(t == 0)
        def _():
            for h in range(nh):
                st_ref[h] = s0_ref[0, h].T if has_s0 else jnp.zeros((HEAD_DIM, HEAD_DIM), F32)

    lbl = lbl_ref[...]
    e = jnp.exp(lbl - jnp.max(lbl, axis=0, keepdims=True))
    lb = e[0:1, :] / jnp.sum(e, axis=0, keepdims=True)
    f = lb + (1.0 - lb) * _sigmoid(fh_ref[...])
    kk = 1.0 - f
    qf = _silu(qh_ref[...])
    b = _cumsum_rows(jnp.log(f), c)
    vv = ih_ref[...]

    row = lax.broadcasted_iota(jnp.int32, (c, c), 0)
    col = lax.broadcasted_iota(jnp.int32, (c, c), 1)
    rid = lax.broadcasted_iota(jnp.int32, (nseq * c, 1), 0)
    seqs = [slice(q * c, (q + 1) * c) for q in range(nseq)]
    heads = [slice(h * HEAD_DIM, (h + 1) * HEAD_DIM) for h in range(nh)]

    a = [[jnp.zeros((c, c), F32) for _ in range(nh)] for _ in range(nseq)]
    s = c // 2
    while s >= HGRN_SUB:
        ref = _bcast_rows(b, s - 1, 2 * s)
        second = (rid % (2 * s)) >= s
        dlt = b - ref
        ee = jnp.exp(jnp.where(second, dlt, -dlt))
        ql = jnp.where(second, qf * ee, 0.0).astype(BF16)
        kl = jnp.where(second, 0.0, kk * ee).astype(BF16)
        same = (row // (2 * s)) == (col // (2 * s))
        for q in range(nseq):
            for h in range(nh):
                a[q][h] = a[q][h] + jnp.where(same, _dot_nt(ql[seqs[q], heads[h]], kl[seqs[q], heads[h]]), 0.0)
        s //= 2
    sub = min(HGRN_SUB, c)
    ref = _bcast_rows(b, 0, sub)
    dlt = b - ref
    qd = (qf * jnp.exp(dlt)).astype(BF16)
    kd = (kk * jnp.exp(jnp.minimum(-dlt, 80.0))).astype(BF16)
    diag = jnp.logical_and((row // sub) == (col // sub), col <= row)
    qe = (qf * jnp.exp(b)).astype(BF16)
    b_last = _bcast_rows(b, c - 1, c)
    kend = (kk * jnp.exp(b_last - b)).astype(BF16)
    dec = jnp.exp(b_last)
    gn = gn_ref[...]
    gate = _silu(og_ref[...])
    vb = vv.astype(BF16)
    eye = (lax.broadcasted_iota(jnp.int32, (HEAD_DIM, HEAD_DIM), 0)
           == lax.broadcasted_iota(jnp.int32, (HEAD_DIM, HEAD_DIM), 1))
    tn_dims = (((0,), (0,)), ((), ()))
    for h in range(nh):
        sl = heads[h]
        st = st_ref[h] if chained else None
        for q in range(nseq):
            rs = seqs[q]
            ah = a[q][h] + jnp.where(diag, _dot_nt(qd[rs, sl], kd[rs, sl]), 0.0)
            intra = _dot(ah.astype(BF16), vb[rs, sl])
            dec_q = dec[q * c:q * c + 1, sl]
            if chained:
                o = _dot_nt(qe[rs, sl], st.astype(BF16)) + intra
                st = st * dec_q + lax.dot_general(vb[rs, sl], kend[rs, sl], tn_dims, preferred_element_type=F32)
            else:
                s_kv = s0_ref[q, h]
                o = _dot(qe[rs, sl], s_kv.astype(BF16)) + intra
                dec_col = jnp.sum(jnp.where(eye, jnp.broadcast_to(dec_q, (HEAD_DIM, HEAD_DIM)), 0.0),
                                  axis=1, keepdims=True)
                so_ref[q, h] = s_kv * dec_col + lax.dot_general(kend[rs, sl], vb[rs, sl], tn_dims,
                                                                preferred_element_type=F32)
            ms = jnp.mean(o * o, axis=-1, keepdims=True)
            o_ref[rs, sl] = (o * lax.rsqrt(ms + EPS) * gn * gate[rs, sl]).astype(o_ref.dtype)
        if chained:
            st_ref[h] = st

            @pl.when(t == pl.num_programs(2) - 1)
            def _(h=h, st=st):
                so_ref[0, h] = st.T


def _hgrn(proj, col0, lb_logits, gnorm, s0, out_buf, row0, batch, seq, chunk, nseq):
    n = proj.shape[0]
    chained = seq != chunk
    assert chained or s0 is not None
    nt = seq // (chunk * nseq) if chained else 1
    nstate = 1 if chained else nseq
    hb = HGRN_HEADS // ATT_HEADS
    width = HGRN_HEADS * HEAD_DIM
    cb0 = col0 // ATT_OUT
    rows = nseq * chunk
    rb0 = row0 // rows

    def col(j):
        return pl.BlockSpec((rows, ATT_OUT), lambda b, h, t, j=j: (rb0 + b * nt + t, cb0 + j * hb + h))

    in_specs = [col(0), col(1), col(2), col(3),
                pl.BlockSpec((lb_logits.shape[0], ATT_OUT), lambda b, h, t: (0, h)),
                pl.BlockSpec((1, HEAD_DIM), lambda b, h, t: (0, 0))]
    args = [proj, proj, proj, proj, lb_logits, gnorm.reshape(1, HEAD_DIM)]
    state_spec = pl.BlockSpec((nstate, ATT_HEADS, HEAD_DIM, HEAD_DIM), lambda b, h, t: (b, h, 0, 0))
    if s0 is not None:
        in_specs.append(state_spec)
        args.append(s0)
    aliases = {}
    if out_buf is not None:
        aliases = {len(args): 0}
        in_specs.append(pl.BlockSpec(memory_space=pl.ANY))
        args.append(out_buf)
    return pl.pallas_call(
        functools.partial(_hgrn_body, chunk=chunk, nseq=nseq, has_s0=s0 is not None, chained=chained),
        grid=(batch // nstate, hb, nt),
        in_specs=in_specs,
        out_specs=[pl.BlockSpec((rows, ATT_OUT), lambda b, h, t: (rb0 + b * nt + t, h)), state_spec],
        out_shape=[jax.ShapeDtypeStruct((n, width), BF16),
                   jax.ShapeDtypeStruct((batch, HGRN_HEADS, HEAD_DIM, HEAD_DIM), F32)],
        scratch_shapes=[pltpu.VMEM((ATT_HEADS, HEAD_DIM, HEAD_DIM), F32)],
        input_output_aliases=aliases,
        compiler_params=_params("arbitrary", "arbitrary", "arbitrary"),
        name=f"hgrn_c{chunk}",
    )(*args)


def _gated_merge_body(att_ref, hg_ref, wa_ref, wh_ref, *rest, ngate):
    ga_refs, gh_refs = rest[:ngate], rest[ngate:2 * ngate]
    o_ref, wab_ref, whb_ref = rest[2 * ngate:]

    @pl.when(pl.program_id(1) == 0)
    def _():
        wab_ref[...] = wa_ref[...].astype(BF16)
        whb_ref[...] = wh_ref[...].astype(BF16)

    pa = _dot(att_ref[...].astype(BF16), wab_ref[...])
    ph = _dot(hg_ref[...].astype(BF16), whb_ref[...])
    ga = jnp.concatenate([r[...] for r in ga_refs], axis=1)
    gh = jnp.concatenate([r[...] for r in gh_refs], axis=1)
    o_ref[...] = (_sigmoid(ga) * pa + _sigmoid(gh) * ph).astype(o_ref.dtype)


def _gated_merge(attn, hgrn, w_pa, w_ph, proj, col_ga, col_gh, tm, tn, tg):
    m = attn.shape[0]
    n = w_pa.shape[1]
    ka, kh = w_pa.shape[0], w_ph.shape[0]
    assert col_ga % tg == 0 and col_gh % tg == 0 and tn % tg == 0, "gate columns must start on a gate block"
    ngate = tn // tg
    gate = lambda c0, u: pl.BlockSpec((tm, tg), lambda j, i: (i, c0 // tg + j * ngate + u))
    gates = [gate(col_ga, u) for u in range(ngate)] + [gate(col_gh, u) for u in range(ngate)]
    return pl.pallas_call(
        functools.partial(_gated_merge_body, ngate=ngate),
        grid=(n // tn, m // tm),
        in_specs=[pl.BlockSpec((tm, ka), lambda j, i: (i, 0)),
                  pl.BlockSpec((tm, kh), lambda j, i: (i, 0)),
                  pl.BlockSpec((ka, tn), lambda j, i: (0, j)),
                  pl.BlockSpec((kh, tn), lambda j, i: (0, j))] + gates,
        out_specs=pl.BlockSpec((tm, tn), lambda j, i: (i, j)),
        out_shape=jax.ShapeDtypeStruct((m, n), BF16),
        scratch_shapes=[pltpu.VMEM((ka, tn), BF16), pltpu.VMEM((kh, tn), BF16)],
        compiler_params=_params("arbitrary", "arbitrary"),
        name="gated_merge",
    )(attn, hgrn, w_pa, w_ph, *([proj] * (2 * ngate)))


def _cross_body(q_ref, kv_ref, *rest, slab_rows, nseq):
    o_ref = rest[-1]
    scale = HEAD_DIM ** -0.5
    tq = q_ref.shape[0] // nseq
    mem_len = kv_ref.shape[1]
    for b in range(nseq):
        rs = slice(b * tq, (b + 1) * tq)
        kv = kv_ref.at[b]
        if slab_rows:
            kv = kv.reshape(mem_len * KV_SLAB, HEAD_DIM)
        for h in range(ATT_HEADS):
            sl = slice(h * HEAD_DIM, (h + 1) * HEAD_DIM)
            q = (q_ref[rs, sl] * scale).astype(BF16)
            if slab_rows:
                k = kv[pl.ds(h, mem_len, stride=KV_SLAB), :].astype(BF16)
                v = kv[pl.ds(ATT_HEADS + h, mem_len, stride=KV_SLAB), :].astype(BF16)
            else:
                k = kv[:, sl].astype(BF16)
                v = kv[:, ATT_OUT + h * HEAD_DIM: ATT_OUT + (h + 1) * HEAD_DIM].astype(BF16)
            s = _dot_nt(q, k)
            p = jnp.exp(s - jnp.max(s, axis=1, keepdims=True))
            o_ref[rs, sl] = _dot(p.astype(BF16), v) / jnp.sum(p, axis=1, keepdims=True)


def _cross_attn(q, mem_kv, out_buf, row0, batch, seq, tq, nseq):
    n = q.shape[0]
    nq = seq // tq
    assert nseq == 1 or nq == 1
    rows = nseq * tq
    rb0 = row0 // rows
    slab_rows = mem_kv.ndim == 4
    kv_spec = pl.BlockSpec((nseq,) + mem_kv.shape[1:], lambda b, i: (b,) + (0,) * (mem_kv.ndim - 1))
    in_specs = [pl.BlockSpec((rows, ATT_OUT), lambda b, i: (rb0 + b * nq + i, 0)), kv_spec]
    args = [q, mem_kv]
    aliases = {}
    if out_buf is not None:
        aliases = {2: 0}
        in_specs.append(pl.BlockSpec(memory_space=pl.ANY))
        args.append(out_buf)
    return pl.pallas_call(
        functools.partial(_cross_body, slab_rows=slab_rows, nseq=nseq),
        grid=(batch // nseq, nq),
        in_specs=in_specs,
        out_specs=pl.BlockSpec((rows, ATT_OUT), lambda b, i: (rb0 + b * nq + i, 0)),
        out_shape=jax.ShapeDtypeStruct((n, ATT_OUT), F32),
        input_output_aliases=aliases,
        compiler_params=_params("arbitrary", "arbitrary"),
        name=f"cross_attn_t{tq}",
    )(*args)


def _router_body(x_ref, g_ref, whi_ref, wlo_ref, b_ref, tri_ref, xs_ref, route_ref, cnt_ref, base_scr):
    tm = x_ref.shape[0]

    @pl.when(pl.program_id(0) == 0)
    def _():
        base_scr[...] = jnp.zeros_like(base_scr)

    x = x_ref[...]
    ms = jnp.mean(x * x, axis=-1, keepdims=True)
    xn = x * lax.rsqrt(ms + EPS) * g_ref[...]
    for c in range(ROW_SLAB):
        xs_ref[pl.ds(c, tm, stride=ROW_SLAB), :] = xn[:, c * HEAD_DIM:(c + 1) * HEAD_DIM]
    hi = xn.astype(BF16)
    lo = (xn - hi.astype(F32)).astype(BF16)
    logits = _dot(hi, whi_ref[...]) + _dot(lo, whi_ref[...]) + _dot(hi, wlo_ref[...]) + b_ref[...]
    lane = lax.broadcasted_iota(jnp.int32, logits.shape, 1)
    big = jnp.int32(1 << 20)
    is_g = jnp.logical_and(lane >= N_EXPERTS, lane < N_EXPERTS + N_GROUPS)
    lg = jnp.where(is_g, logits, NEG)
    mg = jnp.max(lg, axis=1, keepdims=True)
    p_top = 1.0 / jnp.sum(jnp.where(is_g, jnp.exp(lg - mg), 0.0), axis=1, keepdims=True)
    g_idx = jnp.min(jnp.where(jnp.logical_and(is_g, lg == mg), lane, big), axis=1, keepdims=True) - N_EXPERTS
    in_grp = jnp.logical_and(lane < N_EXPERTS, lane // EXPERTS_PER_GROUP == g_idx)
    le = jnp.where(in_grp, logits, NEG)
    v1 = jnp.max(le, axis=1, keepdims=True)
    i1 = jnp.min(jnp.where(jnp.logical_and(in_grp, le == v1), lane, big), axis=1, keepdims=True)
    rest = jnp.logical_and(in_grp, lane != i1)
    le2 = jnp.where(rest, logits, NEG)
    v2 = jnp.max(le2, axis=1, keepdims=True)
    i2 = jnp.min(jnp.where(jnp.logical_and(rest, le2 == v2), lane, big), axis=1, keepdims=True)
    e2 = jnp.exp(v2 - v1)
    w1 = p_top / (1.0 + e2)
    w2 = p_top * e2 / (1.0 + e2)
    hit1, hit2 = lane == i1, lane == i2
    hits = jnp.where(jnp.logical_or(hit1, hit2), 1.0, 0.0)
    before = _dot(tri_ref[...], hits.astype(BF16)) + base_scr[...]
    r1 = jnp.sum(jnp.where(hit1, before, 0.0), axis=1, keepdims=True)
    r2 = jnp.sum(jnp.where(hit2, before, 0.0), axis=1, keepdims=True)
    base_scr[...] += jnp.sum(hits, axis=0, keepdims=True)
    cnt_ref[...] = base_scr[...]
    cols = (i1.astype(F32), i2.astype(F32), w1, w2, r1, r2)
    route = jnp.zeros(logits.shape, F32)
    for c, val in enumerate(cols):
        route = jnp.where(lane == c, val, route)
    route_ref[...] = route


def _router(x, g, w_hi, w_lo, bias, tm):
    m, d = x.shape
    lanes = w_hi.shape[1]
    const = lambda shape: pl.BlockSpec(shape, lambda i: (0, 0))
    tri = jnp.tril(jnp.ones((tm, tm), F32), -1).astype(BF16)
    return pl.pallas_call(
        _router_body,
        grid=(m // tm,),
        in_specs=[pl.BlockSpec((tm, d), lambda i: (i, 0)), const((1, d)), const((d, lanes)), const((d, lanes)),
                  const((1, lanes)), const((tm, tm))],
        out_specs=[pl.BlockSpec((tm * ROW_SLAB, HEAD_DIM), lambda i: (i, 0)),
                   pl.BlockSpec((tm, lanes), lambda i: (i, 0)), const((1, lanes))],
        out_shape=[jax.ShapeDtypeStruct((m * ROW_SLAB, HEAD_DIM), F32), jax.ShapeDtypeStruct((m, lanes), F32),
                   jax.ShapeDtypeStruct((1, lanes), F32)],
        scratch_shapes=[pltpu.VMEM((1, lanes), F32)],
        compiler_params=_params("arbitrary"),
        name="router",
    )(x, g.reshape(1, d), w_hi, w_lo, bias, tri)


def _gather_rows(idx_ref, src_ref, dst_ref, sem, n):
    def issue(p, carry):
        r = pl.multiple_of(idx_ref[0, p] * ROW_SLAB, ROW_SLAB)
        o = pl.multiple_of(p * ROW_SLAB, ROW_SLAB)
        pltpu.make_async_copy(src_ref.at[pl.ds(r, ROW_SLAB), :], dst_ref.at[pl.ds(o, ROW_SLAB), :], sem).start()
        return carry

    lax.fori_loop(0, n, issue, 0, unroll=8)


def _wait_rows(src_ref, dst_ref, sem):
    pltpu.make_async_copy(src_ref.at[pl.ds(0, dst_ref.shape[0]), :], dst_ref, sem).wait()


def _slab_to_rows(ref, rows, first=0):
    return jnp.concatenate([ref[pl.ds(first * ROW_SLAB + c, rows, stride=ROW_SLAB), :] for c in range(ROW_SLAB)],
                           axis=1)


def _tile_idx_specs(tile, steps):
    def cur(t, *_):
        return (t, 0, 0)

    def nxt(t, *_):
        return (jnp.minimum(t + 1, steps - 1), 0, 0)

    return [pl.BlockSpec((None, 1, tile), cur, memory_space=pltpu.SMEM),
            pl.BlockSpec((None, 1, tile), nxt, memory_space=pltpu.SMEM)]


def _experts_body(te_ref, na_ref, idx_ref, idx_next_ref, xs_ref, wg_ref, wu_ref, wd_ref, o_ref,
                  xb0, xb1, sem, wgb, wub, wdb, *, te_rows):
    t = pl.program_id(0)
    na = na_ref[0]
    active = t < na
    bufs = (xb0, xb1)

    @pl.when(jnp.logical_and(t == 0, active))
    def _():
        _gather_rows(idx_ref, xs_ref, xb0, sem.at[0], te_rows)

    for s in range(2):
        @pl.when(jnp.logical_and(t + 1 < na, (t + 1) % 2 == s))
        def _(s=s):
            _gather_rows(idx_next_ref, xs_ref, bufs[s], sem.at[s], te_rows)

    changed = jnp.logical_or(t == 0, te_ref[t] != te_ref[jnp.maximum(t - 1, 0)])

    @pl.when(jnp.logical_and(active, changed))
    def _():
        wgb[...] = wg_ref[...].astype(BF16)
        wub[...] = wu_ref[...].astype(BF16)
        wdb[...] = wd_ref[...].astype(BF16)

    for s in range(2):
        @pl.when(jnp.logical_and(active, t % 2 == s))
        def _(s=s):
            _wait_rows(xs_ref, bufs[s], sem.at[s])
            x = _slab_to_rows(bufs[s], te_rows).astype(BF16)
            h = (_silu(_dot(x, wgb[...])) * _dot(x, wub[...])).astype(BF16)
            for c in range(0, ROW_SLAB, 2):
                o = _dot(h, wdb[:, c * HEAD_DIM:(c + 2) * HEAD_DIM])
                o_ref[pl.ds(c, te_rows, stride=ROW_SLAB), :] = o[:, :HEAD_DIM]
                o_ref[pl.ds(c + 1, te_rows, stride=ROW_SLAB), :] = o[:, HEAD_DIM:]

    @pl.when(jnp.logical_not(active))
    def _():
        o_ref[...] = jnp.zeros_like(o_ref)


def _experts(xs, src_tok, tile_expert, n_active, w_gate, w_up, w_down, te_rows):
    ne, d, ff = w_gate.shape
    nt = tile_expert.shape[0]
    wspec = lambda shape: pl.BlockSpec((None,) + shape, lambda t, te, na: (te[t], 0, 0))
    grid_spec = pltpu.PrefetchScalarGridSpec(
        num_scalar_prefetch=2,
        grid=(nt,),
        in_specs=_tile_idx_specs(te_rows, nt) + [pl.BlockSpec(memory_space=pl.ANY),
                                                  wspec((d, ff)), wspec((d, ff)), wspec((ff, d))],
        out_specs=pl.BlockSpec((te_rows * ROW_SLAB, HEAD_DIM), lambda t, te, na: (t, 0)),
        scratch_shapes=[pltpu.VMEM((te_rows * ROW_SLAB, HEAD_DIM), F32),
                        pltpu.VMEM((te_rows * ROW_SLAB, HEAD_DIM), F32),
                        pltpu.SemaphoreType.DMA((2,)),
                        pltpu.VMEM((d, ff), BF16), pltpu.VMEM((d, ff), BF16), pltpu.VMEM((ff, d), BF16)],
    )
    idx = src_tok.reshape(nt, 1, te_rows)
    return pl.pallas_call(
        functools.partial(_experts_body, te_rows=te_rows),
        grid_spec=grid_spec,
        out_shape=jax.ShapeDtypeStruct((nt * te_rows * ROW_SLAB, HEAD_DIM), F32),
        compiler_params=_params("arbitrary"),
        name="experts",
    )(tile_expert, n_active, idx, idx, xs, w_gate, w_up, w_down)


def _moe_out_body(idx_ref, idx_next_ref, x_ref, route_ref, ys_ref, gf_ref, yp_ref, ysm_ref, db0, db1, sem,
                  *, n_prompt_tiles):
    i = pl.program_id(0)
    nt = pl.num_programs(0)
    tm = x_ref.shape[0]
    bufs = (db0, db1)

    @pl.when(i == 0)
    def _():
        _gather_rows(idx_ref, ys_ref, db0, sem.at[0], 2 * tm)

    for s in range(2):
        @pl.when(jnp.logical_and(i + 1 < nt, (i + 1) % 2 == s))
        def _(s=s):
            _gather_rows(idx_next_ref, ys_ref, bufs[s], sem.at[s], 2 * tm)

    route = route_ref[...]
    w1, w2 = route[:, 2:3], route[:, 3:4]
    for s in range(2):
        @pl.when(i % 2 == s)
        def _(s=s):
            _wait_rows(ys_ref, bufs[s], sem.at[s])
            x = x_ref[...] + w1 * _slab_to_rows(bufs[s], tm) + w2 * _slab_to_rows(bufs[s], tm, first=tm)
            ms = jnp.mean(x * x, axis=-1, keepdims=True)
            y = x * lax.rsqrt(ms + EPS) * gf_ref[...]

            @pl.when(i < n_prompt_tiles)
            def _():
                yp_ref[...] = y

            @pl.when(i >= n_prompt_tiles)
            def _():
                ysm_ref[...] = y


def _moe_out(x, route, ys_sorted, pos1, pos2, g_final, n_prompt, tm):
    m, d = x.shape
    nt = m // tm
    npt = n_prompt // tm
    lanes = route.shape[1]
    idx = jnp.concatenate([pos1.reshape(nt, tm), pos2.reshape(nt, tm)], axis=1).reshape(nt, 1, 2 * tm)
    return pl.pallas_call(
        functools.partial(_moe_out_body, n_prompt_tiles=npt),
        grid=(nt,),
        in_specs=_tile_idx_specs(2 * tm, nt) + [
            pl.BlockSpec((tm, d), lambda i: (i, 0)), pl.BlockSpec((tm, lanes), lambda i: (i, 0)),
            pl.BlockSpec(memory_space=pl.ANY), pl.BlockSpec((1, d), lambda i: (0, 0))],
        out_specs=[pl.BlockSpec((tm, d), lambda i: (jnp.minimum(i, npt - 1), 0)),
                   pl.BlockSpec((tm, d), lambda i: (jnp.maximum(i - npt, 0), 0))],
        out_shape=[jax.ShapeDtypeStruct((n_prompt, d), F32), jax.ShapeDtypeStruct((m - n_prompt, d), F32)],
        scratch_shapes=[pltpu.VMEM((2 * tm * ROW_SLAB, HEAD_DIM), F32),
                        pltpu.VMEM((2 * tm * ROW_SLAB, HEAD_DIM), F32),
                        pltpu.SemaphoreType.DMA((2,))],
        compiler_params=_params("arbitrary"),
        name="moe_out",
    )(idx, idx, x, route, ys_sorted, g_final.reshape(1, d))


def _moe(x, g_ffn, w_rg, b_rg, w_re, b_re, w_gate, w_up, w_down, g_final, n_prompt):
    m, d = x.shape
    lanes = HEAD_DIM
    npad = lanes - N_EXPERTS - N_GROUPS
    w_r = jnp.concatenate([w_re, w_rg, jnp.zeros((d, npad), F32)], axis=1)
    b_r = jnp.concatenate([b_re, b_rg, jnp.zeros((npad,), F32)]).reshape(1, lanes)
    w_r_hi = w_r.astype(BF16)
    w_r_lo = (w_r - w_r_hi.astype(F32)).astype(BF16)
    xs, route, counts = _router(x, g_ffn, w_r_hi, w_r_lo, b_r, 512)

    te = MOE_TILE
    n_tiles = -(-(2 * m + N_EXPERTS * (te - 1)) // te)
    n_tiles += n_tiles % 2
    counts = counts[0, :N_EXPERTS].astype(jnp.int32)
    padded = (counts + te - 1) // te * te
    pad_end = jnp.cumsum(padded)
    pad_off = pad_end - padded
    experts = jnp.arange(N_EXPERTS, dtype=jnp.int32)

    def dest(col_e, col_r):
        e = route[:, col_e].astype(jnp.int32)
        off = jnp.sum(jnp.where(e[:, None] == experts[None, :], pad_off[None, :], 0), axis=1)
        return off + route[:, col_r].astype(jnp.int32)

    pos1, pos2 = dest(0, 4), dest(1, 5)
    tok = jnp.tile(jnp.arange(m, dtype=jnp.int32), 2)
    src_tok = jnp.zeros((n_tiles * te,), jnp.int32).at[jnp.concatenate([pos1, pos2])].set(tok)
    tile_start = jnp.arange(n_tiles, dtype=jnp.int32) * te
    tile_expert = jnp.minimum(jnp.sum(tile_start[:, None] >= pad_end[None, :], axis=1), N_EXPERTS - 1)
    n_active = (pad_end[-1] // te).reshape(1)

    ys_sorted = _experts(xs, src_tok, tile_expert.astype(jnp.int32), n_active.astype(jnp.int32),
                         w_gate, w_up, w_down, te)
    return _moe_out(x, route, ys_sorted, pos1, pos2, g_final, n_prompt, 256)


def _rope_tables(positions):
    half = HEAD_DIM // 2
    inv_freq = ROPE_THETA ** (-jnp.arange(half, dtype=F32) / half)
    ang = positions.astype(F32)[:, None] * inv_freq[None, :]
    cos, sin = jnp.cos(ang), jnp.sin(ang)
    return jnp.concatenate([cos, cos], axis=1), jnp.concatenate([-sin, sin], axis=1)


def kernel(x_prompt, x_sample, cache_swa1, cache_swa2, cache_swa3, state_hgrn, cache_mem_kv, mem_prompt,
           hgrn_lb_logits, norm_mix, w_in, w_proj_attn, w_proj_hgrn, w_out, hgrn_norm, norm_cross, norm_mem,
           w_cq, w_ckv, w_co, norm_ffn, w_rg, b_rg, w_re, b_re, w_e_gate, w_e_up, w_e_down, norm_final):
    bp, seq, d = x_prompt.shape
    bs, dseq, _ = x_sample.shape
    depth = w_in.shape[0]
    assert depth == 1, "single-layer trunk"
    past = cache_swa3.shape[2]
    mem_len = mem_prompt.shape[1]
    np_, ns = bp * seq, bs * dseq
    hw = HGRN_HEADS * HEAD_DIM
    col_hgrn = 3 * ATT_WIDTH
    col_ga = col_hgrn + 4 * hw
    col_gh = col_ga + d
    l = 0

    x_all = jnp.concatenate([x_prompt.reshape(np_, d), x_sample.reshape(ns, d)], axis=0)
    pos = jnp.concatenate([jnp.tile(jnp.arange(seq, dtype=jnp.int32), bp),
                           jnp.tile(past + jnp.arange(dseq, dtype=jnp.int32), bs)])
    cos2, sin2 = _rope_tables(pos)

    xn = _rmsnorm(x_all, norm_mix[l], BF16, 512)
    proj = _matmul(xn, w_in[l], F32, 512, ATT_WIDTH, rope=(cos2, sin2), name="proj_in")

    attn = _attn_prompt(proj, bp, seq, 256)
    attn = _attn_sample(proj, (cache_swa1[l], cache_swa2[l], cache_swa3[l]), attn, np_, bs, dseq)

    hgrn, st_p = _hgrn(proj, col_hgrn, hgrn_lb_logits, hgrn_norm[l], None, None, 0, bp, seq, 128, 2)
    hgrn, st_s = _hgrn(proj, col_hgrn, hgrn_lb_logits, hgrn_norm[l], state_hgrn[l], hgrn, np_, bs, dseq, dseq, 4)

    merged = _gated_merge(attn, hgrn, w_proj_attn[l], w_proj_hgrn[l], proj, col_ga, col_gh, 512, 1024, 512)
    x1 = _matmul(merged, w_out[l], F32, 512, 1024, res=x_all, name="proj_out")

    mem_n = _rmsnorm(mem_prompt.reshape(bp * mem_len, d), norm_mem[l], BF16, 256)
    mkv_p = _matmul(mem_n, w_ckv[l], F32, 256, KV_ROW, name="mem_kv")
    qc = _matmul(x1, w_cq[l], F32, 512, ATT_OUT, norm=norm_cross[l], name="cross_q")
    oc = _cross_attn(qc, mkv_p.reshape(bp, mem_len, KV_ROW), None, 0, bp, seq, 512, 1)
    oc = _cross_attn(qc, cache_mem_kv[l].reshape(bs, mem_len, KV_SLAB, HEAD_DIM), oc, np_, bs, dseq, dseq, 4)
    x2 = _matmul(oc, w_co[l], F32, 512, 1024, res=x1, name="cross_out")

    ff = w_e_gate.shape[-1]
    y_p, y_s = _moe(x2, norm_ffn[l], w_rg[l], b_rg[l], w_re[l], b_re[l],
                    w_e_gate[l].reshape(N_EXPERTS, d, ff), w_e_up[l].reshape(N_EXPERTS, d, ff),
                    w_e_down[l].reshape(N_EXPERTS, ff, d), norm_final, np_)

    swa_p = [_kv_rows(proj, g, 0, bp, seq, min(w, seq), min(w, seq, 256))[None]
             for g, (w, _) in enumerate(ATT_GROUPS)]
    swa_s = [_kv_rows(proj, g, np_, 1, ns, ns, 256).reshape(1, bs, dseq, 2, ATT_HEADS, HEAD_DIM)
             for g in range(len(ATT_GROUPS))]
    return (y_p.reshape(bp, seq, d), y_s.reshape(bs, dseq, d),
            swa_p[0], swa_p[1], swa_p[2], st_p[None],
            mkv_p.reshape(1, bp, mem_len, 2, ATT_HEADS, HEAD_DIM),
            swa_s[0], swa_s[1], swa_s[2], st_s[None])
```

```python
import functools

import jax
import jax.numpy as jnp
from jax import lax
from jax.experimental import pallas as pl
from jax.experimental.pallas import tpu as pltpu

F32 = jnp.float32
BF16 = jnp.bfloat16

ATT_GROUPS = ((128, 1), (512, 4), (2048, 16))
ATT_HEADS = 4
HEAD_DIM = 128
ATT_OUT = ATT_HEADS * HEAD_DIM
ATT_WIDTH = len(ATT_GROUPS) * ATT_OUT
KV_ROW = 2 * ATT_OUT
KV_SLAB = 2 * ATT_HEADS
HGRN_HEADS = 16
ROPE_THETA = 10000.0
EPS = 1e-6
N_EXPERTS = 32
N_GROUPS = 4
EXPERTS_PER_GROUP = 8
NEG = -1e30
ATT_BLOCK = 128
HGRN_SUB = 8
ROW_SLAB = 16
MOE_TILE = 128
VMEM_LIMIT = 56 * 1024 * 1024


def _params(*sem):
    return pltpu.CompilerParams(dimension_semantics=sem, vmem_limit_bytes=VMEM_LIMIT)


def _dot(a, b):
    return jnp.dot(a, b, preferred_element_type=F32)


def _dot_nt(a, b):
    return lax.dot_general(a, b, (((1,), (1,)), ((), ())), preferred_element_type=F32)


def _sigmoid(x):
    return 1.0 / (1.0 + jnp.exp(-x))


def _silu(x):
    return x * _sigmoid(x)


def _rmsnorm_body(x_ref, g_ref, o_ref):
    x = x_ref[...]
    ms = jnp.mean(x * x, axis=-1, keepdims=True)
    o_ref[...] = (x * lax.rsqrt(ms + EPS) * g_ref[...]).astype(o_ref.dtype)


def _rmsnorm(x, g, out_dtype, tm):
    m, d = x.shape
    return pl.pallas_call(
        _rmsnorm_body,
        grid=(m // tm,),
        in_specs=[pl.BlockSpec((tm, d), lambda i: (i, 0)), pl.BlockSpec((1, d), lambda i: (0, 0))],
        out_specs=pl.BlockSpec((tm, d), lambda i: (i, 0)),
        out_shape=jax.ShapeDtypeStruct((m, d), out_dtype),
        compiler_params=_params("arbitrary"),
        name="rmsnorm",
    )(x, g.reshape(1, d))


def _rmsnorm2_body(xa_ref, xb_ref, g_ref, o_ref, *, na_tiles):
    i = pl.program_id(0)
    for ref, cond in ((xa_ref, i < na_tiles), (xb_ref, i >= na_tiles)):
        @pl.when(cond)
        def _(ref=ref):
            x = ref[...]
            ms = jnp.mean(x * x, axis=-1, keepdims=True)
            o_ref[...] = (x * lax.rsqrt(ms + EPS) * g_ref[...]).astype(o_ref.dtype)


def _two_part_specs(tm, width, na_tiles, index=lambda i: i, col=lambda i: 0):
    return [pl.BlockSpec((tm, width), lambda *a: (jnp.minimum(index(*a), na_tiles - 1), col(*a))),
            pl.BlockSpec((tm, width), lambda *a: (jnp.maximum(index(*a) - na_tiles, 0), col(*a)))]


def _rmsnorm2(xa, xb, g, out_dtype, tm):
    (ma, d), mb = xa.shape, xb.shape[0]
    na_tiles = ma // tm
    return pl.pallas_call(
        functools.partial(_rmsnorm2_body, na_tiles=na_tiles),
        grid=((ma + mb) // tm,),
        in_specs=_two_part_specs(tm, d, na_tiles) + [pl.BlockSpec((1, d), lambda i: (0, 0))],
        out_specs=pl.BlockSpec((tm, d), lambda i: (i, 0)),
        out_shape=jax.ShapeDtypeStruct((ma + mb, d), out_dtype),
        compiler_params=_params("arbitrary"),
        name="rmsnorm2",
    )(xa, xb, g.reshape(1, d))


def _mm_body(*refs, has_norm, has_res, has_rope, res_a_tiles):
    refs = list(refs)
    a_ref, w_ref = refs[:2]
    o_ref, wb_ref = refs[-2:]
    extra = refs[2:-2]
    j = pl.program_id(0)

    @pl.when(pl.program_id(1) == 0)
    def _():
        wb_ref[...] = w_ref[...].astype(BF16)

    a = a_ref[...]
    if has_norm:
        g_ref = extra.pop(0)
        ms = jnp.mean(a * a, axis=-1, keepdims=True)
        a = a * lax.rsqrt(ms + EPS) * g_ref[...]
    a = a.astype(BF16)
    if not has_rope:
        acc = _dot(a, wb_ref[...])
        if has_res and res_a_tiles is None:
            acc = acc + extra[0][...]
        if has_res and res_a_tiles is not None:
            i = pl.program_id(1)
            for r_ref, cond in ((extra[0], i < res_a_tiles), (extra[1], i >= res_a_tiles)):
                @pl.when(cond)
                def _(r_ref=r_ref):
                    o_ref[...] = (acc + r_ref[...]).astype(o_ref.dtype)
            return
        o_ref[...] = acc.astype(o_ref.dtype)
        return

    cos_ref, sin_ref = extra[-2:]

    @pl.when(j >= 2)
    def _():
        o_ref[...] = _dot(a, wb_ref[...]).astype(o_ref.dtype)

    @pl.when(j < 2)
    def _():
        cos = cos_ref[...]
        sin = sin_ref[...]
        scale = jnp.where(j == 0, HEAD_DIM ** -0.5, 1.0)
        for p in range(0, ATT_WIDTH, 2 * HEAD_DIM):
            acc = _dot(a, wb_ref[:, p:p + 2 * HEAD_DIM])
            for u in range(2):
                x = acc[:, u * HEAD_DIM:(u + 1) * HEAD_DIM]
                sl = slice(p + u * HEAD_DIM, p + (u + 1) * HEAD_DIM)
                o_ref[:, sl] = ((x * cos + pltpu.roll(x, HEAD_DIM // 2, axis=1) * sin) * scale).astype(o_ref.dtype)


def _matmul(a, w, out_dtype, tm, tn, norm=None, res=None, rope=None, name="matmul"):
    m, k = a.shape
    n = w.shape[1]
    in_specs = [pl.BlockSpec((tm, k), lambda j, i: (i, 0)), pl.BlockSpec((k, tn), lambda j, i: (0, j))]
    args = [a, w]
    if norm is not None:
        in_specs.append(pl.BlockSpec((1, k), lambda j, i: (0, 0)))
        args.append(norm.reshape(1, k))
    res_a_tiles = None
    if isinstance(res, tuple):
        res_a_tiles = res[0].shape[0] // tm
        in_specs.extend(_two_part_specs(tm, tn, res_a_tiles, index=lambda j, i: i, col=lambda j, i: j))
        args.extend(res)
    elif res is not None:
        in_specs.append(pl.BlockSpec((tm, tn), lambda j, i: (i, j)))
        args.append(res)
    if rope is not None:
        assert tn == ATT_WIDTH
        in_specs.extend([pl.BlockSpec((tm, HEAD_DIM), lambda j, i: (i, 0))] * 2)
        args.extend(rope)
    return pl.pallas_call(
        functools.partial(_mm_body, has_norm=norm is not None, has_res=res is not None,
                          has_rope=rope is not None, res_a_tiles=res_a_tiles),
        grid=(n // tn, m // tm),
        in_specs=in_specs,
        out_specs=pl.BlockSpec((tm, tn), lambda j, i: (i, j)),
        out_shape=jax.ShapeDtypeStruct((m, n), out_dtype),
        scratch_shapes=[pltpu.VMEM((k, tn), BF16)],
        compiler_params=_params("arbitrary", "arbitrary"),
        name=name,
    )(*args)


def _prompt_key_blocks(bq):
    table = []
    for g, (w, _) in enumerate(ATT_GROUPS):
        nback = -(-w // bq)
        table.extend((g, back) for back in range(nback, -1, -1))
    return tuple(table)


def _attn_prompt_body(q_ref, k_ref, v_ref, o_ref, m_scr, l_scr, acc_scr, *, bq, table):
    i = pl.program_id(1)
    j = pl.program_id(2)
    ng = len(ATT_GROUPS)

    @pl.when(j == 0)
    def _():
        m_scr[...] = jnp.full(m_scr.shape, NEG, F32)
        l_scr[...] = jnp.zeros(l_scr.shape, F32)
        acc_scr[...] = jnp.zeros(acc_scr.shape, F32)

    row = lax.broadcasted_iota(jnp.int32, (bq, bq), 0)
    col = lax.broadcasted_iota(jnp.int32, (bq, bq), 1)
    for g, (w, d) in enumerate(ATT_GROUPS):
        j0 = min(jj for jj, (gg, _) in enumerate(table) if gg == g)
        nback = max(back for gg, back in table if gg == g)
        back = nback - (j - j0)

        @pl.when(jnp.logical_and(jnp.logical_and(j >= j0, j <= j0 + nback), i >= back))
        def _(g=g, w=w, d=d, back=back):
            dist = back * bq + row - col
            valid = jnp.logical_and(jnp.logical_and(dist >= 0, dist <= w), (dist & (d - 1)) == 0)
            bias = jnp.where(valid, 0.0, NEG)
            heads = range(ATT_HEADS)
            hs = [slice(h * HEAD_DIM, (h + 1) * HEAD_DIM) for h in heads]
            m_old = [m_scr[g * ATT_HEADS + h] for h in heads]
            l_old = [l_scr[g * ATT_HEADS + h] for h in heads]
            a_old = [acc_scr[g, :, hs[h]] for h in heads]
            qs = [q_ref[:, g * ATT_OUT + h * HEAD_DIM: g * ATT_OUT + (h + 1) * HEAD_DIM].astype(BF16) for h in heads]
            ss = [_dot_nt(qs[h], k_ref[:, hs[h]].astype(BF16)) + bias for h in heads]
            m_new = [jnp.maximum(m_old[h], jnp.max(ss[h], axis=1, keepdims=True)) for h in heads]
            ps = [jnp.exp(ss[h] - jnp.concatenate([m_new[h]] * (bq // HEAD_DIM), axis=1)) for h in heads]
            alpha = [jnp.exp(m_old[h] - m_new[h]) for h in heads]
            pv = [_dot(ps[h].astype(BF16), v_ref[:, hs[h]].astype(BF16)) for h in heads]
            for h in heads:
                m_scr[g * ATT_HEADS + h] = m_new[h]
                l_scr[g * ATT_HEADS + h] = alpha[h] * l_old[h] + jnp.sum(ps[h], axis=1, keepdims=True)
                acc_scr[g, :, hs[h]] = alpha[h] * a_old[h] + pv[h]

    @pl.when(j == pl.num_programs(2) - 1)
    def _():
        for h in range(ATT_HEADS):
            sl = slice(h * HEAD_DIM, (h + 1) * HEAD_DIM)
            lse = [m_scr[g * ATT_HEADS + h] + jnp.log(l_scr[g * ATT_HEADS + h]) for g in range(ng)]
            mx = functools.reduce(jnp.maximum, lse)
            ws = [jnp.exp(x - mx) for x in lse]
            num = sum(ws[g] / l_scr[g * ATT_HEADS + h] * acc_scr[g, :, sl] for g in range(ng))
            o_ref[:, sl] = num / sum(ws)


def _attn_prompt(proj, batch, seq, bq):
    n = proj.shape[0]
    ng = len(ATT_GROUPS)
    nq = seq // bq
    table = _prompt_key_blocks(bq)
    groups = jnp.asarray([g for g, _ in table], jnp.int32)
    backs = jnp.asarray([b for _, b in table], jnp.int32)

    def kv_map(part):
        def index(b, i, j, g_ref, back_ref):
            return (b * nq + jnp.maximum(i - back_ref[j], 0), part * ng + g_ref[j])
        return index

    grid_spec = pltpu.PrefetchScalarGridSpec(
        num_scalar_prefetch=2,
        grid=(batch, nq, len(table)),
        in_specs=[pl.BlockSpec((bq, ATT_WIDTH), lambda b, i, j, g_ref, back_ref: (b * nq + i, 0)),
                  pl.BlockSpec((bq, ATT_OUT), kv_map(1)),
                  pl.BlockSpec((bq, ATT_OUT), kv_map(2))],
        out_specs=pl.BlockSpec((bq, ATT_OUT), lambda b, i, j, g_ref, back_ref: (b * nq + i, 0)),
        scratch_shapes=[pltpu.VMEM((ng * ATT_HEADS, bq, HEAD_DIM), F32),
                        pltpu.VMEM((ng * ATT_HEADS, bq, HEAD_DIM), F32),
                        pltpu.VMEM((ng, bq, ATT_OUT), F32)],
    )

    def body(g_ref, back_ref, *refs):
        del g_ref, back_ref
        _attn_prompt_body(*refs, bq=bq, table=table)

    return pl.pallas_call(
        body,
        grid_spec=grid_spec,
        out_shape=jax.ShapeDtypeStruct((n, ATT_OUT), F32),
        compiler_params=_params("arbitrary", "arbitrary", "arbitrary"),
        name="attn_prompt",
    )(groups, backs, proj, proj, proj)


def _head_lanes(ref2d, first, rows):
    return jnp.concatenate([ref2d[pl.ds(first + h, rows, stride=KV_SLAB), :] for h in range(ATT_HEADS)], axis=1)


def _attn_sample_body(q_ref, kn_ref, vn_ref, *rest, tq):
    cache_refs, o_ref = rest[:-2], rest[-1]
    nrow = ATT_HEADS * tq
    nkey = ATT_BLOCK
    rid = lax.broadcasted_iota(jnp.int32, (nrow, 1), 0)
    i_row = rid % tq
    head_row = rid // tq
    lane_head = lax.broadcasted_iota(jnp.int32, (1, ATT_OUT), 1) // HEAD_DIM
    head_mask = head_row == lane_head
    key = lax.broadcasted_iota(jnp.int32, (1, nkey), 1)
    pad = jnp.zeros((nkey - tq, ATT_OUT), F32)
    outs, lses = [], []
    ci = 0
    for g, (_, d) in enumerate(ATT_GROUPS):
        qg = q_ref[:, g * ATT_OUT:(g + 1) * ATT_OUT]
        qrows = jnp.where(head_mask, jnp.concatenate([qg] * ATT_HEADS, axis=0), 0.0).astype(BF16)
        k_new = jnp.concatenate([kn_ref[:, g * ATT_OUT:(g + 1) * ATT_OUT], pad], axis=0)
        v_new = jnp.concatenate([vn_ref[:, g * ATT_OUT:(g + 1) * ATT_OUT], pad], axis=0)
        valid_new = jnp.logical_and(jnp.logical_and(key < tq, key <= i_row), (key % d) == (i_row % d))
        blocks = [(k_new, v_new, valid_new)]
        for r in range(min(d, tq)):
            c2 = cache_refs[ci].reshape(nkey * KV_SLAB, HEAD_DIM)
            ci += 1
            valid = jnp.logical_and((i_row % d) == r, key >= i_row // d)
            blocks.append((_head_lanes(c2, 0, nkey), _head_lanes(c2, ATT_HEADS, nkey), valid))
        ss = [jnp.where(valid, _dot_nt(qrows, kb.astype(BF16)), NEG) for kb, _, valid in blocks]
        m = functools.reduce(jnp.maximum, [jnp.max(s, axis=1, keepdims=True) for s in ss])
        ps = [jnp.exp(s - m) for s in ss]
        l = sum(jnp.sum(p, axis=1, keepdims=True) for p in ps)
        acc = sum(_dot(p.astype(BF16), vb.astype(BF16)) for p, (_, vb, _) in zip(ps, blocks))
        o = acc / l
        lse = m + jnp.log(l)
        outs.append(jnp.concatenate(
            [o[h * tq:(h + 1) * tq, h * HEAD_DIM:(h + 1) * HEAD_DIM] for h in range(ATT_HEADS)], axis=1))
        lses.append(jnp.concatenate(
            [jnp.broadcast_to(lse[h * tq:(h + 1) * tq], (tq, HEAD_DIM)) for h in range(ATT_HEADS)], axis=1))
    mx = jnp.maximum(jnp.maximum(lses[0], lses[1]), lses[2])
    ws = [jnp.exp(x - mx) for x in lses]
    o_ref[...] = (ws[0] * outs[0] + ws[1] * outs[1] + ws[2] * outs[2]) / (ws[0] + ws[1] + ws[2])


def _attn_sample(proj, caches, attn_buf, row0, batch, tq):
    blk0 = row0 // tq
    views, specs = [], []
    for (w, d), c in zip(ATT_GROUPS, caches):
        assert c.shape[1] == w and w // d == ATT_BLOCK, "window buffers must hold exactly one window"
        view = c.reshape(batch, w // d, d, KV_SLAB, HEAD_DIM)
        for r in range(min(d, tq)):
            views.append(view)
            specs.append(pl.BlockSpec((None, w // d, None, KV_SLAB, HEAD_DIM), lambda b, r=r: (b, 0, r, 0, 0)))
    n_in = 3 + len(views)
    qkv = [pl.BlockSpec((tq, ATT_WIDTH), lambda b, part=part: (blk0 + b, part)) for part in range(3)]
    return pl.pallas_call(
        functools.partial(_attn_sample_body, tq=tq),
        grid=(batch,),
        in_specs=qkv + specs + [pl.BlockSpec(memory_space=pl.ANY)],
        out_specs=pl.BlockSpec((tq, ATT_OUT), lambda b: (blk0 + b, 0)),
        out_shape=jax.ShapeDtypeStruct(attn_buf.shape, F32),
        input_output_aliases={n_in: 0},
        compiler_params=_params("arbitrary"),
        name="attn_sample",
    )(proj, proj, proj, *views, attn_buf)


def _kv_rows_body(k_ref, v_ref, o_ref):
    tm = k_ref.shape[0]
    for h in range(ATT_HEADS):
        sl = slice(h * HEAD_DIM, (h + 1) * HEAD_DIM)
        o_ref[pl.ds(h, tm, stride=KV_SLAB), :] = k_ref[:, sl]
        o_ref[pl.ds(ATT_HEADS + h, tm, stride=KV_SLAB), :] = v_ref[:, sl]


def _kv_rows(proj, g, row0, batch, seq, keep, tm):
    ng = len(ATT_GROUPS)
    nt = keep // tm
    rb0 = (row0 + seq - keep) // tm
    per_seq = seq // tm
    out = pl.pallas_call(
        _kv_rows_body,
        grid=(batch, nt),
        in_specs=[pl.BlockSpec((tm, ATT_OUT), lambda b, t: (rb0 + b * per_seq + t, ng + g)),
                  pl.BlockSpec((tm, ATT_OUT), lambda b, t: (rb0 + b * per_seq + t, 2 * ng + g))],
        out_specs=pl.BlockSpec((tm * KV_SLAB, HEAD_DIM), lambda b, t: (b * nt + t, 0)),
        out_shape=jax.ShapeDtypeStruct((batch * keep * KV_SLAB, HEAD_DIM), F32),
        compiler_params=_params("arbitrary", "arbitrary"),
        name=f"kv_rows_g{g}_{tm}",
    )(proj, proj)
    return out.reshape(batch, keep, 2, ATT_HEADS, HEAD_DIM)


def _cumsum_rows(x, seg):
    row = lax.broadcasted_iota(jnp.int32, (x.shape[0], 1), 0) % seg
    sh = 1
    while sh < seg:
        x = x + jnp.where(row >= sh, pltpu.roll(x, sh, axis=0), 0.0)
        sh *= 2
    return x


def _bcast_rows(b, first, period):
    c, w = b.shape
    parts = [jnp.broadcast_to(b[p * period + first: p * period + first + 1, :], (period, w))
             for p in range(c // period)]
    return parts[0] if len(parts) == 1 else jnp.concatenate(parts, axis=0)


def _hgrn_body(*refs, chunk, nseq, has_s0, chained):
    refs = list(refs)
    qh_ref, fh_ref, ih_ref, og_ref, lbl_ref, gn_ref = refs[:6]
    s0_ref = refs[6] if has_s0 else None
    o_ref, so_ref, st_ref = refs[-3:]
    c = chunk
    nh = ATT_HEADS
    t = pl.program_id(2)

    if chained:
        @pl.when(t == 0)
        def _():
            for h in range(nh):
                st_ref[h] = s0_ref[0, h].T if has_s0 else jnp.zeros((HEAD_DIM, HEAD_DIM), F32)

    lbl = lbl_ref[...]
    e = jnp.exp(lbl - jnp.max(lbl, axis=0, keepdims=True))
    lb = e[0:1, :] / jnp.sum(e, axis=0, keepdims=True)
    f = lb + (1.0 - lb) * _sigmoid(fh_ref[...])
    kk = 1.0 - f
    qf = _silu(qh_ref[...])
    b = _cumsum_rows(jnp.log(f), c)
    vv = ih_ref[...]

    row = lax.broadcasted_iota(jnp.int32, (c, c), 0)
    col = lax.broadcasted_iota(jnp.int32, (c, c), 1)
    rid = lax.broadcasted_iota(jnp.int32, (nseq * c, 1), 0)
    seqs = [slice(q * c, (q + 1) * c) for q in range(nseq)]
    heads = [slice(h * HEAD_DIM, (h + 1) * HEAD_DIM) for h in range(nh)]

    a = [[jnp.zeros((c, c), F32) for _ in range(nh)] for _ in range(nseq)]
    s = c // 2
    while s >= HGRN_SUB:
        ref = _bcast_rows(b, s - 1, 2 * s)
        second = (rid % (2 * s)) >= s
        dlt = b - ref
        ee = jnp.exp(jnp.where(second, dlt, -dlt))
        ql = jnp.where(second, qf * ee, 0.0).astype(BF16)
        kl = jnp.where(second, 0.0, kk * ee).astype(BF16)
        same = (row // (2 * s)) == (col // (2 * s))
        for q in range(nseq):
            for h in range(nh):
                a[q][h] = a[q][h] + jnp.where(same, _dot_nt(ql[seqs[q], heads[h]], kl[seqs[q], heads[h]]), 0.0)
        s //= 2
    sub = min(HGRN_SUB, c)
    ref = _bcast_rows(b, 0, sub)
    dlt = b - ref
    qd = (qf * jnp.exp(dlt)).astype(BF16)
    kd = (kk * jnp.exp(jnp.minimum(-dlt, 80.0))).astype(BF16)
    diag = jnp.logical_and((row // sub) == (col // sub), col <= row)
    qe = (qf * jnp.exp(b)).astype(BF16)
    b_last = _bcast_rows(b, c - 1, c)
    kend = (kk * jnp.exp(b_last - b)).astype(BF16)
    dec = jnp.exp(b_last)
    gn = gn_ref[...]
    gate = _silu(og_ref[...])
    vb = vv.astype(BF16)
    eye = (lax.broadcasted_iota(jnp.int32, (HEAD_DIM, HEAD_DIM), 0)
           == lax.broadcasted_iota(jnp.int32, (HEAD_DIM, HEAD_DIM), 1))
    tn_dims = (((0,), (0,)), ((), ()))
    for h in range(nh):
        sl = heads[h]
        st = st_ref[h] if chained else None
        for q in range(nseq):
            rs = seqs[q]
            ah = a[q][h] + jnp.where(diag, _dot_nt(qd[rs, sl], kd[rs, sl]), 0.0)
            intra = _dot(ah.astype(BF16), vb[rs, sl])
            dec_q = dec[q * c:q * c + 1, sl]
            if chained:
                o = _dot_nt(qe[rs, sl], st.astype(BF16)) + intra
                st = st * dec_q + lax.dot_general(vb[rs, sl], kend[rs, sl], tn_dims, preferred_element_type=F32)
            else:
                s_kv = s0_ref[q, h]
                o = _dot(qe[rs, sl], s_kv.astype(BF16)) + intra
                dec_col = jnp.sum(jnp.where(eye, jnp.broadcast_to(dec_q, (HEAD_DIM, HEAD_DIM)), 0.0),
                                  axis=1, keepdims=True)
                so_ref[q, h] = s_kv * dec_col + lax.dot_general(kend[rs, sl], vb[rs, sl], tn_dims,
                                                                preferred_element_type=F32)
            ms = jnp.mean(o * o, axis=-1, keepdims=True)
            o_ref[rs, sl] = (o * lax.rsqrt(ms + EPS) * gn * gate[rs, sl]).astype(o_ref.dtype)
        if chained:
            st_ref[h] = st

            @pl.when(t == pl.num_programs(2) - 1)
            def _(h=h, st=st):
                so_ref[0, h] = st.T


def _hgrn(proj, col0, lb_logits, gnorm, s0, out_buf, row0, batch, seq, chunk, nseq):
    n = proj.shape[0]
    chained = seq != chunk
    assert chained or s0 is not None
    nt = seq // (chunk * nseq) if chained else 1
    nstate = 1 if chained else nseq
    hb = HGRN_HEADS // ATT_HEADS
    width = HGRN_HEADS * HEAD_DIM
    cb0 = col0 // ATT_OUT
    rows = nseq * chunk
    rb0 = row0 // rows

    def col(j):
        return pl.BlockSpec((rows, ATT_OUT), lambda b, h, t, j=j: (rb0 + b * nt + t, cb0 + j * hb + h))

    in_specs = [col(0), col(1), col(2), col(3),
                pl.BlockSpec((lb_logits.shape[0], ATT_OUT), lambda b, h, t: (0, h)),
                pl.BlockSpec((1, HEAD_DIM), lambda b, h, t: (0, 0))]
    args = [proj, proj, proj, proj, lb_logits, gnorm.reshape(1, HEAD_DIM)]
    state_spec = pl.BlockSpec((nstate, ATT_HEADS, HEAD_DIM, HEAD_DIM), lambda b, h, t: (b, h, 0, 0))
    if s0 is not None:
        in_specs.append(state_spec)
        args.append(s0)
    aliases = {}
    if out_buf is not None:
        aliases = {len(args): 0}
        in_specs.append(pl.BlockSpec(memory_space=pl.ANY))
        args.append(out_buf)
    return pl.pallas_call(
        functools.partial(_hgrn_body, chunk=chunk, nseq=nseq, has_s0=s0 is not None, chained=chained),
        grid=(batch // nstate, hb, nt),
        in_specs=in_specs,
        out_specs=[pl.BlockSpec((rows, ATT_OUT), lambda b, h, t: (rb0 + b * nt + t, h)), state_spec],
        out_shape=[jax.ShapeDtypeStruct((n, width), BF16),
                   jax.ShapeDtypeStruct((batch, HGRN_HEADS, HEAD_DIM, HEAD_DIM), F32)],
        scratch_shapes=[pltpu.VMEM((ATT_HEADS, HEAD_DIM, HEAD_DIM), F32)],
        input_output_aliases=aliases,
        compiler_params=_params("arbitrary", "arbitrary", "arbitrary"),
        name=f"hgrn_c{chunk}",
    )(*args)


def _gated_merge_body(att_ref, hg_ref, wa_ref, wh_ref, *rest, ngate):
    ga_refs, gh_refs = rest[:ngate], rest[ngate:2 * ngate]
    o_ref, wab_ref, whb_ref = rest[2 * ngate:]

    @pl.when(pl.program_id(1) == 0)
    def _():
        wab_ref[...] = wa_ref[...].astype(BF16)
        whb_ref[...] = wh_ref[...].astype(BF16)

    pa = _dot(att_ref[...].astype(BF16), wab_ref[...])
    ph = _dot(hg_ref[...].astype(BF16), whb_ref[...])
    ga = jnp.concatenate([r[...] for r in ga_refs], axis=1)
    gh = jnp.concatenate([r[...] for r in gh_refs], axis=1)
    o_ref[...] = (_sigmoid(ga) * pa + _sigmoid(gh) * ph).astype(o_ref.dtype)


def _gated_merge(attn, hgrn, w_pa, w_ph, proj, col_ga, col_gh, tm, tn, tg):
    m = attn.shape[0]
    n = w_pa.shape[1]
    ka, kh = w_pa.shape[0], w_ph.shape[0]
    assert col_ga % tg == 0 and col_gh % tg == 0 and tn % tg == 0, "gate columns must start on a gate block"
    ngate = tn // tg
    gate = lambda c0, u: pl.BlockSpec((tm, tg), lambda j, i: (i, c0 // tg + j * ngate + u))
    gates = [gate(col_ga, u) for u in range(ngate)] + [gate(col_gh, u) for u in range(ngate)]
    return pl.pallas_call(
        functools.partial(_gated_merge_body, ngate=ngate),
        grid=(n // tn, m // tm),
        in_specs=[pl.BlockSpec((tm, ka), lambda j, i: (i, 0)),
                  pl.BlockSpec((tm, kh), lambda j, i: (i, 0)),
                  pl.BlockSpec((ka, tn), lambda j, i: (0, j)),
                  pl.BlockSpec((kh, tn), lambda j, i: (0, j))] + gates,
        out_specs=pl.BlockSpec((tm, tn), lambda j, i: (i, j)),
        out_shape=jax.ShapeDtypeStruct((m, n), BF16),
        scratch_shapes=[pltpu.VMEM((ka, tn), BF16), pltpu.VMEM((kh, tn), BF16)],
        compiler_params=_params("arbitrary", "arbitrary"),
        name="gated_merge",
    )(attn, hgrn, w_pa, w_ph, *([proj] * (2 * ngate)))


def _cross_body(q_ref, kv_ref, *rest, slab_rows, nseq):
    o_ref = rest[-1]
    scale = HEAD_DIM ** -0.5
    tq = q_ref.shape[0] // nseq
    mem_len = kv_ref.shape[1]
    if slab_rows:
        nrow = ATT_HEADS * tq
        head_row = lax.broadcasted_iota(jnp.int32, (nrow, 1), 0) // tq
        lane_head = lax.broadcasted_iota(jnp.int32, (1, ATT_OUT), 1) // HEAD_DIM
        head_mask = head_row == lane_head
        for b in range(nseq):
            rs = slice(b * tq, (b + 1) * tq)
            kv = kv_ref.at[b].reshape(mem_len * KV_SLAB, HEAD_DIM)
            qb = q_ref[rs, :] * scale
            qrows = jnp.where(head_mask, jnp.concatenate([qb] * ATT_HEADS, axis=0), 0.0).astype(BF16)
            s = _dot_nt(qrows, _head_lanes(kv, 0, mem_len).astype(BF16))
            p = jnp.exp(s - jnp.max(s, axis=1, keepdims=True))
            o = _dot(p.astype(BF16), _head_lanes(kv, ATT_HEADS, mem_len).astype(BF16))
            o = o / jnp.sum(p, axis=1, keepdims=True)
            o_ref[rs, :] = jnp.concatenate(
                [o[h * tq:(h + 1) * tq, h * HEAD_DIM:(h + 1) * HEAD_DIM] for h in range(ATT_HEADS)], axis=1)
        return
    for b in range(nseq):
        rs = slice(b * tq, (b + 1) * tq)
        kv = kv_ref.at[b]
        for h in range(ATT_HEADS):
            sl = slice(h * HEAD_DIM, (h + 1) * HEAD_DIM)
            q = (q_ref[rs, sl] * scale).astype(BF16)
            k = kv[:, sl].astype(BF16)
            v = kv[:, ATT_OUT + h * HEAD_DIM: ATT_OUT + (h + 1) * HEAD_DIM].astype(BF16)
            s = _dot_nt(q, k)
            p = jnp.exp(s - jnp.max(s, axis=1, keepdims=True))
            o_ref[rs, sl] = _dot(p.astype(BF16), v) / jnp.sum(p, axis=1, keepdims=True)


def _cross_attn(q, mem_kv, out_buf, row0, batch, seq, tq, nseq):
    n = q.shape[0]
    nq = seq // tq
    assert nseq == 1 or nq == 1
    rows = nseq * tq
    rb0 = row0 // rows
    slab_rows = mem_kv.ndim == 4
    kv_spec = pl.BlockSpec((nseq,) + mem_kv.shape[1:], lambda b, i: (b,) + (0,) * (mem_kv.ndim - 1))
    in_specs = [pl.BlockSpec((rows, ATT_OUT), lambda b, i: (rb0 + b * nq + i, 0)), kv_spec]
    args = [q, mem_kv]
    aliases = {}
    if out_buf is not None:
        aliases = {2: 0}
        in_specs.append(pl.BlockSpec(memory_space=pl.ANY))
        args.append(out_buf)
    return pl.pallas_call(
        functools.partial(_cross_body, slab_rows=slab_rows, nseq=nseq),
        grid=(batch // nseq, nq),
        in_specs=in_specs,
        out_specs=pl.BlockSpec((rows, ATT_OUT), lambda b, i: (rb0 + b * nq + i, 0)),
        out_shape=jax.ShapeDtypeStruct((n, ATT_OUT), F32),
        input_output_aliases=aliases,
        compiler_params=_params("arbitrary", "arbitrary"),
        name=f"cross_attn_t{tq}",
    )(*args)


def _router_body(x_ref, g_ref, whi_ref, wlo_ref, b_ref, tri_ref, xs_ref, route_ref, cnt_ref, base_scr):
    tm = x_ref.shape[0]

    @pl.when(pl.program_id(0) == 0)
    def _():
        base_scr[...] = jnp.zeros_like(base_scr)

    x = x_ref[...]
    ms = jnp.mean(x * x, axis=-1, keepdims=True)
    xn = x * lax.rsqrt(ms + EPS) * g_ref[...]
    for c in range(ROW_SLAB):
        xs_ref[pl.ds(c, tm, stride=ROW_SLAB), :] = xn[:, c * HEAD_DIM:(c + 1) * HEAD_DIM]
    hi = xn.astype(BF16)
    lo = (xn - hi.astype(F32)).astype(BF16)
    logits = _dot(hi, whi_ref[...]) + _dot(lo, whi_ref[...]) + _dot(hi, wlo_ref[...]) + b_ref[...]
    lane = lax.broadcasted_iota(jnp.int32, logits.shape, 1)
    big = jnp.int32(1 << 20)
    is_g = jnp.logical_and(lane >= N_EXPERTS, lane < N_EXPERTS + N_GROUPS)
    lg = jnp.where(is_g, logits, NEG)
    mg = jnp.max(lg, axis=1, keepdims=True)
    p_top = 1.0 / jnp.sum(jnp.where(is_g, jnp.exp(lg - mg), 0.0), axis=1, keepdims=True)
    g_idx = jnp.min(jnp.where(jnp.logical_and(is_g, lg == mg), lane, big), axis=1, keepdims=True) - N_EXPERTS
    in_grp = jnp.logical_and(lane < N_EXPERTS, lane // EXPERTS_PER_GROUP == g_idx)
    le = jnp.where(in_grp, logits, NEG)
    v1 = jnp.max(le, axis=1, keepdims=True)
    i1 = jnp.min(jnp.where(jnp.logical_and(in_grp, le == v1), lane, big), axis=1, keepdims=True)
    rest = jnp.logical_and(in_grp, lane != i1)
    le2 = jnp.where(rest, logits, NEG)
    v2 = jnp.max(le2, axis=1, keepdims=True)
    i2 = jnp.min(jnp.where(jnp.logical_and(rest, le2 == v2), lane, big), axis=1, keepdims=True)
    e2 = jnp.exp(v2 - v1)
    w1 = p_top / (1.0 + e2)
    w2 = p_top * e2 / (1.0 + e2)
    hit1, hit2 = lane == i1, lane == i2
    hits = jnp.where(jnp.logical_or(hit1, hit2), 1.0, 0.0)
    before = _dot(tri_ref[...], hits.astype(BF16)) + base_scr[...]
    r1 = jnp.sum(jnp.where(hit1, before, 0.0), axis=1, keepdims=True)
    r2 = jnp.sum(jnp.where(hit2, before, 0.0), axis=1, keepdims=True)
    base_scr[...] += jnp.sum(hits, axis=0, keepdims=True)
    cnt_ref[...] = base_scr[...]
    cols = (i1.astype(F32), i2.astype(F32), w1, w2, r1, r2)
    route = jnp.zeros(logits.shape, F32)
    for c, val in enumerate(cols):
        route = jnp.where(lane == c, val, route)
    route_ref[...] = route


def _router(x, g, w_hi, w_lo, bias, tm):
    m, d = x.shape
    lanes = w_hi.shape[1]
    const = lambda shape: pl.BlockSpec(shape, lambda i: (0, 0))
    tri = jnp.tril(jnp.ones((tm, tm), F32), -1).astype(BF16)
    return pl.pallas_call(
        _router_body,
        grid=(m // tm,),
        in_specs=[pl.BlockSpec((tm, d), lambda i: (i, 0)), const((1, d)), const((d, lanes)), const((d, lanes)),
                  const((1, lanes)), const((tm, tm))],
        out_specs=[pl.BlockSpec((tm * ROW_SLAB, HEAD_DIM), lambda i: (i, 0)),
                   pl.BlockSpec((tm, lanes), lambda i: (i, 0)), const((1, lanes))],
        out_shape=[jax.ShapeDtypeStruct((m * ROW_SLAB, HEAD_DIM), F32), jax.ShapeDtypeStruct((m, lanes), F32),
                   jax.ShapeDtypeStruct((1, lanes), F32)],
        scratch_shapes=[pltpu.VMEM((1, lanes), F32)],
        compiler_params=_params("arbitrary"),
        name="router",
    )(x, g.reshape(1, d), w_hi, w_lo, bias, tri)


def _gather_rows(idx_ref, src_ref, dst_ref, sem, n):
    def issue(p, carry):
        r = pl.multiple_of(idx_ref[0, p] * ROW_SLAB, ROW_SLAB)
        o = pl.multiple_of(p * ROW_SLAB, ROW_SLAB)
        pltpu.make_async_copy(src_ref.at[pl.ds(r, ROW_SLAB), :], dst_ref.at[pl.ds(o, ROW_SLAB), :], sem).start()
        return carry

    lax.fori_loop(0, n, issue, 0, unroll=8)


def _wait_rows(src_ref, dst_ref, sem):
    pltpu.make_async_copy(src_ref.at[pl.ds(0, dst_ref.shape[0]), :], dst_ref, sem).wait()


def _slab_to_rows(ref, rows, first=0):
    return jnp.concatenate([ref[pl.ds(first * ROW_SLAB + c, rows, stride=ROW_SLAB), :] for c in range(ROW_SLAB)],
                           axis=1)


def _tile_idx_specs(tile, steps):
    def cur(t, *_):
        return (t, 0, 0)

    def nxt(t, *_):
        return (jnp.minimum(t + 1, steps - 1), 0, 0)

    return [pl.BlockSpec((None, 1, tile), cur, memory_space=pltpu.SMEM),
            pl.BlockSpec((None, 1, tile), nxt, memory_space=pltpu.SMEM)]


def _experts_body(te_ref, na_ref, slot_ref, next_ref, idx_ref, idx_next_ref, xs_ref, wg_ref, wu_ref, wd_ref, o_ref,
                  xb0, xb1, sem, wgf, wuf, wdf, wsem, wgb, wub, wdb, *, te_rows):
    t = pl.program_id(0)
    na = na_ref[0]
    active = t < na
    bufs = (xb0, xb1)

    def weight_copies(e, s):
        return [pltpu.make_async_copy(src.at[e], dst.at[s], wsem.at[s])
                for src, dst in ((wg_ref, wgf), (wu_ref, wuf), (wd_ref, wdf))]

    @pl.when(jnp.logical_and(t == 0, active))
    def _():
        for cp in weight_copies(te_ref[0], 0):
            cp.start()
        _gather_rows(idx_ref, xs_ref, xb0, sem.at[0], te_rows)

    for s in range(2):
        @pl.when(jnp.logical_and(t + 1 < na, (t + 1) % 2 == s))
        def _(s=s):
            _gather_rows(idx_next_ref, xs_ref, bufs[s], sem.at[s], te_rows)

    changed = jnp.logical_or(t == 0, te_ref[t] != te_ref[jnp.maximum(t - 1, 0)])
    for s in range(2):
        @pl.when(jnp.logical_and(jnp.logical_and(active, changed), slot_ref[t] == s))
        def _(s=s):
            for cp in weight_copies(te_ref[t], s):
                cp.wait()
            wgb[...] = wgf[s].astype(BF16)
            wub[...] = wuf[s].astype(BF16)
            wdb[...] = wdf[s].astype(BF16)

            @pl.when(next_ref[t] >= 0)
            def _():
                for cp in weight_copies(next_ref[t], 1 - s):
                    cp.start()

    for s in range(2):
        @pl.when(jnp.logical_and(active, t % 2 == s))
        def _(s=s):
            _wait_rows(xs_ref, bufs[s], sem.at[s])
            x = _slab_to_rows(bufs[s], te_rows).astype(BF16)
            h = (_silu(_dot(x, wgb[...])) * _dot(x, wub[...])).astype(BF16)
            for c in range(0, ROW_SLAB, 2):
                o = _dot(h, wdb[:, c * HEAD_DIM:(c + 2) * HEAD_DIM])
                o_ref[pl.ds(c, te_rows, stride=ROW_SLAB), :] = o[:, :HEAD_DIM]
                o_ref[pl.ds(c + 1, te_rows, stride=ROW_SLAB), :] = o[:, HEAD_DIM:]

    @pl.when(jnp.logical_not(active))
    def _():
        o_ref[...] = jnp.zeros_like(o_ref)


def _experts(xs, src_tok, tile_expert, n_active, w_gate, w_up, w_down, te_rows):
    ne, d, ff = w_gate.shape
    nt = tile_expert.shape[0]
    tiles = jnp.arange(nt, dtype=jnp.int32)
    first = jnp.logical_and(jnp.concatenate([jnp.ones((1,), bool), tile_expert[1:] != tile_expert[:-1]]),
                            tiles < n_active[0])
    slot = (jnp.cumsum(first.astype(jnp.int32)) - 1) % 2
    later_first = jnp.concatenate([jnp.where(first, tiles, nt)[1:], jnp.full((1,), nt, jnp.int32)])
    next_first = lax.cummin(later_first, axis=0, reverse=True)
    next_expert = jnp.where(next_first < nt, tile_expert[jnp.minimum(next_first, nt - 1)], -1)
    any_spec = pl.BlockSpec(memory_space=pl.ANY)
    grid_spec = pltpu.PrefetchScalarGridSpec(
        num_scalar_prefetch=4,
        grid=(nt,),
        in_specs=_tile_idx_specs(te_rows, nt) + [any_spec] * 4,
        out_specs=pl.BlockSpec((te_rows * ROW_SLAB, HEAD_DIM), lambda t, *_: (t, 0)),
        scratch_shapes=[pltpu.VMEM((te_rows * ROW_SLAB, HEAD_DIM), F32),
                        pltpu.VMEM((te_rows * ROW_SLAB, HEAD_DIM), F32),
                        pltpu.SemaphoreType.DMA((2,)),
                        pltpu.VMEM((2, d, ff), F32), pltpu.VMEM((2, d, ff), F32), pltpu.VMEM((2, ff, d), F32),
                        pltpu.SemaphoreType.DMA((2,)),
                        pltpu.VMEM((d, ff), BF16), pltpu.VMEM((d, ff), BF16), pltpu.VMEM((ff, d), BF16)],
    )
    idx = src_tok.reshape(nt, 1, te_rows)
    return pl.pallas_call(
        functools.partial(_experts_body, te_rows=te_rows),
        grid_spec=grid_spec,
        out_shape=jax.ShapeDtypeStruct((nt * te_rows * ROW_SLAB, HEAD_DIM), F32),
        compiler_params=_params("arbitrary"),
        name="experts",
    )(tile_expert, n_active, slot.astype(jnp.int32), next_expert.astype(jnp.int32),
      idx, idx, xs, w_gate, w_up, w_down)


def _moe_out_body(idx_ref, idx_next_ref, x_ref, route_ref, ys_ref, gf_ref, yp_ref, ysm_ref, db0, db1, sem,
                  *, n_prompt_tiles):
    i = pl.program_id(0)
    nt = pl.num_programs(0)
    tm = x_ref.shape[0]
    bufs = (db0, db1)

    @pl.when(i == 0)
    def _():
        _gather_rows(idx_ref, ys_ref, db0, sem.at[0], 2 * tm)

    for s in range(2):
        @pl.when(jnp.logical_and(i + 1 < nt, (i + 1) % 2 == s))
        def _(s=s):
            _gather_rows(idx_next_ref, ys_ref, bufs[s], sem.at[s], 2 * tm)

    route = route_ref[...]
    w1, w2 = route[:, 2:3], route[:, 3:4]
    for s in range(2):
        @pl.when(i % 2 == s)
        def _(s=s):
            _wait_rows(ys_ref, bufs[s], sem.at[s])
            x = x_ref[...] + w1 * _slab_to_rows(bufs[s], tm) + w2 * _slab_to_rows(bufs[s], tm, first=tm)
            ms = jnp.mean(x * x, axis=-1, keepdims=True)
            y = x * lax.rsqrt(ms + EPS) * gf_ref[...]

            @pl.when(i < n_prompt_tiles)
            def _():
                yp_ref[...] = y

            @pl.when(i >= n_prompt_tiles)
            def _():
                ysm_ref[...] = y


def _moe_out(x, route, ys_sorted, pos1, pos2, g_final, n_prompt, tm):
    m, d = x.shape
    nt = m // tm
    npt = n_prompt // tm
    lanes = route.shape[1]
    idx = jnp.concatenate([pos1.reshape(nt, tm), pos2.reshape(nt, tm)], axis=1).reshape(nt, 1, 2 * tm)
    return pl.pallas_call(
        functools.partial(_moe_out_body, n_prompt_tiles=npt),
        grid=(nt,),
        in_specs=_tile_idx_specs(2 * tm, nt) + [
            pl.BlockSpec((tm, d), lambda i: (i, 0)), pl.BlockSpec((tm, lanes), lambda i: (i, 0)),
            pl.BlockSpec(memory_space=pl.ANY), pl.BlockSpec((1, d), lambda i: (0, 0))],
        out_specs=[pl.BlockSpec((tm, d), lambda i: (jnp.minimum(i, npt - 1), 0)),
                   pl.BlockSpec((tm, d), lambda i: (jnp.maximum(i - npt, 0), 0))],
        out_shape=[jax.ShapeDtypeStruct((n_prompt, d), F32), jax.ShapeDtypeStruct((m - n_prompt, d), F32)],
        scratch_shapes=[pltpu.VMEM((2 * tm * ROW_SLAB, HEAD_DIM), F32),
                        pltpu.VMEM((2 * tm * ROW_SLAB, HEAD_DIM), F32),
                        pltpu.SemaphoreType.DMA((2,))],
        compiler_params=_params("arbitrary"),
        name="moe_out",
    )(idx, idx, x, route, ys_sorted, g_final.reshape(1, d))


def _moe(x, g_ffn, w_rg, b_rg, w_re, b_re, w_gate, w_up, w_down, g_final, n_prompt):
    m, d = x.shape
    lanes = HEAD_DIM
    npad = lanes - N_EXPERTS - N_GROUPS
    w_r = jnp.concatenate([w_re, w_rg, jnp.zeros((d, npad), F32)], axis=1)
    b_r = jnp.concatenate([b_re, b_rg, jnp.zeros((npad,), F32)]).reshape(1, lanes)
    w_r_hi = w_r.astype(BF16)
    w_r_lo = (w_r - w_r_hi.astype(F32)).astype(BF16)
    xs, route, counts = _router(x, g_ffn, w_r_hi, w_r_lo, b_r, 512)

    te = MOE_TILE
    n_tiles = -(-(2 * m + N_EXPERTS * (te - 1)) // te)
    n_tiles += n_tiles % 2
    counts = counts[0, :N_EXPERTS].astype(jnp.int32)
    padded = (counts + te - 1) // te * te
    pad_end = jnp.cumsum(padded)
    pad_off = pad_end - padded
    experts = jnp.arange(N_EXPERTS, dtype=jnp.int32)

    def dest(col_e, col_r):
        e = route[:, col_e].astype(jnp.int32)
        off = jnp.sum(jnp.where(e[:, None] == experts[None, :], pad_off[None, :], 0), axis=1)
        return off + route[:, col_r].astype(jnp.int32)

    pos1, pos2 = dest(0, 4), dest(1, 5)
    tok = jnp.tile(jnp.arange(m, dtype=jnp.int32), 2)
    src_tok = jnp.zeros((n_tiles * te,), jnp.int32).at[jnp.concatenate([pos1, pos2])].set(tok)
    tile_start = jnp.arange(n_tiles, dtype=jnp.int32) * te
    tile_expert = jnp.minimum(jnp.sum(tile_start[:, None] >= pad_end[None, :], axis=1), N_EXPERTS - 1)
    n_active = (pad_end[-1] // te).reshape(1)

    ys_sorted = _experts(xs, src_tok, tile_expert.astype(jnp.int32), n_active.astype(jnp.int32),
                         w_gate, w_up, w_down, te)
    return _moe_out(x, route, ys_sorted, pos1, pos2, g_final, n_prompt, 256)


def _rope_tables(positions):
    half = HEAD_DIM // 2
    inv_freq = ROPE_THETA ** (-jnp.arange(half, dtype=F32) / half)
    ang = positions.astype(F32)[:, None] * inv_freq[None, :]
    cos, sin = jnp.cos(ang), jnp.sin(ang)
    return jnp.concatenate([cos, cos], axis=1), jnp.concatenate([-sin, sin], axis=1)


def kernel(x_prompt, x_sample, cache_swa1, cache_swa2, cache_swa3, state_hgrn, cache_mem_kv, mem_prompt,
           hgrn_lb_logits, norm_mix, w_in, w_proj_attn, w_proj_hgrn, w_out, hgrn_norm, norm_cross, norm_mem,
           w_cq, w_ckv, w_co, norm_ffn, w_rg, b_rg, w_re, b_re, w_e_gate, w_e_up, w_e_down, norm_final):
    bp, seq, d = x_prompt.shape
    bs, dseq, _ = x_sample.shape
    depth = w_in.shape[0]
    assert depth == 1, "single-layer trunk"
    past = cache_swa3.shape[2]
    mem_len = mem_prompt.shape[1]
    np_, ns = bp * seq, bs * dseq
    hw = HGRN_HEADS * HEAD_DIM
    col_hgrn = 3 * ATT_WIDTH
    col_ga = col_hgrn + 4 * hw
    col_gh = col_ga + d
    l = 0

    x_parts = (x_prompt.reshape(np_, d), x_sample.reshape(ns, d))
    pos = jnp.concatenate([jnp.tile(jnp.arange(seq, dtype=jnp.int32), bp),
                           jnp.tile(past + jnp.arange(dseq, dtype=jnp.int32), bs)])
    cos2, sin2 = _rope_tables(pos)

    xn = _rmsnorm2(*x_parts, norm_mix[l], BF16, 512)
    proj = _matmul(xn, w_in[l], F32, 512, ATT_WIDTH, rope=(cos2, sin2), name="proj_in")

    attn = _attn_prompt(proj, bp, seq, 256)
    attn = _attn_sample(proj, (cache_swa1[l], cache_swa2[l], cache_swa3[l]), attn, np_, bs, dseq)

    hgrn, st_p = _hgrn(proj, col_hgrn, hgrn_lb_logits, hgrn_norm[l], None, None, 0, bp, seq, 128, 4)
    hgrn, st_s = _hgrn(proj, col_hgrn, hgrn_lb_logits, hgrn_norm[l], state_hgrn[l], hgrn, np_, bs, dseq, dseq, 4)

    merged = _gated_merge(attn, hgrn, w_proj_attn[l], w_proj_hgrn[l], proj, col_ga, col_gh, 512, 1024, 512)
    x1 = _matmul(merged, w_out[l], F32, 512, 1024, res=x_parts, name="proj_out")

    mem_n = _rmsnorm(mem_prompt.reshape(bp * mem_len, d), norm_mem[l], BF16, 256)
    mkv_p = _matmul(mem_n, w_ckv[l], F32, 256, KV_ROW, name="mem_kv")
    qc = _matmul(x1, w_cq[l], F32, 512, ATT_OUT, norm=norm_cross[l], name="cross_q")
    oc = _cross_attn(qc, mkv_p.reshape(bp, mem_len, KV_ROW), None, 0, bp, seq, 512, 1)
    oc = _cross_attn(qc, cache_mem_kv[l].reshape(bs, mem_len, KV_SLAB, HEAD_DIM), oc, np_, bs, dseq, dseq, 4)
    x2 = _matmul(oc, w_co[l], F32, 512, 1024, res=x1, name="cross_out")

    ff = w_e_gate.shape[-1]
    y_p, y_s = _moe(x2, norm_ffn[l], w_rg[l], b_rg[l], w_re[l], b_re[l],
                    w_e_gate[l].reshape(N_EXPERTS, d, ff), w_e_up[l].reshape(N_EXPERTS, d, ff),
                    w_e_down[l].reshape(N_EXPERTS, ff, d), norm_final, np_)

    swa_p = [_kv_rows(proj, g, 0, bp, seq, min(w, seq), min(w, seq, 256))[None]
             for g, (w, _) in enumerate(ATT_GROUPS)]
    swa_s = [_kv_rows(proj, g, np_, 1, ns, ns, 256).reshape(1, bs, dseq, 2, ATT_HEADS, HEAD_DIM)
             for g in range(len(ATT_GROUPS))]
    return (y_p.reshape(bp, seq, d), y_s.reshape(bs, dseq, d),
            swa_p[0], swa_p[1], swa_p[2], st_p[None],
            mkv_p.reshape(1, bp, mem_len, 2, ATT_HEADS, HEAD_DIM),
            swa_s[0], swa_s[1], swa_s[2], st_s[None])
```

```python
import functools

import jax
import jax.numpy as jnp
from jax import lax
from jax.experimental import pallas as pl
from jax.experimental.pallas import tpu as pltpu

F32 = jnp.float32
BF16 = jnp.bfloat16

ATT_GROUPS = ((128, 1), (512, 4), (2048, 16))
ATT_HEADS = 4
HEAD_DIM = 128
ATT_OUT = ATT_HEADS * HEAD_DIM
ATT_WIDTH = len(ATT_GROUPS) * ATT_OUT
KV_ROW = 2 * ATT_OUT
KV_SLAB = 2 * ATT_HEADS
HGRN_HEADS = 16
ROPE_THETA = 10000.0
EPS = 1e-6
N_EXPERTS = 32
N_GROUPS = 4
EXPERTS_PER_GROUP = 8
NEG = -1e30
ATT_BLOCK = 128
HGRN_SUB = 8
ROW_SLAB = 16
MOE_TILE = 128
VMEM_LIMIT = 56 * 1024 * 1024


def _params(*sem):
    return pltpu.CompilerParams(dimension_semantics=sem, vmem_limit_bytes=VMEM_LIMIT)


def _dot(a, b):
    return jnp.dot(a, b, preferred_element_type=F32)


def _dot_nt(a, b):
    return lax.dot_general(a, b, (((1,), (1,)), ((), ())), preferred_element_type=F32)


def _sigmoid(x):
    return 1.0 / (1.0 + jnp.exp(-x))


def _silu(x):
    return x * _sigmoid(x)


def _rmsnorm_body(x_ref, g_ref, o_ref):
    x = x_ref[...]
    ms = jnp.mean(x * x, axis=-1, keepdims=True)
    o_ref[...] = (x * lax.rsqrt(ms + EPS) * g_ref[...]).astype(o_ref.dtype)


def _rmsnorm(x, g, out_dtype, tm):
    m, d = x.shape
    return pl.pallas_call(
        _rmsnorm_body,
        grid=(m // tm,),
        in_specs=[pl.BlockSpec((tm, d), lambda i: (i, 0)), pl.BlockSpec((1, d), lambda i: (0, 0))],
        out_specs=pl.BlockSpec((tm, d), lambda i: (i, 0)),
        out_shape=jax.ShapeDtypeStruct((m, d), out_dtype),
        compiler_params=_params("arbitrary"),
        name="rmsnorm",
    )(x, g.reshape(1, d))


def _rmsnorm2_body(xa_ref, xb_ref, g_ref, o_ref, *, na_tiles):
    i = pl.program_id(0)
    for ref, cond in ((xa_ref, i < na_tiles), (xb_ref, i >= na_tiles)):
        @pl.when(cond)
        def _(ref=ref):
            x = ref[...]
            ms = jnp.mean(x * x, axis=-1, keepdims=True)
            o_ref[...] = (x * lax.rsqrt(ms + EPS) * g_ref[...]).astype(o_ref.dtype)


def _two_part_specs(tm, width, na_tiles, index=lambda i: i, col=lambda i: 0):
    return [pl.BlockSpec((tm, width), lambda *a: (jnp.minimum(index(*a), na_tiles - 1), col(*a))),
            pl.BlockSpec((tm, width), lambda *a: (jnp.maximum(index(*a) - na_tiles, 0), col(*a)))]


def _rmsnorm2(xa, xb, g, out_dtype, tm):
    (ma, d), mb = xa.shape, xb.shape[0]
    na_tiles = ma // tm
    return pl.pallas_call(
        functools.partial(_rmsnorm2_body, na_tiles=na_tiles),
        grid=((ma + mb) // tm,),
        in_specs=_two_part_specs(tm, d, na_tiles) + [pl.BlockSpec((1, d), lambda i: (0, 0))],
        out_specs=pl.BlockSpec((tm, d), lambda i: (i, 0)),
        out_shape=jax.ShapeDtypeStruct((ma + mb, d), out_dtype),
        compiler_params=_params("arbitrary"),
        name="rmsnorm2",
    )(xa, xb, g.reshape(1, d))


def _mm_body(*refs, has_norm, has_res, has_rope, res_a_tiles):
    refs = list(refs)
    a_ref, w_ref = refs[:2]
    o_ref, wb_ref = refs[-2:]
    extra = refs[2:-2]
    j = pl.program_id(0)

    @pl.when(pl.program_id(1) == 0)
    def _():
        wb_ref[...] = w_ref[...].astype(BF16)

    a = a_ref[...]
    if has_norm:
        g_ref = extra.pop(0)
        ms = jnp.mean(a * a, axis=-1, keepdims=True)
        a = a * lax.rsqrt(ms + EPS) * g_ref[...]
    a = a.astype(BF16)
    if not has_rope:
        acc = _dot(a, wb_ref[...])
        if has_res and res_a_tiles is None:
            acc = acc + extra[0][...]
        if has_res and res_a_tiles is not None:
            i = pl.program_id(1)
            for r_ref, cond in ((extra[0], i < res_a_tiles), (extra[1], i >= res_a_tiles)):
                @pl.when(cond)
                def _(r_ref=r_ref):
                    o_ref[...] = (acc + r_ref[...]).astype(o_ref.dtype)
            return
        o_ref[...] = acc.astype(o_ref.dtype)
        return

    cos_ref, sin_ref = extra[-2:]

    @pl.when(j >= 2)
    def _():
        o_ref[...] = _dot(a, wb_ref[...]).astype(o_ref.dtype)

    @pl.when(j < 2)
    def _():
        cos = cos_ref[...]
        sin = sin_ref[...]
        scale = jnp.where(j == 0, HEAD_DIM ** -0.5, 1.0)
        for p in range(0, ATT_WIDTH, 2 * HEAD_DIM):
            acc = _dot(a, wb_ref[:, p:p + 2 * HEAD_DIM])
            for u in range(2):
                x = acc[:, u * HEAD_DIM:(u + 1) * HEAD_DIM]
                sl = slice(p + u * HEAD_DIM, p + (u + 1) * HEAD_DIM)
                o_ref[:, sl] = ((x * cos + pltpu.roll(x, HEAD_DIM // 2, axis=1) * sin) * scale).astype(o_ref.dtype)


def _matmul(a, w, out_dtype, tm, tn, norm=None, res=None, rope=None, name="matmul"):
    m, k = a.shape
    n = w.shape[1]
    in_specs = [pl.BlockSpec((tm, k), lambda j, i: (i, 0)), pl.BlockSpec((k, tn), lambda j, i: (0, j))]
    args = [a, w]
    if norm is not None:
        in_specs.append(pl.BlockSpec((1, k), lambda j, i: (0, 0)))
        args.append(norm.reshape(1, k))
    res_a_tiles = None
    if isinstance(res, tuple):
        res_a_tiles = res[0].shape[0] // tm
        in_specs.extend(_two_part_specs(tm, tn, res_a_tiles, index=lambda j, i: i, col=lambda j, i: j))
        args.extend(res)
    elif res is not None:
        in_specs.append(pl.BlockSpec((tm, tn), lambda j, i: (i, j)))
        args.append(res)
    if rope is not None:
        assert tn == ATT_WIDTH
        in_specs.extend([pl.BlockSpec((tm, HEAD_DIM), lambda j, i: (i, 0))] * 2)
        args.extend(rope)
    return pl.pallas_call(
        functools.partial(_mm_body, has_norm=norm is not None, has_res=res is not None,
                          has_rope=rope is not None, res_a_tiles=res_a_tiles),
        grid=(n // tn, m // tm),
        in_specs=in_specs,
        out_specs=pl.BlockSpec((tm, tn), lambda j, i: (i, j)),
        out_shape=jax.ShapeDtypeStruct((m, n), out_dtype),
        scratch_shapes=[pltpu.VMEM((k, tn), BF16)],
        compiler_params=_params("arbitrary", "arbitrary"),
        name=name,
    )(*args)


def _prompt_key_blocks(bq):
    table = []
    for g, (w, _) in enumerate(ATT_GROUPS):
        nback = -(-w // bq)
        table.extend((g, back) for back in range(nback, -1, -1))
    return tuple(table)


def _attn_prompt_body(q_ref, k_ref, v_ref, o_ref, m_scr, l_scr, acc_scr, *, bq, table):
    i = pl.program_id(1)
    j = pl.program_id(2)
    ng = len(ATT_GROUPS)

    @pl.when(j == 0)
    def _():
        m_scr[...] = jnp.full(m_scr.shape, NEG, F32)
        l_scr[...] = jnp.zeros(l_scr.shape, F32)
        acc_scr[...] = jnp.zeros(acc_scr.shape, F32)

    row = lax.broadcasted_iota(jnp.int32, (bq, bq), 0)
    col = lax.broadcasted_iota(jnp.int32, (bq, bq), 1)
    for g, (w, d) in enumerate(ATT_GROUPS):
        j0 = min(jj for jj, (gg, _) in enumerate(table) if gg == g)
        nback = max(back for gg, back in table if gg == g)
        back = nback - (j - j0)

        @pl.when(jnp.logical_and(jnp.logical_and(j >= j0, j <= j0 + nback), i >= back))
        def _(g=g, w=w, d=d, back=back):
            dist = back * bq + row - col
            valid = jnp.logical_and(jnp.logical_and(dist >= 0, dist <= w), (dist & (d - 1)) == 0)
            bias = jnp.where(valid, 0.0, NEG)
            heads = range(ATT_HEADS)
            hs = [slice(h * HEAD_DIM, (h + 1) * HEAD_DIM) for h in heads]
            m_old = [m_scr[g * ATT_HEADS + h] for h in heads]
            l_old = [l_scr[g * ATT_HEADS + h] for h in heads]
            a_old = [acc_scr[g, :, hs[h]] for h in heads]
            qs = [q_ref[:, g * ATT_OUT + h * HEAD_DIM: g * ATT_OUT + (h + 1) * HEAD_DIM].astype(BF16) for h in heads]
            ss = [_dot_nt(qs[h], k_ref[:, hs[h]].astype(BF16)) + bias for h in heads]
            m_new = [jnp.maximum(m_old[h], jnp.max(ss[h], axis=1, keepdims=True)) for h in heads]
            ps = [jnp.exp(ss[h] - jnp.concatenate([m_new[h]] * (bq // HEAD_DIM), axis=1)) for h in heads]
            alpha = [jnp.exp(m_old[h] - m_new[h]) for h in heads]
            pv = [_dot(ps[h].astype(BF16), v_ref[:, hs[h]].astype(BF16)) for h in heads]
            for h in heads:
                m_scr[g * ATT_HEADS + h] = m_new[h]
                l_scr[g * ATT_HEADS + h] = alpha[h] * l_old[h] + jnp.sum(ps[h], axis=1, keepdims=True)
                acc_scr[g, :, hs[h]] = alpha[h] * a_old[h] + pv[h]

    @pl.when(j == pl.num_programs(2) - 1)
    def _():
        for h in range(ATT_HEADS):
            sl = slice(h * HEAD_DIM, (h + 1) * HEAD_DIM)
            lse = [m_scr[g * ATT_HEADS + h] + jnp.log(l_scr[g * ATT_HEADS + h]) for g in range(ng)]
            mx = functools.reduce(jnp.maximum, lse)
            ws = [jnp.exp(x - mx) for x in lse]
            num = sum(ws[g] / l_scr[g * ATT_HEADS + h] * acc_scr[g, :, sl] for g in range(ng))
            o_ref[:, sl] = num / sum(ws)


def _attn_prompt(proj, batch, seq, bq):
    n = proj.shape[0]
    ng = len(ATT_GROUPS)
    nq = seq // bq
    table = _prompt_key_blocks(bq)
    groups = jnp.asarray([g for g, _ in table], jnp.int32)
    backs = jnp.asarray([b for _, b in table], jnp.int32)

    def kv_map(part):
        def index(b, i, j, g_ref, back_ref):
            return (b * nq + jnp.maximum(i - back_ref[j], 0), part * ng + g_ref[j])
        return index

    grid_spec = pltpu.PrefetchScalarGridSpec(
        num_scalar_prefetch=2,
        grid=(batch, nq, len(table)),
        in_specs=[pl.BlockSpec((bq, ATT_WIDTH), lambda b, i, j, g_ref, back_ref: (b * nq + i, 0)),
                  pl.BlockSpec((bq, ATT_OUT), kv_map(1)),
                  pl.BlockSpec((bq, ATT_OUT), kv_map(2))],
        out_specs=pl.BlockSpec((bq, ATT_OUT), lambda b, i, j, g_ref, back_ref: (b * nq + i, 0)),
        scratch_shapes=[pltpu.VMEM((ng * ATT_HEADS, bq, HEAD_DIM), F32),
                        pltpu.VMEM((ng * ATT_HEADS, bq, HEAD_DIM), F32),
                        pltpu.VMEM((ng, bq, ATT_OUT), F32)],
    )

    def body(g_ref, back_ref, *refs):
        del g_ref, back_ref
        _attn_prompt_body(*refs, bq=bq, table=table)

    return pl.pallas_call(
        body,
        grid_spec=grid_spec,
        out_shape=jax.ShapeDtypeStruct((n, ATT_OUT), F32),
        compiler_params=_params("arbitrary", "arbitrary", "arbitrary"),
        name="attn_prompt",
    )(groups, backs, proj, proj, proj)


def _head_lanes(ref2d, first, rows):
    return jnp.concatenate([ref2d[pl.ds(first + h, rows, stride=KV_SLAB), :] for h in range(ATT_HEADS)], axis=1)


def _attn_sample_body(q_ref, kn_ref, vn_ref, *rest, tq):
    cache_refs, o_ref = rest[:-2], rest[-1]
    nrow = ATT_HEADS * tq
    nkey = ATT_BLOCK
    rid = lax.broadcasted_iota(jnp.int32, (nrow, 1), 0)
    i_row = rid % tq
    head_row = rid // tq
    lane_head = lax.broadcasted_iota(jnp.int32, (1, ATT_OUT), 1) // HEAD_DIM
    head_mask = head_row == lane_head
    key = lax.broadcasted_iota(jnp.int32, (1, nkey), 1)
    pad = jnp.zeros((nkey - tq, ATT_OUT), F32)
    outs, lses = [], []
    ci = 0
    for g, (_, d) in enumerate(ATT_GROUPS):
        qg = q_ref[:, g * ATT_OUT:(g + 1) * ATT_OUT]
        qrows = jnp.where(head_mask, jnp.concatenate([qg] * ATT_HEADS, axis=0), 0.0).astype(BF16)
        k_new = jnp.concatenate([kn_ref[:, g * ATT_OUT:(g + 1) * ATT_OUT], pad], axis=0)
        v_new = jnp.concatenate([vn_ref[:, g * ATT_OUT:(g + 1) * ATT_OUT], pad], axis=0)
        valid_new = jnp.logical_and(jnp.logical_and(key < tq, key <= i_row), (key % d) == (i_row % d))
        blocks = [(k_new, v_new, valid_new)]
        for r in range(min(d, tq)):
            c2 = cache_refs[ci].reshape(nkey * KV_SLAB, HEAD_DIM)
            ci += 1
            valid = jnp.logical_and((i_row % d) == r, key >= i_row // d)
            blocks.append((_head_lanes(c2, 0, nkey), _head_lanes(c2, ATT_HEADS, nkey), valid))
        ss = [jnp.where(valid, _dot_nt(qrows, kb.astype(BF16)), NEG) for kb, _, valid in blocks]
        m = functools.reduce(jnp.maximum, [jnp.max(s, axis=1, keepdims=True) for s in ss])
        ps = [jnp.exp(s - m) for s in ss]
        l = sum(jnp.sum(p, axis=1, keepdims=True) for p in ps)
        acc = sum(_dot(p.astype(BF16), vb.astype(BF16)) for p, (_, vb, _) in zip(ps, blocks))
        o = acc / l
        lse = m + jnp.log(l)
        outs.append(jnp.concatenate(
            [o[h * tq:(h + 1) * tq, h * HEAD_DIM:(h + 1) * HEAD_DIM] for h in range(ATT_HEADS)], axis=1))
        lses.append(jnp.concatenate(
            [jnp.broadcast_to(lse[h * tq:(h + 1) * tq], (tq, HEAD_DIM)) for h in range(ATT_HEADS)], axis=1))
    mx = jnp.maximum(jnp.maximum(lses[0], lses[1]), lses[2])
    ws = [jnp.exp(x - mx) for x in lses]
    o_ref[...] = (ws[0] * outs[0] + ws[1] * outs[1] + ws[2] * outs[2]) / (ws[0] + ws[1] + ws[2])


def _attn_sample(proj, caches, attn_buf, row0, batch, tq):
    blk0 = row0 // tq
    views, specs = [], []
    for (w, d), c in zip(ATT_GROUPS, caches):
        assert c.shape[1] == w and w // d == ATT_BLOCK, "window buffers must hold exactly one window"
        view = c.reshape(batch, w // d, d, KV_SLAB, HEAD_DIM)
        for r in range(min(d, tq)):
            views.append(view)
            specs.append(pl.BlockSpec((None, w // d, None, KV_SLAB, HEAD_DIM), lambda b, r=r: (b, 0, r, 0, 0)))
    n_in = 3 + len(views)
    qkv = [pl.BlockSpec((tq, ATT_WIDTH), lambda b, part=part: (blk0 + b, part)) for part in range(3)]
    return pl.pallas_call(
        functools.partial(_attn_sample_body, tq=tq),
        grid=(batch,),
        in_specs=qkv + specs + [pl.BlockSpec(memory_space=pl.ANY)],
        out_specs=pl.BlockSpec((tq, ATT_OUT), lambda b: (blk0 + b, 0)),
        out_shape=jax.ShapeDtypeStruct(attn_buf.shape, F32),
        input_output_aliases={n_in: 0},
        compiler_params=_params("arbitrary"),
        name="attn_sample",
    )(proj, proj, proj, *views, attn_buf)


def _kv_rows_body(k_ref, v_ref, o_ref):
    tm = k_ref.shape[0]
    for h in range(ATT_HEADS):
        sl = slice(h * HEAD_DIM, (h + 1) * HEAD_DIM)
        o_ref[pl.ds(h, tm, stride=KV_SLAB), :] = k_ref[:, sl]
        o_ref[pl.ds(ATT_HEADS + h, tm, stride=KV_SLAB), :] = v_ref[:, sl]


def _kv_rows(proj, g, row0, batch, seq, keep, tm):
    ng = len(ATT_GROUPS)
    nt = keep // tm
    rb0 = (row0 + seq - keep) // tm
    per_seq = seq // tm
    out = pl.pallas_call(
        _kv_rows_body,
        grid=(batch, nt),
        in_specs=[pl.BlockSpec((tm, ATT_OUT), lambda b, t: (rb0 + b * per_seq + t, ng + g)),
                  pl.BlockSpec((tm, ATT_OUT), lambda b, t: (rb0 + b * per_seq + t, 2 * ng + g))],
        out_specs=pl.BlockSpec((tm * KV_SLAB, HEAD_DIM), lambda b, t: (b * nt + t, 0)),
        out_shape=jax.ShapeDtypeStruct((batch * keep * KV_SLAB, HEAD_DIM), F32),
        compiler_params=_params("arbitrary", "arbitrary"),
        name=f"kv_rows_g{g}_{tm}",
    )(proj, proj)
    return out.reshape(batch, keep, 2, ATT_HEADS, HEAD_DIM)


def _cumsum_rows(x, seg):
    row = lax.broadcasted_iota(jnp.int32, (x.shape[0], 1), 0) % seg
    sh = 1
    while sh < seg:
        x = x + jnp.where(row >= sh, pltpu.roll(x, sh, axis=0), 0.0)
        sh *= 2
    return x


def _bcast_rows(b, first, period):
    c, w = b.shape
    parts = [jnp.broadcast_to(b[p * period + first: p * period + first + 1, :], (period, w))
             for p in range(c // period)]
    return parts[0] if len(parts) == 1 else jnp.concatenate(parts, axis=0)


def _hgrn_body(*refs, chunk, nseq, has_s0, chained):
    refs = list(refs)
    qh_ref, fh_ref, ih_ref, og_ref, lbl_ref, gn_ref = refs[:6]
    s0_ref = refs[6] if has_s0 else None
    o_ref, so_ref, st_ref = refs[-3:]
    c = chunk
    nh = ATT_HEADS
    t = pl.program_id(2)

    if chained:
        @pl.when(t == 0)
        def _():
            for h in range(nh):
                st_ref[h] = s0_ref[0, h].T if has_s0 else jnp.zeros((HEAD_DIM, HEAD_DIM), F32)

    lbl = lbl_ref[...]
    e = jnp.exp(lbl - jnp.max(lbl, axis=0, keepdims=True))
    lb = e[0:1, :] / jnp.sum(e, axis=0, keepdims=True)
    f = lb + (1.0 - lb) * _sigmoid(fh_ref[...])
    kk = 1.0 - f
    qf = _silu(qh_ref[...])
    b = _cumsum_rows(jnp.log(f), c)
    vv = ih_ref[...]

    row = lax.broadcasted_iota(jnp.int32, (c, c), 0)
    col = lax.broadcasted_iota(jnp.int32, (c, c), 1)
    rid = lax.broadcasted_iota(jnp.int32, (nseq * c, 1), 0)
    seqs = [slice(q * c, (q + 1) * c) for q in range(nseq)]
    heads = [slice(h * HEAD_DIM, (h + 1) * HEAD_DIM) for h in range(nh)]

    a = [[jnp.zeros((c, c), F32) for _ in range(nh)] for _ in range(nseq)]
    s = c // 2
    while s >= HGRN_SUB:
        ref = _bcast_rows(b, s - 1, 2 * s)
        second = (rid % (2 * s)) >= s
        dlt = b - ref
        ee = jnp.exp(jnp.where(second, dlt, -dlt))
        ql = jnp.where(second, qf * ee, 0.0).astype(BF16)
        kl = jnp.where(second, 0.0, kk * ee).astype(BF16)
        same = (row // (2 * s)) == (col // (2 * s))
        for q in range(nseq):
            for h in range(nh):
                a[q][h] = a[q][h] + jnp.where(same, _dot_nt(ql[seqs[q], heads[h]], kl[seqs[q], heads[h]]), 0.0)
        s //= 2
    sub = min(HGRN_SUB, c)
    ref = _bcast_rows(b, 0, sub)
    dlt = b - ref
    qd = (qf * jnp.exp(dlt)).astype(BF16)
    kd = (kk * jnp.exp(jnp.minimum(-dlt, 80.0))).astype(BF16)
    diag = jnp.logical_and((row // sub) == (col // sub), col <= row)
    qe = (qf * jnp.exp(b)).astype(BF16)
    b_last = _bcast_rows(b, c - 1, c)
    kend = (kk * jnp.exp(b_last - b)).astype(BF16)
    dec = jnp.exp(b_last)
    gn = gn_ref[...]
    gate = _silu(og_ref[...])
    vb = vv.astype(BF16)
    eye = (lax.broadcasted_iota(jnp.int32, (HEAD_DIM, HEAD_DIM), 0)
           == lax.broadcasted_iota(jnp.int32, (HEAD_DIM, HEAD_DIM), 1))
    tn_dims = (((0,), (0,)), ((), ()))
    for h in range(nh):
        sl = heads[h]
        st = st_ref[h] if chained else None
        for q in range(nseq):
            rs = seqs[q]
            ah = a[q][h] + jnp.where(diag, _dot_nt(qd[rs, sl], kd[rs, sl]), 0.0)
            intra = _dot(ah.astype(BF16), vb[rs, sl])
            dec_q = dec[q * c:q * c + 1, sl]
            if chained:
                o = _dot_nt(qe[rs, sl], st.astype(BF16)) + intra
                st = st * dec_q + lax.dot_general(vb[rs, sl], kend[rs, sl], tn_dims, preferred_element_type=F32)
            else:
                s_kv = s0_ref[q, h]
                o = _dot(qe[rs, sl], s_kv.astype(BF16)) + intra
                dec_col = jnp.sum(jnp.where(eye, jnp.broadcast_to(dec_q, (HEAD_DIM, HEAD_DIM)), 0.0),
                                  axis=1, keepdims=True)
                so_ref[q, h] = s_kv * dec_col + lax.dot_general(kend[rs, sl], vb[rs, sl], tn_dims,
                                                                preferred_element_type=F32)
            ms = jnp.mean(o * o, axis=-1, keepdims=True)
            o_ref[rs, sl] = (o * lax.rsqrt(ms + EPS) * gn * gate[rs, sl]).astype(o_ref.dtype)
        if chained:
            st_ref[h] = st

            @pl.when(t == pl.num_programs(2) - 1)
            def _(h=h, st=st):
                so_ref[0, h] = st.T


def _hgrn(proj, col0, lb_logits, gnorm, s0, out_buf, row0, batch, seq, chunk, nseq):
    n = proj.shape[0]
    chained = seq != chunk
    assert chained or s0 is not None
    nt = seq // (chunk * nseq) if chained else 1
    nstate = 1 if chained else nseq
    hb = HGRN_HEADS // ATT_HEADS
    width = HGRN_HEADS * HEAD_DIM
    cb0 = col0 // ATT_OUT
    rows = nseq * chunk
    rb0 = row0 // rows

    def col(j):
        return pl.BlockSpec((rows, ATT_OUT), lambda b, h, t, j=j: (rb0 + b * nt + t, cb0 + j * hb + h))

    in_specs = [col(0), col(1), col(2), col(3),
                pl.BlockSpec((lb_logits.shape[0], ATT_OUT), lambda b, h, t: (0, h)),
                pl.BlockSpec((1, HEAD_DIM), lambda b, h, t: (0, 0))]
    args = [proj, proj, proj, proj, lb_logits, gnorm.reshape(1, HEAD_DIM)]
    state_spec = pl.BlockSpec((nstate, ATT_HEADS, HEAD_DIM, HEAD_DIM), lambda b, h, t: (b, h, 0, 0))
    if s0 is not None:
        in_specs.append(state_spec)
        args.append(s0)
    aliases = {}
    if out_buf is not None:
        aliases = {len(args): 0}
        in_specs.append(pl.BlockSpec(memory_space=pl.ANY))
        args.append(out_buf)
    return pl.pallas_call(
        functools.partial(_hgrn_body, chunk=chunk, nseq=nseq, has_s0=s0 is not None, chained=chained),
        grid=(batch // nstate, hb, nt),
        in_specs=in_specs,
        out_specs=[pl.BlockSpec((rows, ATT_OUT), lambda b, h, t: (rb0 + b * nt + t, h)), state_spec],
        out_shape=[jax.ShapeDtypeStruct((n, width), BF16),
                   jax.ShapeDtypeStruct((batch, HGRN_HEADS, HEAD_DIM, HEAD_DIM), F32)],
        scratch_shapes=[pltpu.VMEM((ATT_HEADS, HEAD_DIM, HEAD_DIM), F32)],
        input_output_aliases=aliases,
        compiler_params=_params("arbitrary", "arbitrary", "arbitrary"),
        name=f"hgrn_c{chunk}",
    )(*args)


def _gated_merge_body(att_ref, hg_ref, wa_ref, wh_ref, *rest, ngate):
    ga_refs, gh_refs = rest[:ngate], rest[ngate:2 * ngate]
    o_ref, wab_ref, whb_ref = rest[2 * ngate:]

    @pl.when(pl.program_id(1) == 0)
    def _():
        wab_ref[...] = wa_ref[...].astype(BF16)
        whb_ref[...] = wh_ref[...].astype(BF16)

    pa = _dot(att_ref[...].astype(BF16), wab_ref[...])
    ph = _dot(hg_ref[...].astype(BF16), whb_ref[...])
    ga = jnp.concatenate([r[...] for r in ga_refs], axis=1)
    gh = jnp.concatenate([r[...] for r in gh_refs], axis=1)
    o_ref[...] = (_sigmoid(ga) * pa + _sigmoid(gh) * ph).astype(o_ref.dtype)


def _gated_merge(attn, hgrn, w_pa, w_ph, proj, col_ga, col_gh, tm, tn, tg):
    m = attn.shape[0]
    n = w_pa.shape[1]
    ka, kh = w_pa.shape[0], w_ph.shape[0]
    assert col_ga % tg == 0 and col_gh % tg == 0 and tn % tg == 0, "gate columns must start on a gate block"
    ngate = tn // tg
    gate = lambda c0, u: pl.BlockSpec((tm, tg), lambda j, i: (i, c0 // tg + j * ngate + u))
    gates = [gate(col_ga, u) for u in range(ngate)] + [gate(col_gh, u) for u in range(ngate)]
    return pl.pallas_call(
        functools.partial(_gated_merge_body, ngate=ngate),
        grid=(n // tn, m // tm),
        in_specs=[pl.BlockSpec((tm, ka), lambda j, i: (i, 0)),
                  pl.BlockSpec((tm, kh), lambda j, i: (i, 0)),
                  pl.BlockSpec((ka, tn), lambda j, i: (0, j)),
                  pl.BlockSpec((kh, tn), lambda j, i: (0, j))] + gates,
        out_specs=pl.BlockSpec((tm, tn), lambda j, i: (i, j)),
        out_shape=jax.ShapeDtypeStruct((m, n), BF16),
        scratch_shapes=[pltpu.VMEM((ka, tn), BF16), pltpu.VMEM((kh, tn), BF16)],
        compiler_params=_params("arbitrary", "arbitrary"),
        name="gated_merge",
    )(attn, hgrn, w_pa, w_ph, *([proj] * (2 * ngate)))


def _cross_body(q_ref, kv_ref, *rest, slab_rows, nseq):
    o_ref = rest[-1]
    scale = HEAD_DIM ** -0.5
    tq = q_ref.shape[0] // nseq
    mem_len = kv_ref.shape[1]
    if slab_rows:
        nrow = ATT_HEADS * tq
        head_row = lax.broadcasted_iota(jnp.int32, (nrow, 1), 0) // tq
        lane_head = lax.broadcasted_iota(jnp.int32, (1, ATT_OUT), 1) // HEAD_DIM
        head_mask = head_row == lane_head
        for b in range(nseq):
            rs = slice(b * tq, (b + 1) * tq)
            kv = kv_ref.at[b].reshape(mem_len * KV_SLAB, HEAD_DIM)
            qb = q_ref[rs, :] * scale
            qrows = jnp.where(head_mask, jnp.concatenate([qb] * ATT_HEADS, axis=0), 0.0).astype(BF16)
            s = _dot_nt(qrows, _head_lanes(kv, 0, mem_len).astype(BF16))
            p = jnp.exp(s - jnp.max(s, axis=1, keepdims=True))
            o = _dot(p.astype(BF16), _head_lanes(kv, ATT_HEADS, mem_len).astype(BF16))
            o = o / jnp.sum(p, axis=1, keepdims=True)
            o_ref[rs, :] = jnp.concatenate(
                [o[h * tq:(h + 1) * tq, h * HEAD_DIM:(h + 1) * HEAD_DIM] for h in range(ATT_HEADS)], axis=1)
        return
    for b in range(nseq):
        rs = slice(b * tq, (b + 1) * tq)
        kv = kv_ref.at[b]
        for h in range(ATT_HEADS):
            sl = slice(h * HEAD_DIM, (h + 1) * HEAD_DIM)
            q = (q_ref[rs, sl] * scale).astype(BF16)
            k = kv[:, sl].astype(BF16)
            v = kv[:, ATT_OUT + h * HEAD_DIM: ATT_OUT + (h + 1) * HEAD_DIM].astype(BF16)
            s = _dot_nt(q, k)
            p = jnp.exp(s - jnp.max(s, axis=1, keepdims=True))
            o_ref[rs, sl] = _dot(p.astype(BF16), v) / jnp.sum(p, axis=1, keepdims=True)


def _cross_attn(q, mem_kv, out_buf, row0, batch, seq, tq, nseq):
    n = q.shape[0]
    nq = seq // tq
    assert nseq == 1 or nq == 1
    rows = nseq * tq
    rb0 = row0 // rows
    slab_rows = mem_kv.ndim == 4
    kv_spec = pl.BlockSpec((nseq,) + mem_kv.shape[1:], lambda b, i: (b,) + (0,) * (mem_kv.ndim - 1))
    in_specs = [pl.BlockSpec((rows, ATT_OUT), lambda b, i: (rb0 + b * nq + i, 0)), kv_spec]
    args = [q, mem_kv]
    aliases = {}
    if out_buf is not None:
        aliases = {2: 0}
        in_specs.append(pl.BlockSpec(memory_space=pl.ANY))
        args.append(out_buf)
    return pl.pallas_call(
        functools.partial(_cross_body, slab_rows=slab_rows, nseq=nseq),
        grid=(batch // nseq, nq),
        in_specs=in_specs,
        out_specs=pl.BlockSpec((rows, ATT_OUT), lambda b, i: (rb0 + b * nq + i, 0)),
        out_shape=jax.ShapeDtypeStruct((n, ATT_OUT), F32),
        input_output_aliases=aliases,
        compiler_params=_params("arbitrary", "arbitrary"),
        name=f"cross_attn_t{tq}",
    )(*args)


def _router_body(x_ref, g_ref, whi_ref, wlo_ref, b_ref, tri_ref, xs_ref, route_ref, cnt_ref, base_scr):
    tm = x_ref.shape[0]

    @pl.when(pl.program_id(0) == 0)
    def _():
        base_scr[...] = jnp.zeros_like(base_scr)

    x = x_ref[...]
    ms = jnp.mean(x * x, axis=-1, keepdims=True)
    xn = x * lax.rsqrt(ms + EPS) * g_ref[...]
    for c in range(ROW_SLAB):
        xs_ref[pl.ds(c, tm, stride=ROW_SLAB), :] = xn[:, c * HEAD_DIM:(c + 1) * HEAD_DIM]
    hi = xn.astype(BF16)
    lo = (xn - hi.astype(F32)).astype(BF16)
    logits = _dot(hi, whi_ref[...]) + _dot(lo, whi_ref[...]) + _dot(hi, wlo_ref[...]) + b_ref[...]
    lane = lax.broadcasted_iota(jnp.int32, logits.shape, 1)
    big = jnp.int32(1 << 20)
    is_g = jnp.logical_and(lane >= N_EXPERTS, lane < N_EXPERTS + N_GROUPS)
    lg = jnp.where(is_g, logits, NEG)
    mg = jnp.max(lg, axis=1, keepdims=True)
    p_top = 1.0 / jnp.sum(jnp.where(is_g, jnp.exp(lg - mg), 0.0), axis=1, keepdims=True)
    g_idx = jnp.min(jnp.where(jnp.logical_and(is_g, lg == mg), lane, big), axis=1, keepdims=True) - N_EXPERTS
    in_grp = jnp.logical_and(lane < N_EXPERTS, lane // EXPERTS_PER_GROUP == g_idx)
    le = jnp.where(in_grp, logits, NEG)
    v1 = jnp.max(le, axis=1, keepdims=True)
    i1 = jnp.min(jnp.where(jnp.logical_and(in_grp, le == v1), lane, big), axis=1, keepdims=True)
    rest = jnp.logical_and(in_grp, lane != i1)
    le2 = jnp.where(rest, logits, NEG)
    v2 = jnp.max(le2, axis=1, keepdims=True)
    i2 = jnp.min(jnp.where(jnp.logical_and(rest, le2 == v2), lane, big), axis=1, keepdims=True)
    e2 = jnp.exp(v2 - v1)
    w1 = p_top / (1.0 + e2)
    w2 = p_top * e2 / (1.0 + e2)
    hit1, hit2 = lane == i1, lane == i2
    hits = jnp.where(jnp.logical_or(hit1, hit2), 1.0, 0.0)
    before = _dot(tri_ref[...], hits.astype(BF16)) + base_scr[...]
    r1 = jnp.sum(jnp.where(hit1, before, 0.0), axis=1, keepdims=True)
    r2 = jnp.sum(jnp.where(hit2, before, 0.0), axis=1, keepdims=True)
    base_scr[...] += jnp.sum(hits, axis=0, keepdims=True)
    cnt_ref[...] = base_scr[...]
    cols = (i1.astype(F32), i2.astype(F32), w1, w2, r1, r2)
    route = jnp.zeros(logits.shape, F32)
    for c, val in enumerate(cols):
        route = jnp.where(lane == c, val, route)
    route_ref[...] = route


def _router(x, g, w_hi, w_lo, bias, tm):
    m, d = x.shape
    lanes = w_hi.shape[1]
    const = lambda shape: pl.BlockSpec(shape, lambda i: (0, 0))
    tri = jnp.tril(jnp.ones((tm, tm), F32), -1).astype(BF16)
    return pl.pallas_call(
        _router_body,
        grid=(m // tm,),
        in_specs=[pl.BlockSpec((tm, d), lambda i: (i, 0)), const((1, d)), const((d, lanes)), const((d, lanes)),
                  const((1, lanes)), const((tm, tm))],
        out_specs=[pl.BlockSpec((tm * ROW_SLAB, HEAD_DIM), lambda i: (i, 0)),
                   pl.BlockSpec((tm, lanes), lambda i: (i, 0)), const((1, lanes))],
        out_shape=[jax.ShapeDtypeStruct((m * ROW_SLAB, HEAD_DIM), F32), jax.ShapeDtypeStruct((m, lanes), F32),
                   jax.ShapeDtypeStruct((1, lanes), F32)],
        scratch_shapes=[pltpu.VMEM((1, lanes), F32)],
        compiler_params=_params("arbitrary"),
        name="router",
    )(x, g.reshape(1, d), w_hi, w_lo, bias, tri)


def _gather_rows(idx_ref, src_ref, dst_ref, sem, n):
    def issue(p, carry):
        r = pl.multiple_of(idx_ref[0, p] * ROW_SLAB, ROW_SLAB)
        o = pl.multiple_of(p * ROW_SLAB, ROW_SLAB)
        pltpu.make_async_copy(src_ref.at[pl.ds(r, ROW_SLAB), :], dst_ref.at[pl.ds(o, ROW_SLAB), :], sem).start()
        return carry

    lax.fori_loop(0, n, issue, 0, unroll=8)


def _wait_rows(src_ref, dst_ref, sem):
    pltpu.make_async_copy(src_ref.at[pl.ds(0, dst_ref.shape[0]), :], dst_ref, sem).wait()


def _slab_to_rows(ref, rows, first=0):
    return jnp.concatenate([ref[pl.ds(first * ROW_SLAB + c, rows, stride=ROW_SLAB), :] for c in range(ROW_SLAB)],
                           axis=1)


def _tile_idx_specs(tile, steps):
    def cur(t, *_):
        return (t, 0, 0)

    def nxt(t, *_):
        return (jnp.minimum(t + 1, steps - 1), 0, 0)

    return [pl.BlockSpec((None, 1, tile), cur, memory_space=pltpu.SMEM),
            pl.BlockSpec((None, 1, tile), nxt, memory_space=pltpu.SMEM)]


def _experts_body(te_ref, na_ref, slot_ref, next_ref, idx_ref, idx_next_ref, xs_ref, wg_ref, wu_ref, wd_ref, o_ref,
                  xb0, xb1, sem, wgf, wuf, wdf, wsem, wgb, wub, wdb, *, te_rows):
    t = pl.program_id(0)
    na = na_ref[0]
    active = t < na
    bufs = (xb0, xb1)

    def weight_copies(e, s):
        return [pltpu.make_async_copy(src.at[e], dst.at[s], wsem.at[s])
                for src, dst in ((wg_ref, wgf), (wu_ref, wuf), (wd_ref, wdf))]

    @pl.when(jnp.logical_and(t == 0, active))
    def _():
        for cp in weight_copies(te_ref[0], 0):
            cp.start()
        _gather_rows(idx_ref, xs_ref, xb0, sem.at[0], te_rows)

    for s in range(2):
        @pl.when(jnp.logical_and(t + 1 < na, (t + 1) % 2 == s))
        def _(s=s):
            _gather_rows(idx_next_ref, xs_ref, bufs[s], sem.at[s], te_rows)

    changed = jnp.logical_or(t == 0, te_ref[t] != te_ref[jnp.maximum(t - 1, 0)])
    for s in range(2):
        @pl.when(jnp.logical_and(jnp.logical_and(active, changed), slot_ref[t] == s))
        def _(s=s):
            for cp in weight_copies(te_ref[t], s):
                cp.wait()
            wgb[...] = wgf[s].astype(BF16)
            wub[...] = wuf[s].astype(BF16)
            wdb[...] = wdf[s].astype(BF16)

            @pl.when(next_ref[t] >= 0)
            def _():
                for cp in weight_copies(next_ref[t], 1 - s):
                    cp.start()

    for s in range(2):
        @pl.when(jnp.logical_and(active, t % 2 == s))
        def _(s=s):
            _wait_rows(xs_ref, bufs[s], sem.at[s])
            x = _slab_to_rows(bufs[s], te_rows).astype(BF16)
            h = (_silu(_dot(x, wgb[...])) * _dot(x, wub[...])).astype(BF16)
            for c in range(0, ROW_SLAB, 2):
                o = _dot(h, wdb[:, c * HEAD_DIM:(c + 2) * HEAD_DIM])
                o_ref[pl.ds(c, te_rows, stride=ROW_SLAB), :] = o[:, :HEAD_DIM]
                o_ref[pl.ds(c + 1, te_rows, stride=ROW_SLAB), :] = o[:, HEAD_DIM:]

    @pl.when(jnp.logical_not(active))
    def _():
        o_ref[...] = jnp.zeros_like(o_ref)


def _experts(xs, src_tok, tile_expert, n_active, w_gate, w_up, w_down, te_rows):
    ne, d, ff = w_gate.shape
    nt = tile_expert.shape[0]
    tiles = jnp.arange(nt, dtype=jnp.int32)
    first = jnp.logical_and(jnp.concatenate([jnp.ones((1,), bool), tile_expert[1:] != tile_expert[:-1]]),
                            tiles < n_active[0])
    slot = (jnp.cumsum(first.astype(jnp.int32)) - 1) % 2
    later_first = jnp.concatenate([jnp.where(first, tiles, nt)[1:], jnp.full((1,), nt, jnp.int32)])
    next_first = lax.cummin(later_first, axis=0, reverse=True)
    next_expert = jnp.where(next_first < nt, tile_expert[jnp.minimum(next_first, nt - 1)], -1)
    any_spec = pl.BlockSpec(memory_space=pl.ANY)
    grid_spec = pltpu.PrefetchScalarGridSpec(
        num_scalar_prefetch=4,
        grid=(nt,),
        in_specs=_tile_idx_specs(te_rows, nt) + [any_spec] * 4,
        out_specs=pl.BlockSpec((te_rows * ROW_SLAB, HEAD_DIM), lambda t, *_: (t, 0)),
        scratch_shapes=[pltpu.VMEM((te_rows * ROW_SLAB, HEAD_DIM), F32),
                        pltpu.VMEM((te_rows * ROW_SLAB, HEAD_DIM), F32),
                        pltpu.SemaphoreType.DMA((2,)),
                        pltpu.VMEM((2, d, ff), F32), pltpu.VMEM((2, d, ff), F32), pltpu.VMEM((2, ff, d), F32),
                        pltpu.SemaphoreType.DMA((2,)),
                        pltpu.VMEM((d, ff), BF16), pltpu.VMEM((d, ff), BF16), pltpu.VMEM((ff, d), BF16)],
    )
    idx = src_tok.reshape(nt, 1, te_rows)
    return pl.pallas_call(
        functools.partial(_experts_body, te_rows=te_rows),
        grid_spec=grid_spec,
        out_shape=jax.ShapeDtypeStruct((nt * te_rows * ROW_SLAB, HEAD_DIM), F32),
        compiler_params=_params("arbitrary"),
        name="experts",
    )(tile_expert, n_active, slot.astype(jnp.int32), next_expert.astype(jnp.int32),
      idx, idx, xs, w_gate, w_up, w_down)


def _moe_out_body(idx_ref, idx_next_ref, x_ref, route_ref, ys_ref, gf_ref, yp_ref, ysm_ref, db0, db1, sem,
                  *, n_prompt_tiles):
    i = pl.program_id(0)
    nt = pl.num_programs(0)
    tm = x_ref.shape[0]
    bufs = (db0, db1)

    @pl.when(i == 0)
    def _():
        _gather_rows(idx_ref, ys_ref, db0, sem.at[0], 2 * tm)

    for s in range(2):
        @pl.when(jnp.logical_and(i + 1 < nt, (i + 1) % 2 == s))
        def _(s=s):
            _gather_rows(idx_next_ref, ys_ref, bufs[s], sem.at[s], 2 * tm)

    route = route_ref[...]
    w1, w2 = route[:, 2:3], route[:, 3:4]
    for s in range(2):
        @pl.when(i % 2 == s)
        def _(s=s):
            _wait_rows(ys_ref, bufs[s], sem.at[s])
            x = x_ref[...] + w1 * _slab_to_rows(bufs[s], tm) + w2 * _slab_to_rows(bufs[s], tm, first=tm)
            ms = jnp.mean(x * x, axis=-1, keepdims=True)
            y = x * lax.rsqrt(ms + EPS) * gf_ref[...]

            @pl.when(i < n_prompt_tiles)
            def _():
                yp_ref[...] = y

            @pl.when(i >= n_prompt_tiles)
            def _():
                ysm_ref[...] = y


def _moe_out(x, route, ys_sorted, pos1, pos2, g_final, n_prompt, tm):
    m, d = x.shape
    nt = m // tm
    npt = n_prompt // tm
    lanes = route.shape[1]
    idx = jnp.concatenate([pos1.reshape(nt, tm), pos2.reshape(nt, tm)], axis=1).reshape(nt, 1, 2 * tm)
    return pl.pallas_call(
        functools.partial(_moe_out_body, n_prompt_tiles=npt),
        grid=(nt,),
        in_specs=_tile_idx_specs(2 * tm, nt) + [
            pl.BlockSpec((tm, d), lambda i: (i, 0)), pl.BlockSpec((tm, lanes), lambda i: (i, 0)),
            pl.BlockSpec(memory_space=pl.ANY), pl.BlockSpec((1, d), lambda i: (0, 0))],
        out_specs=[pl.BlockSpec((tm, d), lambda i: (jnp.minimum(i, npt - 1), 0)),
                   pl.BlockSpec((tm, d), lambda i: (jnp.maximum(i - npt, 0), 0))],
        out_shape=[jax.ShapeDtypeStruct((n_prompt, d), F32), jax.ShapeDtypeStruct((m - n_prompt, d), F32)],
        scratch_shapes=[pltpu.VMEM((2 * tm * ROW_SLAB, HEAD_DIM), F32),
                        pltpu.VMEM((2 * tm * ROW_SLAB, HEAD_DIM), F32),
                        pltpu.SemaphoreType.DMA((2,))],
        compiler_params=_params("arbitrary"),
        name="moe_out",
    )(idx, idx, x, route, ys_sorted, g_final.reshape(1, d))


def _moe(x, g_ffn, w_rg, b_rg, w_re, b_re, w_gate, w_up, w_down, g_final, n_prompt):
    m, d = x.shape
    lanes = HEAD_DIM
    npad = lanes - N_EXPERTS - N_GROUPS
    w_r = jnp.concatenate([w_re, w_rg, jnp.zeros((d, npad), F32)], axis=1)
    b_r = jnp.concatenate([b_re, b_rg, jnp.zeros((npad,), F32)]).reshape(1, lanes)
    w_r_hi = w_r.astype(BF16)
    w_r_lo = (w_r - w_r_hi.astype(F32)).astype(BF16)
    xs, route, counts = _router(x, g_ffn, w_r_hi, w_r_lo, b_r, 512)

    te = MOE_TILE
    n_tiles = -(-(2 * m + N_EXPERTS * (te - 1)) // te)
    n_tiles += n_tiles % 2
    counts = counts[0, :N_EXPERTS].astype(jnp.int32)
    padded = (counts + te - 1) // te * te
    pad_end = jnp.cumsum(padded)
    pad_off = pad_end - padded
    experts = jnp.arange(N_EXPERTS, dtype=jnp.int32)

    def dest(col_e, col_r):
        e = route[:, col_e].astype(jnp.int32)
        off = jnp.sum(jnp.where(e[:, None] == experts[None, :], pad_off[None, :], 0), axis=1)
        return off + route[:, col_r].astype(jnp.int32)

    pos1, pos2 = dest(0, 4), dest(1, 5)
    tok = jnp.tile(jnp.arange(m, dtype=jnp.int32), 2)
    src_tok = jnp.zeros((n_tiles * te,), jnp.int32).at[jnp.concatenate([pos1, pos2])].set(tok)
    tile_start = jnp.arange(n_tiles, dtype=jnp.int32) * te
    tile_expert = jnp.minimum(jnp.sum(tile_start[:, None] >= pad_end[None, :], axis=1), N_EXPERTS - 1)
    n_active = (pad_end[-1] // te).reshape(1)

    ys_sorted = _experts(xs, src_tok, tile_expert.astype(jnp.int32), n_active.astype(jnp.int32),
                         w_gate, w_up, w_down, te)
    return _moe_out(x, route, ys_sorted, pos1, pos2, g_final, n_prompt, 256)


def _rope_tables(positions):
    half = HEAD_DIM // 2
    inv_freq = ROPE_THETA ** (-jnp.arange(half, dtype=F32) / half)
    ang = positions.astype(F32)[:, None] * inv_freq[None, :]
    cos, sin = jnp.cos(ang), jnp.sin(ang)
    return jnp.concatenate([cos, cos], axis=1), jnp.concatenate([-sin, sin], axis=1)


def kernel(x_prompt, x_sample, cache_swa1, cache_swa2, cache_swa3, state_hgrn, cache_mem_kv, mem_prompt,
           hgrn_lb_logits, norm_mix, w_in, w_proj_attn, w_proj_hgrn, w_out, hgrn_norm, norm_cross, norm_mem,
           w_cq, w_ckv, w_co, norm_ffn, w_rg, b_rg, w_re, b_re, w_e_gate, w_e_up, w_e_down, norm_final):
    bp, seq, d = x_prompt.shape
    bs, dseq, _ = x_sample.shape
    depth = w_in.shape[0]
    assert depth == 1, "single-layer trunk"
    past = cache_swa3.shape[2]
    mem_len = mem_prompt.shape[1]
    np_, ns = bp * seq, bs * dseq
    hw = HGRN_HEADS * HEAD_DIM
    col_hgrn = 3 * ATT_WIDTH
    col_ga = col_hgrn + 4 * hw
    col_gh = col_ga + d
    l = 0

    x_parts = (x_prompt.reshape(np_, d), x_sample.reshape(ns, d))
    pos = jnp.concatenate([jnp.tile(jnp.arange(seq, dtype=jnp.int32), bp),
                           jnp.tile(past + jnp.arange(dseq, dtype=jnp.int32), bs)])
    cos2, sin2 = _rope_tables(pos)

    xn = _rmsnorm2(*x_parts, norm_mix[l], BF16, 512)
    proj = _matmul(xn, w_in[l], F32, 512, ATT_WIDTH, rope=(cos2, sin2), name="proj_in")

    attn = _attn_prompt(proj, bp, seq, 512)
    attn = _attn_sample(proj, (cache_swa1[l], cache_swa2[l], cache_swa3[l]), attn, np_, bs, dseq)

    hgrn, st_p = _hgrn(proj, col_hgrn, hgrn_lb_logits, hgrn_norm[l], None, None, 0, bp, seq, 128, 8)
    hgrn, st_s = _hgrn(proj, col_hgrn, hgrn_lb_logits, hgrn_norm[l], state_hgrn[l], hgrn, np_, bs, dseq, dseq, 4)

    merged = _gated_merge(attn, hgrn, w_proj_attn[l], w_proj_hgrn[l], proj, col_ga, col_gh, 512, 1024, 512)
    x1 = _matmul(merged, w_out[l], F32, 512, 1024, res=x_parts, name="proj_out")

    mem_n = _rmsnorm(mem_prompt.reshape(bp * mem_len, d), norm_mem[l], BF16, 256)
    mkv_p = _matmul(mem_n, w_ckv[l], F32, 256, KV_ROW, name="mem_kv")
    qc = _matmul(x1, w_cq[l], F32, 512, ATT_OUT, norm=norm_cross[l], name="cross_q")
    oc = _cross_attn(qc, mkv_p.reshape(bp, mem_len, KV_ROW), None, 0, bp, seq, 512, 1)
    oc = _cross_attn(qc, cache_mem_kv[l].reshape(bs, mem_len, KV_SLAB, HEAD_DIM), oc, np_, bs, dseq, dseq, 4)
    x2 = _matmul(oc, w_co[l], F32, 512, 1024, res=x1, name="cross_out")

    ff = w_e_gate.shape[-1]
    y_p, y_s = _moe(x2, norm_ffn[l], w_rg[l], b_rg[l], w_re[l], b_re[l],
                    w_e_gate[l].reshape(N_EXPERTS, d, ff), w_e_up[l].reshape(N_EXPERTS, d, ff),
                    w_e_down[l].reshape(N_EXPERTS, ff, d), norm_final, np_)

    swa_p = [_kv_rows(proj, g, 0, bp, seq, min(w, seq), min(w, seq, 256))[None]
             for g, (w, _) in enumerate(ATT_GROUPS)]
    swa_s = [_kv_rows(proj, g, np_, 1, ns, ns, 256).reshape(1, bs, dseq, 2, ATT_HEADS, HEAD_DIM)
             for g in range(len(ATT_GROUPS))]
    return (y_p.reshape(bp, seq, d), y_s.reshape(bs, dseq, d),
            swa_p[0], swa_p[1], swa_p[2], st_p[None],
            mkv_p.reshape(1, bp, mem_len, 2, ATT_HEADS, HEAD_DIM),
            swa_s[0], swa_s[1], swa_s[2], st_s[None])
```

```python
import functools

import jax
import jax.numpy as jnp
from jax import lax
from jax.experimental import pallas as pl
from jax.experimental.pallas import tpu as pltpu

F32 = jnp.float32
BF16 = jnp.bfloat16

ATT_GROUPS = ((128, 1), (512, 4), (2048, 16))
ATT_HEADS = 4
HEAD_DIM = 128
ATT_OUT = ATT_HEADS * HEAD_DIM
ATT_WIDTH = len(ATT_GROUPS) * ATT_OUT
KV_ROW = 2 * ATT_OUT
KV_SLAB = 2 * ATT_HEADS
HGRN_HEADS = 16
ROPE_THETA = 10000.0
EPS = 1e-6
N_EXPERTS = 32
N_GROUPS = 4
EXPERTS_PER_GROUP = 8
NEG = -1e30
ATT_BLOCK = 128
HGRN_SUB = 8
ROW_SLAB = 16
MOE_TILE = 128
VMEM_LIMIT = 56 * 1024 * 1024


def _params(*sem):
    return pltpu.CompilerParams(dimension_semantics=sem, vmem_limit_bytes=VMEM_LIMIT)


def _dot(a, b):
    return jnp.dot(a, b, preferred_element_type=F32)


def _dot_nt(a, b):
    return lax.dot_general(a, b, (((1,), (1,)), ((), ())), preferred_element_type=F32)


def _sigmoid(x):
    return 1.0 / (1.0 + jnp.exp(-x))


def _silu(x):
    return x * _sigmoid(x)


def _rmsnorm_body(x_ref, g_ref, o_ref):
    x = x_ref[...]
    ms = jnp.mean(x * x, axis=-1, keepdims=True)
    o_ref[...] = (x * lax.rsqrt(ms + EPS) * g_ref[...]).astype(o_ref.dtype)


def _rmsnorm(x, g, out_dtype, tm):
    m, d = x.shape
    return pl.pallas_call(
        _rmsnorm_body,
        grid=(m // tm,),
        in_specs=[pl.BlockSpec((tm, d), lambda i: (i, 0)), pl.BlockSpec((1, d), lambda i: (0, 0))],
        out_specs=pl.BlockSpec((tm, d), lambda i: (i, 0)),
        out_shape=jax.ShapeDtypeStruct((m, d), out_dtype),
        compiler_params=_params("arbitrary"),
        name="rmsnorm",
    )(x, g.reshape(1, d))


def _rmsnorm2_body(xa_ref, xb_ref, g_ref, o_ref, *, na_tiles):
    i = pl.program_id(0)
    for ref, cond in ((xa_ref, i < na_tiles), (xb_ref, i >= na_tiles)):
        @pl.when(cond)
        def _(ref=ref):
            x = ref[...]
            ms = jnp.mean(x * x, axis=-1, keepdims=True)
            o_ref[...] = (x * lax.rsqrt(ms + EPS) * g_ref[...]).astype(o_ref.dtype)


def _two_part_specs(tm, width, na_tiles, index=lambda i: i, col=lambda i: 0):
    return [pl.BlockSpec((tm, width), lambda *a: (jnp.minimum(index(*a), na_tiles - 1), col(*a))),
            pl.BlockSpec((tm, width), lambda *a: (jnp.maximum(index(*a) - na_tiles, 0), col(*a)))]


def _rmsnorm2(xa, xb, g, out_dtype, tm):
    (ma, d), mb = xa.shape, xb.shape[0]
    na_tiles = ma // tm
    return pl.pallas_call(
        functools.partial(_rmsnorm2_body, na_tiles=na_tiles),
        grid=((ma + mb) // tm,),
        in_specs=_two_part_specs(tm, d, na_tiles) + [pl.BlockSpec((1, d), lambda i: (0, 0))],
        out_specs=pl.BlockSpec((tm, d), lambda i: (i, 0)),
        out_shape=jax.ShapeDtypeStruct((ma + mb, d), out_dtype),
        compiler_params=_params("arbitrary"),
        name="rmsnorm2",
    )(xa, xb, g.reshape(1, d))


def _mm_body(*refs, has_norm, has_res, has_rope, res_a_tiles):
    refs = list(refs)
    a_ref, w_ref = refs[:2]
    o_ref, wb_ref = refs[-2:]
    extra = refs[2:-2]
    j = pl.program_id(0)

    @pl.when(pl.program_id(1) == 0)
    def _():
        wb_ref[...] = w_ref[...].astype(BF16)

    a = a_ref[...]
    if has_norm:
        g_ref = extra.pop(0)
        ms = jnp.mean(a * a, axis=-1, keepdims=True)
        a = a * lax.rsqrt(ms + EPS) * g_ref[...]
    a = a.astype(BF16)
    if not has_rope:
        acc = _dot(a, wb_ref[...])
        if has_res and res_a_tiles is None:
            acc = acc + extra[0][...]
        if has_res and res_a_tiles is not None:
            i = pl.program_id(1)
            for r_ref, cond in ((extra[0], i < res_a_tiles), (extra[1], i >= res_a_tiles)):
                @pl.when(cond)
                def _(r_ref=r_ref):
                    o_ref[...] = (acc + r_ref[...]).astype(o_ref.dtype)
            return
        o_ref[...] = acc.astype(o_ref.dtype)
        return

    cos_ref, sin_ref = extra[-2:]

    @pl.when(j >= 2)
    def _():
        o_ref[...] = _dot(a, wb_ref[...]).astype(o_ref.dtype)

    @pl.when(j < 2)
    def _():
        cos = cos_ref[...]
        sin = sin_ref[...]
        scale = jnp.where(j == 0, HEAD_DIM ** -0.5, 1.0)
        for p in range(0, ATT_WIDTH, 2 * HEAD_DIM):
            acc = _dot(a, wb_ref[:, p:p + 2 * HEAD_DIM])
            for u in range(2):
                x = acc[:, u * HEAD_DIM:(u + 1) * HEAD_DIM]
                sl = slice(p + u * HEAD_DIM, p + (u + 1) * HEAD_DIM)
                o_ref[:, sl] = ((x * cos + pltpu.roll(x, HEAD_DIM // 2, axis=1) * sin) * scale).astype(o_ref.dtype)


def _matmul(a, w, out_dtype, tm, tn, norm=None, res=None, rope=None, name="matmul"):
    m, k = a.shape
    n = w.shape[1]
    in_specs = [pl.BlockSpec((tm, k), lambda j, i: (i, 0)), pl.BlockSpec((k, tn), lambda j, i: (0, j))]
    args = [a, w]
    if norm is not None:
        in_specs.append(pl.BlockSpec((1, k), lambda j, i: (0, 0)))
        args.append(norm.reshape(1, k))
    res_a_tiles = None
    if isinstance(res, tuple):
        res_a_tiles = res[0].shape[0] // tm
        in_specs.extend(_two_part_specs(tm, tn, res_a_tiles, index=lambda j, i: i, col=lambda j, i: j))
        args.extend(res)
    elif res is not None:
        in_specs.append(pl.BlockSpec((tm, tn), lambda j, i: (i, j)))
        args.append(res)
    if rope is not None:
        assert tn == ATT_WIDTH
        in_specs.extend([pl.BlockSpec((tm, HEAD_DIM), lambda j, i: (i, 0))] * 2)
        args.extend(rope)
    return pl.pallas_call(
        functools.partial(_mm_body, has_norm=norm is not None, has_res=res is not None,
                          has_rope=rope is not None, res_a_tiles=res_a_tiles),
        grid=(n // tn, m // tm),
        in_specs=in_specs,
        out_specs=pl.BlockSpec((tm, tn), lambda j, i: (i, j)),
        out_shape=jax.ShapeDtypeStruct((m, n), out_dtype),
        scratch_shapes=[pltpu.VMEM((k, tn), BF16)],
        compiler_params=_params("arbitrary", "arbitrary"),
        name=name,
    )(*args)


def _prompt_key_blocks(bq):
    table = []
    for g, (w, _) in enumerate(ATT_GROUPS):
        nback = -(-w // bq)
        table.extend((g, back) for back in range(nback, -1, -1))
    return tuple(table)


def _attn_prompt_body(q_ref, k_ref, v_ref, o_ref, m_scr, l_scr, acc_scr, *, bq, table):
    i = pl.program_id(1)
    j = pl.program_id(2)
    ng = len(ATT_GROUPS)

    @pl.when(j == 0)
    def _():
        m_scr[...] = jnp.full(m_scr.shape, NEG, F32)
        l_scr[...] = jnp.zeros(l_scr.shape, F32)
        acc_scr[...] = jnp.zeros(acc_scr.shape, F32)

    row = lax.broadcasted_iota(jnp.int32, (bq, bq), 0)
    col = lax.broadcasted_iota(jnp.int32, (bq, bq), 1)
    for g, (w, d) in enumerate(ATT_GROUPS):
        j0 = min(jj for jj, (gg, _) in enumerate(table) if gg == g)
        nback = max(back for gg, back in table if gg == g)
        back = nback - (j - j0)

        @pl.when(jnp.logical_and(jnp.logical_and(j >= j0, j <= j0 + nback), i >= back))
        def _(g=g, w=w, d=d, back=back):
            dist = back * bq + row - col
            valid = jnp.logical_and(jnp.logical_and(dist >= 0, dist <= w), (dist & (d - 1)) == 0)
            bias = jnp.where(valid, 0.0, NEG)
            heads = range(ATT_HEADS)
            hs = [slice(h * HEAD_DIM, (h + 1) * HEAD_DIM) for h in heads]
            m_old = [m_scr[g * ATT_HEADS + h] for h in heads]
            l_old = [l_scr[g * ATT_HEADS + h] for h in heads]
            a_old = [acc_scr[g, :, hs[h]] for h in heads]
            qs = [q_ref[:, g * ATT_OUT + h * HEAD_DIM: g * ATT_OUT + (h + 1) * HEAD_DIM].astype(BF16) for h in heads]
            ss = [_dot_nt(qs[h], k_ref[:, hs[h]].astype(BF16)) + bias for h in heads]
            m_new = [jnp.maximum(m_old[h], jnp.max(ss[h], axis=1, keepdims=True)) for h in heads]
            ps = [jnp.exp(ss[h] - jnp.concatenate([m_new[h]] * (bq // HEAD_DIM), axis=1)) for h in heads]
            alpha = [jnp.exp(m_old[h] - m_new[h]) for h in heads]
            pv = [_dot(ps[h].astype(BF16), v_ref[:, hs[h]].astype(BF16)) for h in heads]
            for h in heads:
                m_scr[g * ATT_HEADS + h] = m_new[h]
                l_scr[g * ATT_HEADS + h] = alpha[h] * l_old[h] + jnp.sum(ps[h], axis=1, keepdims=True)
                acc_scr[g, :, hs[h]] = alpha[h] * a_old[h] + pv[h]

    @pl.when(j == pl.num_programs(2) - 1)
    def _():
        for h in range(ATT_HEADS):
            sl = slice(h * HEAD_DIM, (h + 1) * HEAD_DIM)
            lse = [m_scr[g * ATT_HEADS + h] + jnp.log(l_scr[g * ATT_HEADS + h]) for g in range(ng)]
            mx = functools.reduce(jnp.maximum, lse)
            ws = [jnp.exp(x - mx) for x in lse]
            num = sum(ws[g] / l_scr[g * ATT_HEADS + h] * acc_scr[g, :, sl] for g in range(ng))
            o_ref[:, sl] = num / sum(ws)


def _attn_prompt(proj, batch, seq, bq):
    n = proj.shape[0]
    ng = len(ATT_GROUPS)
    nq = seq // bq
    table = _prompt_key_blocks(bq)
    groups = jnp.asarray([g for g, _ in table], jnp.int32)
    backs = jnp.asarray([b for _, b in table], jnp.int32)

    def kv_map(part):
        def index(b, i, j, g_ref, back_ref):
            return (b * nq + jnp.maximum(i - back_ref[j], 0), part * ng + g_ref[j])
        return index

    grid_spec = pltpu.PrefetchScalarGridSpec(
        num_scalar_prefetch=2,
        grid=(batch, nq, len(table)),
        in_specs=[pl.BlockSpec((bq, ATT_WIDTH), lambda b, i, j, g_ref, back_ref: (b * nq + i, 0)),
                  pl.BlockSpec((bq, ATT_OUT), kv_map(1)),
                  pl.BlockSpec((bq, ATT_OUT), kv_map(2))],
        out_specs=pl.BlockSpec((bq, ATT_OUT), lambda b, i, j, g_ref, back_ref: (b * nq + i, 0)),
        scratch_shapes=[pltpu.VMEM((ng * ATT_HEADS, bq, HEAD_DIM), F32),
                        pltpu.VMEM((ng * ATT_HEADS, bq, HEAD_DIM), F32),
                        pltpu.VMEM((ng, bq, ATT_OUT), F32)],
    )

    def body(g_ref, back_ref, *refs):
        del g_ref, back_ref
        _attn_prompt_body(*refs, bq=bq, table=table)

    return pl.pallas_call(
        body,
        grid_spec=grid_spec,
        out_shape=jax.ShapeDtypeStruct((n, ATT_OUT), F32),
        compiler_params=_params("arbitrary", "arbitrary", "arbitrary"),
        name="attn_prompt",
    )(groups, backs, proj, proj, proj)


def _head_lanes(ref2d, first, rows):
    return jnp.concatenate([ref2d[pl.ds(first + h, rows, stride=KV_SLAB), :] for h in range(ATT_HEADS)], axis=1)


def _attn_sample_body(q_ref, kn_ref, vn_ref, *rest, tq):
    cache_refs, o_ref = rest[:-2], rest[-1]
    nrow = ATT_HEADS * tq
    nkey = ATT_BLOCK
    rid = lax.broadcasted_iota(jnp.int32, (nrow, 1), 0)
    i_row = rid % tq
    head_row = rid // tq
    lane_head = lax.broadcasted_iota(jnp.int32, (1, ATT_OUT), 1) // HEAD_DIM
    head_mask = head_row == lane_head
    key = lax.broadcasted_iota(jnp.int32, (1, nkey), 1)
    pad = jnp.zeros((nkey - tq, ATT_OUT), F32)
    outs, lses = [], []
    ci = 0
    for g, (_, d) in enumerate(ATT_GROUPS):
        qg = q_ref[:, g * ATT_OUT:(g + 1) * ATT_OUT]
        qrows = jnp.where(head_mask, jnp.concatenate([qg] * ATT_HEADS, axis=0), 0.0).astype(BF16)
        k_new = jnp.concatenate([kn_ref[:, g * ATT_OUT:(g + 1) * ATT_OUT], pad], axis=0)
        v_new = jnp.concatenate([vn_ref[:, g * ATT_OUT:(g + 1) * ATT_OUT], pad], axis=0)
        valid_new = jnp.logical_and(jnp.logical_and(key < tq, key <= i_row), (key % d) == (i_row % d))
        blocks = [(k_new, v_new, valid_new)]
        for r in range(min(d, tq)):
            c2 = cache_refs[ci].reshape(nkey * KV_SLAB, HEAD_DIM)
            ci += 1
            valid = jnp.logical_and((i_row % d) == r, key >= i_row // d)
            blocks.append((_head_lanes(c2, 0, nkey), _head_lanes(c2, ATT_HEADS, nkey), valid))
        ss = [jnp.where(valid, _dot_nt(qrows, kb.astype(BF16)), NEG) for kb, _, valid in blocks]
        m = functools.reduce(jnp.maximum, [jnp.max(s, axis=1, keepdims=True) for s in ss])
        ps = [jnp.exp(s - m) for s in ss]
        l = sum(jnp.sum(p, axis=1, keepdims=True) for p in ps)
        acc = sum(_dot(p.astype(BF16), vb.astype(BF16)) for p, (_, vb, _) in zip(ps, blocks))
        o = acc / l
        lse = m + jnp.log(l)
        outs.append(jnp.concatenate(
            [o[h * tq:(h + 1) * tq, h * HEAD_DIM:(h + 1) * HEAD_DIM] for h in range(ATT_HEADS)], axis=1))
        lses.append(jnp.concatenate(
            [jnp.broadcast_to(lse[h * tq:(h + 1) * tq], (tq, HEAD_DIM)) for h in range(ATT_HEADS)], axis=1))
    mx = jnp.maximum(jnp.maximum(lses[0], lses[1]), lses[2])
    ws = [jnp.exp(x - mx) for x in lses]
    o_ref[...] = (ws[0] * outs[0] + ws[1] * outs[1] + ws[2] * outs[2]) / (ws[0] + ws[1] + ws[2])


def _attn_sample(proj, caches, attn_buf, row0, batch, tq):
    blk0 = row0 // tq
    views, specs = [], []
    for (w, d), c in zip(ATT_GROUPS, caches):
        assert c.shape[1] == w and w // d == ATT_BLOCK, "window buffers must hold exactly one window"
        view = c.reshape(batch, w // d, d, KV_SLAB, HEAD_DIM)
        for r in range(min(d, tq)):
            views.append(view)
            specs.append(pl.BlockSpec((None, w // d, None, KV_SLAB, HEAD_DIM), lambda b, r=r: (b, 0, r, 0, 0)))
    n_in = 3 + len(views)
    qkv = [pl.BlockSpec((tq, ATT_WIDTH), lambda b, part=part: (blk0 + b, part)) for part in range(3)]
    return pl.pallas_call(
        functools.partial(_attn_sample_body, tq=tq),
        grid=(batch,),
        in_specs=qkv + specs + [pl.BlockSpec(memory_space=pl.ANY)],
        out_specs=pl.BlockSpec((tq, ATT_OUT), lambda b: (blk0 + b, 0)),
        out_shape=jax.ShapeDtypeStruct(attn_buf.shape, F32),
        input_output_aliases={n_in: 0},
        compiler_params=_params("arbitrary"),
        name="attn_sample",
    )(proj, proj, proj, *views, attn_buf)


def _kv_rows_body(k_ref, v_ref, o_ref):
    tm = k_ref.shape[0]
    for h in range(ATT_HEADS):
        sl = slice(h * HEAD_DIM, (h + 1) * HEAD_DIM)
        o_ref[pl.ds(h, tm, stride=KV_SLAB), :] = k_ref[:, sl]
        o_ref[pl.ds(ATT_HEADS + h, tm, stride=KV_SLAB), :] = v_ref[:, sl]


def _kv_rows(proj, g, row0, batch, seq, keep, tm):
    ng = len(ATT_GROUPS)
    nt = keep // tm
    rb0 = (row0 + seq - keep) // tm
    per_seq = seq // tm
    out = pl.pallas_call(
        _kv_rows_body,
        grid=(batch, nt),
        in_specs=[pl.BlockSpec((tm, ATT_OUT), lambda b, t: (rb0 + b * per_seq + t, ng + g)),
                  pl.BlockSpec((tm, ATT_OUT), lambda b, t: (rb0 + b * per_seq + t, 2 * ng + g))],
        out_specs=pl.BlockSpec((tm * KV_SLAB, HEAD_DIM), lambda b, t: (b * nt + t, 0)),
        out_shape=jax.ShapeDtypeStruct((batch * keep * KV_SLAB, HEAD_DIM), F32),
        compiler_params=_params("arbitrary", "arbitrary"),
        name=f"kv_rows_g{g}_{tm}",
    )(proj, proj)
    return out.reshape(batch, keep, 2, ATT_HEADS, HEAD_DIM)


def _cumsum_rows(x, seg):
    row = lax.broadcasted_iota(jnp.int32, (x.shape[0], 1), 0) % seg
    sh = 1
    while sh < seg:
        x = x + jnp.where(row >= sh, pltpu.roll(x, sh, axis=0), 0.0)
        sh *= 2
    return x


def _bcast_rows(b, first, period):
    c, w = b.shape
    parts = [jnp.broadcast_to(b[p * period + first: p * period + first + 1, :], (period, w))
             for p in range(c // period)]
    return parts[0] if len(parts) == 1 else jnp.concatenate(parts, axis=0)


def _hgrn_body(*refs, chunk, nseq, has_s0, chained):
    refs = list(refs)
    qh_ref, fh_ref, ih_ref, og_ref, lbl_ref, gn_ref = refs[:6]
    s0_ref = refs[6] if has_s0 else None
    o_ref, so_ref, st_ref = refs[-3:]
    c = chunk
    nh = ATT_HEADS
    t = pl.program_id(2)

    if chained:
        @pl.when(t == 0)
        def _():
            for h in range(nh):
                st_ref[h] = s0_ref[0, h].T if has_s0 else jnp.zeros((HEAD_DIM, HEAD_DIM), F32)

    lbl = lbl_ref[...]
    e = jnp.exp(lbl - jnp.max(lbl, axis=0, keepdims=True))
    lb = e[0:1, :] / jnp.sum(e, axis=0, keepdims=True)
    f = lb + (1.0 - lb) * _sigmoid(fh_ref[...])
    kk = 1.0 - f
    qf = _silu(qh_ref[...])
    b = _cumsum_rows(jnp.log(f), c)
    vv = ih_ref[...]

    row = lax.broadcasted_iota(jnp.int32, (c, c), 0)
    col = lax.broadcasted_iota(jnp.int32, (c, c), 1)
    rid = lax.broadcasted_iota(jnp.int32, (nseq * c, 1), 0)
    seqs = [slice(q * c, (q + 1) * c) for q in range(nseq)]
    heads = [slice(h * HEAD_DIM, (h + 1) * HEAD_DIM) for h in range(nh)]

    a = [[jnp.zeros((c, c), F32) for _ in range(nh)] for _ in range(nseq)]
    s = c // 2
    while s >= HGRN_SUB:
        ref = _bcast_rows(b, s - 1, 2 * s)
        second = (rid % (2 * s)) >= s
        dlt = b - ref
        ee = jnp.exp(jnp.where(second, dlt, -dlt))
        ql = jnp.where(second, qf * ee, 0.0).astype(BF16)
        kl = jnp.where(second, 0.0, kk * ee).astype(BF16)
        same = (row // (2 * s)) == (col // (2 * s))
        for q in range(nseq):
            for h in range(nh):
                a[q][h] = a[q][h] + jnp.where(same, _dot_nt(ql[seqs[q], heads[h]], kl[seqs[q], heads[h]]), 0.0)
        s //= 2
    sub = min(HGRN_SUB, c)
    ref = _bcast_rows(b, 0, sub)
    dlt = b - ref
    qd = (qf * jnp.exp(dlt)).astype(BF16)
    kd = (kk * jnp.exp(jnp.minimum(-dlt, 80.0))).astype(BF16)
    diag = jnp.logical_and((row // sub) == (col // sub), col <= row)
    qe = (qf * jnp.exp(b)).astype(BF16)
    b_last = _bcast_rows(b, c - 1, c)
    kend = (kk * jnp.exp(b_last - b)).astype(BF16)
    dec = jnp.exp(b_last)
    gn = gn_ref[...]
    gate = _silu(og_ref[...])
    vb = vv.astype(BF16)
    eye = (lax.broadcasted_iota(jnp.int32, (HEAD_DIM, HEAD_DIM), 0)
           == lax.broadcasted_iota(jnp.int32, (HEAD_DIM, HEAD_DIM), 1))
    tn_dims = (((0,), (0,)), ((), ()))
    for h in range(nh):
        sl = heads[h]
        st = st_ref[h] if chained else None
        for q in range(nseq):
            rs = seqs[q]
            ah = a[q][h] + jnp.where(diag, _dot_nt(qd[rs, sl], kd[rs, sl]), 0.0)
            intra = _dot(ah.astype(BF16), vb[rs, sl])
            dec_q = dec[q * c:q * c + 1, sl]
            if chained:
                o = _dot_nt(qe[rs, sl], st.astype(BF16)) + intra
                st = st * dec_q + lax.dot_general(vb[rs, sl], kend[rs, sl], tn_dims, preferred_element_type=F32)
            else:
                s_kv = s0_ref[q, h]
                o = _dot(qe[rs, sl], s_kv.astype(BF16)) + intra
                dec_col = jnp.sum(jnp.where(eye, jnp.broadcast_to(dec_q, (HEAD_DIM, HEAD_DIM)), 0.0),
                                  axis=1, keepdims=True)
                so_ref[q, h] = s_kv * dec_col + lax.dot_general(kend[rs, sl], vb[rs, sl], tn_dims,
                                                                preferred_element_type=F32)
            ms = jnp.mean(o * o, axis=-1, keepdims=True)
            o_ref[rs, sl] = (o * lax.rsqrt(ms + EPS) * gn * gate[rs, sl]).astype(o_ref.dtype)
        if chained:
            st_ref[h] = st

            @pl.when(t == pl.num_programs(2) - 1)
            def _(h=h, st=st):
                so_ref[0, h] = st.T


def _hgrn(proj, col0, lb_logits, gnorm, s0, out_buf, row0, batch, seq, chunk, nseq):
    n = proj.shape[0]
    chained = seq != chunk
    assert chained or s0 is not None
    nt = seq // (chunk * nseq) if chained else 1
    nstate = 1 if chained else nseq
    hb = HGRN_HEADS // ATT_HEADS
    width = HGRN_HEADS * HEAD_DIM
    cb0 = col0 // ATT_OUT
    rows = nseq * chunk
    rb0 = row0 // rows

    def col(j):
        return pl.BlockSpec((rows, ATT_OUT), lambda b, h, t, j=j: (rb0 + b * nt + t, cb0 + j * hb + h))

    in_specs = [col(0), col(1), col(2), col(3),
                pl.BlockSpec((lb_logits.shape[0], ATT_OUT), lambda b, h, t: (0, h)),
                pl.BlockSpec((1, HEAD_DIM), lambda b, h, t: (0, 0))]
    args = [proj, proj, proj, proj, lb_logits, gnorm.reshape(1, HEAD_DIM)]
    state_spec = pl.BlockSpec((nstate, ATT_HEADS, HEAD_DIM, HEAD_DIM), lambda b, h, t: (b, h, 0, 0))
    if s0 is not None:
        in_specs.append(state_spec)
        args.append(s0)
    aliases = {}
    if out_buf is not None:
        aliases = {len(args): 0}
        in_specs.append(pl.BlockSpec(memory_space=pl.ANY))
        args.append(out_buf)
    return pl.pallas_call(
        functools.partial(_hgrn_body, chunk=chunk, nseq=nseq, has_s0=s0 is not None, chained=chained),
        grid=(batch // nstate, hb, nt),
        in_specs=in_specs,
        out_specs=[pl.BlockSpec((rows, ATT_OUT), lambda b, h, t: (rb0 + b * nt + t, h)), state_spec],
        out_shape=[jax.ShapeDtypeStruct((n, width), BF16),
                   jax.ShapeDtypeStruct((batch, HGRN_HEADS, HEAD_DIM, HEAD_DIM), F32)],
        scratch_shapes=[pltpu.VMEM((ATT_HEADS, HEAD_DIM, HEAD_DIM), F32)],
        input_output_aliases=aliases,
        compiler_params=_params("arbitrary", "arbitrary", "arbitrary"),
        name=f"hgrn_c{chunk}",
    )(*args)


def _gated_merge_body(att_ref, hg_ref, wa_ref, wh_ref, *rest, ngate):
    ga_refs, gh_refs = rest[:ngate], rest[ngate:2 * ngate]
    o_ref, wab_ref, whb_ref = rest[2 * ngate:]

    @pl.when(pl.program_id(1) == 0)
    def _():
        wab_ref[...] = wa_ref[...].astype(BF16)
        whb_ref[...] = wh_ref[...].astype(BF16)

    pa = _dot(att_ref[...].astype(BF16), wab_ref[...])
    ph = _dot(hg_ref[...].astype(BF16), whb_ref[...])
    ga = jnp.concatenate([r[...] for r in ga_refs], axis=1)
    gh = jnp.concatenate([r[...] for r in gh_refs], axis=1)
    o_ref[...] = (_sigmoid(ga) * pa + _sigmoid(gh) * ph).astype(o_ref.dtype)


def _gated_merge(attn, hgrn, w_pa, w_ph, proj, col_ga, col_gh, tm, tn, tg):
    m = attn.shape[0]
    n = w_pa.shape[1]
    ka, kh = w_pa.shape[0], w_ph.shape[0]
    assert col_ga % tg == 0 and col_gh % tg == 0 and tn % tg == 0, "gate columns must start on a gate block"
    ngate = tn // tg
    gate = lambda c0, u: pl.BlockSpec((tm, tg), lambda j, i: (i, c0 // tg + j * ngate + u))
    gates = [gate(col_ga, u) for u in range(ngate)] + [gate(col_gh, u) for u in range(ngate)]
    return pl.pallas_call(
        functools.partial(_gated_merge_body, ngate=ngate),
        grid=(n // tn, m // tm),
        in_specs=[pl.BlockSpec((tm, ka), lambda j, i: (i, 0)),
                  pl.BlockSpec((tm, kh), lambda j, i: (i, 0)),
                  pl.BlockSpec((ka, tn), lambda j, i: (0, j)),
                  pl.BlockSpec((kh, tn), lambda j, i: (0, j))] + gates,
        out_specs=pl.BlockSpec((tm, tn), lambda j, i: (i, j)),
        out_shape=jax.ShapeDtypeStruct((m, n), BF16),
        scratch_shapes=[pltpu.VMEM((ka, tn), BF16), pltpu.VMEM((kh, tn), BF16)],
        compiler_params=_params("arbitrary", "arbitrary"),
        name="gated_merge",
    )(attn, hgrn, w_pa, w_ph, *([proj] * (2 * ngate)))


def _cross_body(q_ref, kv_ref, *rest, slab_rows, nseq):
    o_ref = rest[-1]
    scale = HEAD_DIM ** -0.5
    tq = q_ref.shape[0] // nseq
    mem_len = kv_ref.shape[1]
    if slab_rows:
        nrow = ATT_HEADS * tq
        head_row = lax.broadcasted_iota(jnp.int32, (nrow, 1), 0) // tq
        lane_head = lax.broadcasted_iota(jnp.int32, (1, ATT_OUT), 1) // HEAD_DIM
        head_mask = head_row == lane_head
        for b in range(nseq):
            rs = slice(b * tq, (b + 1) * tq)
            kv = kv_ref.at[b].reshape(mem_len * KV_SLAB, HEAD_DIM)
            qb = q_ref[rs, :] * scale
            qrows = jnp.where(head_mask, jnp.concatenate([qb] * ATT_HEADS, axis=0), 0.0).astype(BF16)
            s = _dot_nt(qrows, _head_lanes(kv, 0, mem_len).astype(BF16))
            p = jnp.exp(s - jnp.max(s, axis=1, keepdims=True))
            o = _dot(p.astype(BF16), _head_lanes(kv, ATT_HEADS, mem_len).astype(BF16))
            o = o / jnp.sum(p, axis=1, keepdims=True)
            o_ref[rs, :] = jnp.concatenate(
                [o[h * tq:(h + 1) * tq, h * HEAD_DIM:(h + 1) * HEAD_DIM] for h in range(ATT_HEADS)], axis=1)
        return
    for b in range(nseq):
        rs = slice(b * tq, (b + 1) * tq)
        kv = kv_ref.at[b]
        for h in range(ATT_HEADS):
            sl = slice(h * HEAD_DIM, (h + 1) * HEAD_DIM)
            q = (q_ref[rs, sl] * scale).astype(BF16)
            k = kv[:, sl].astype(BF16)
            v = kv[:, ATT_OUT + h * HEAD_DIM: ATT_OUT + (h + 1) * HEAD_DIM].astype(BF16)
            s = _dot_nt(q, k)
            p = jnp.exp(s - jnp.max(s, axis=1, keepdims=True))
            o_ref[rs, sl] = _dot(p.astype(BF16), v) / jnp.sum(p, axis=1, keepdims=True)


def _cross_attn(q, mem_kv, out_buf, row0, batch, seq, tq, nseq):
    n = q.shape[0]
    nq = seq // tq
    assert nseq == 1 or nq == 1
    rows = nseq * tq
    rb0 = row0 // rows
    slab_rows = mem_kv.ndim == 4
    kv_spec = pl.BlockSpec((nseq,) + mem_kv.shape[1:], lambda b, i: (b,) + (0,) * (mem_kv.ndim - 1))
    in_specs = [pl.BlockSpec((rows, ATT_OUT), lambda b, i: (rb0 + b * nq + i, 0)), kv_spec]
    args = [q, mem_kv]
    aliases = {}
    if out_buf is not None:
        aliases = {2: 0}
        in_specs.append(pl.BlockSpec(memory_space=pl.ANY))
        args.append(out_buf)
    return pl.pallas_call(
        functools.partial(_cross_body, slab_rows=slab_rows, nseq=nseq),
        grid=(batch // nseq, nq),
        in_specs=in_specs,
        out_specs=pl.BlockSpec((rows, ATT_OUT), lambda b, i: (rb0 + b * nq + i, 0)),
        out_shape=jax.ShapeDtypeStruct((n, ATT_OUT), F32),
        input_output_aliases=aliases,
        compiler_params=_params("arbitrary", "arbitrary"),
        name=f"cross_attn_t{tq}",
    )(*args)


def _router_body(x_ref, g_ref, whi_ref, wlo_ref, b_ref, tri_ref, xs_ref, route_ref, cnt_ref, base_scr):
    tm = x_ref.shape[0]

    @pl.when(pl.program_id(0) == 0)
    def _():
        base_scr[...] = jnp.zeros_like(base_scr)

    x = x_ref[...]
    ms = jnp.mean(x * x, axis=-1, keepdims=True)
    xn = x * lax.rsqrt(ms + EPS) * g_ref[...]
    for c in range(ROW_SLAB):
        xs_ref[pl.ds(c, tm, stride=ROW_SLAB), :] = xn[:, c * HEAD_DIM:(c + 1) * HEAD_DIM]
    hi = xn.astype(BF16)
    lo = (xn - hi.astype(F32)).astype(BF16)
    logits = _dot(hi, whi_ref[...]) + _dot(lo, whi_ref[...]) + _dot(hi, wlo_ref[...]) + b_ref[...]
    lane = lax.broadcasted_iota(jnp.int32, logits.shape, 1)
    big = jnp.int32(1 << 20)
    is_g = jnp.logical_and(lane >= N_EXPERTS, lane < N_EXPERTS + N_GROUPS)
    lg = jnp.where(is_g, logits, NEG)
    mg = jnp.max(lg, axis=1, keepdims=True)
    p_top = 1.0 / jnp.sum(jnp.where(is_g, jnp.exp(lg - mg), 0.0), axis=1, keepdims=True)
    g_idx = jnp.min(jnp.where(jnp.logical_and(is_g, lg == mg), lane, big), axis=1, keepdims=True) - N_EXPERTS
    in_grp = jnp.logical_and(lane < N_EXPERTS, lane // EXPERTS_PER_GROUP == g_idx)
    le = jnp.where(in_grp, logits, NEG)
    v1 = jnp.max(le, axis=1, keepdims=True)
    i1 = jnp.min(jnp.where(jnp.logical_and(in_grp, le == v1), lane, big), axis=1, keepdims=True)
    rest = jnp.logical_and(in_grp, lane != i1)
    le2 = jnp.where(rest, logits, NEG)
    v2 = jnp.max(le2, axis=1, keepdims=True)
    i2 = jnp.min(jnp.where(jnp.logical_and(rest, le2 == v2), lane, big), axis=1, keepdims=True)
    e2 = jnp.exp(v2 - v1)
    w1 = p_top / (1.0 + e2)
    w2 = p_top * e2 / (1.0 + e2)
    hit1, hit2 = lane == i1, lane == i2
    hits = jnp.where(jnp.logical_or(hit1, hit2), 1.0, 0.0)
    before = _dot(tri_ref[...], hits.astype(BF16)) + base_scr[...]
    r1 = jnp.sum(jnp.where(hit1, before, 0.0), axis=1, keepdims=True)
    r2 = jnp.sum(jnp.where(hit2, before, 0.0), axis=1, keepdims=True)
    base_scr[...] += jnp.sum(hits, axis=0, keepdims=True)
    cnt_ref[...] = base_scr[...]
    cols = (i1.astype(F32), i2.astype(F32), w1, w2, r1, r2)
    route = jnp.zeros(logits.shape, F32)
    for c, val in enumerate(cols):
        route = jnp.where(lane == c, val, route)
    route_ref[...] = route


def _router(x, g, w_hi, w_lo, bias, tm):
    m, d = x.shape
    lanes = w_hi.shape[1]
    const = lambda shape: pl.BlockSpec(shape, lambda i: (0, 0))
    tri = jnp.tril(jnp.ones((tm, tm), F32), -1).astype(BF16)
    return pl.pallas_call(
        _router_body,
        grid=(m // tm,),
        in_specs=[pl.BlockSpec((tm, d), lambda i: (i, 0)), const((1, d)), const((d, lanes)), const((d, lanes)),
                  const((1, lanes)), const((tm, tm))],
        out_specs=[pl.BlockSpec((tm * ROW_SLAB, HEAD_DIM), lambda i: (i, 0)),
                   pl.BlockSpec((tm, lanes), lambda i: (i, 0)), const((1, lanes))],
        out_shape=[jax.ShapeDtypeStruct((m * ROW_SLAB, HEAD_DIM), F32), jax.ShapeDtypeStruct((m, lanes), F32),
                   jax.ShapeDtypeStruct((1, lanes), F32)],
        scratch_shapes=[pltpu.VMEM((1, lanes), F32)],
        compiler_params=_params("arbitrary"),
        name="router",
    )(x, g.reshape(1, d), w_hi, w_lo, bias, tri)


def _gather_rows(idx_ref, src_ref, dst_ref, sem, n):
    def issue(p, carry):
        r = pl.multiple_of(idx_ref[0, p] * ROW_SLAB, ROW_SLAB)
        o = pl.multiple_of(p * ROW_SLAB, ROW_SLAB)
        pltpu.make_async_copy(src_ref.at[pl.ds(r, ROW_SLAB), :], dst_ref.at[pl.ds(o, ROW_SLAB), :], sem).start()
        return carry

    lax.fori_loop(0, n, issue, 0, unroll=8)


def _wait_rows(src_ref, dst_ref, sem):
    pltpu.make_async_copy(src_ref.at[pl.ds(0, dst_ref.shape[0]), :], dst_ref, sem).wait()


def _slab_to_rows(ref, rows, first=0):
    return jnp.concatenate([ref[pl.ds(first * ROW_SLAB + c, rows, stride=ROW_SLAB), :] for c in range(ROW_SLAB)],
                           axis=1)


def _tile_idx_specs(tile, steps):
    def cur(t, *_):
        return (t, 0, 0)

    def nxt(t, *_):
        return (jnp.minimum(t + 1, steps - 1), 0, 0)

    return [pl.BlockSpec((None, 1, tile), cur, memory_space=pltpu.SMEM),
            pl.BlockSpec((None, 1, tile), nxt, memory_space=pltpu.SMEM)]


def _experts_body(te_ref, na_ref, slot_ref, next_ref, idx_ref, idx_next_ref, xs_ref, wg_ref, wu_ref, wd_ref, o_ref,
                  xb0, xb1, sem, wgf, wuf, wdf, wsem, wgb, wub, wdb, *, te_rows):
    t = pl.program_id(0)
    na = na_ref[0]
    active = t < na
    bufs = (xb0, xb1)

    def weight_copies(e, s):
        return [pltpu.make_async_copy(src.at[e], dst.at[s], wsem.at[s])
                for src, dst in ((wg_ref, wgf), (wu_ref, wuf), (wd_ref, wdf))]

    @pl.when(jnp.logical_and(t == 0, active))
    def _():
        for cp in weight_copies(te_ref[0], 0):
            cp.start()
        _gather_rows(idx_ref, xs_ref, xb0, sem.at[0], te_rows)

    for s in range(2):
        @pl.when(jnp.logical_and(t + 1 < na, (t + 1) % 2 == s))
        def _(s=s):
            _gather_rows(idx_next_ref, xs_ref, bufs[s], sem.at[s], te_rows)

    changed = jnp.logical_or(t == 0, te_ref[t] != te_ref[jnp.maximum(t - 1, 0)])
    for s in range(2):
        @pl.when(jnp.logical_and(jnp.logical_and(active, changed), slot_ref[t] == s))
        def _(s=s):
            for cp in weight_copies(te_ref[t], s):
                cp.wait()
            wgb[...] = wgf[s].astype(BF16)
            wub[...] = wuf[s].astype(BF16)
            wdb[...] = wdf[s].astype(BF16)

            @pl.when(next_ref[t] >= 0)
            def _():
                for cp in weight_copies(next_ref[t], 1 - s):
                    cp.start()

    for s in range(2):
        @pl.when(jnp.logical_and(active, t % 2 == s))
        def _(s=s):
            _wait_rows(xs_ref, bufs[s], sem.at[s])
            x = _slab_to_rows(bufs[s], te_rows).astype(BF16)
            h = (_silu(_dot(x, wgb[...])) * _dot(x, wub[...])).astype(BF16)
            for c in range(0, ROW_SLAB, 2):
                o = _dot(h, wdb[:, c * HEAD_DIM:(c + 2) * HEAD_DIM])
                o_ref[pl.ds(c, te_rows, stride=ROW_SLAB), :] = o[:, :HEAD_DIM]
                o_ref[pl.ds(c + 1, te_rows, stride=ROW_SLAB), :] = o[:, HEAD_DIM:]

    @pl.when(jnp.logical_not(active))
    def _():
        o_ref[...] = jnp.zeros_like(o_ref)


def _experts(xs, src_tok, tile_expert, n_active, w_gate, w_up, w_down, te_rows):
    ne, d, ff = w_gate.shape
    nt = tile_expert.shape[0]
    tiles = jnp.arange(nt, dtype=jnp.int32)
    first = jnp.logical_and(jnp.concatenate([jnp.ones((1,), bool), tile_expert[1:] != tile_expert[:-1]]),
                            tiles < n_active[0])
    slot = (jnp.cumsum(first.astype(jnp.int32)) - 1) % 2
    later_first = jnp.concatenate([jnp.where(first, tiles, nt)[1:], jnp.full((1,), nt, jnp.int32)])
    next_first = lax.cummin(later_first, axis=0, reverse=True)
    next_expert = jnp.where(next_first < nt, tile_expert[jnp.minimum(next_first, nt - 1)], -1)
    any_spec = pl.BlockSpec(memory_space=pl.ANY)
    grid_spec = pltpu.PrefetchScalarGridSpec(
        num_scalar_prefetch=4,
        grid=(nt,),
        in_specs=_tile_idx_specs(te_rows, nt) + [any_spec] * 4,
        out_specs=pl.BlockSpec((te_rows * ROW_SLAB, HEAD_DIM), lambda t, *_: (t, 0)),
        scratch_shapes=[pltpu.VMEM((te_rows * ROW_SLAB, HEAD_DIM), F32),
                        pltpu.VMEM((te_rows * ROW_SLAB, HEAD_DIM), F32),
                        pltpu.SemaphoreType.DMA((2,)),
                        pltpu.VMEM((2, d, ff), F32), pltpu.VMEM((2, d, ff), F32), pltpu.VMEM((2, ff, d), F32),
                        pltpu.SemaphoreType.DMA((2,)),
                        pltpu.VMEM((d, ff), BF16), pltpu.VMEM((d, ff), BF16), pltpu.VMEM((ff, d), BF16)],
    )
    idx = src_tok.reshape(nt, 1, te_rows)
    return pl.pallas_call(
        functools.partial(_experts_body, te_rows=te_rows),
        grid_spec=grid_spec,
        out_shape=jax.ShapeDtypeStruct((nt * te_rows * ROW_SLAB, HEAD_DIM), F32),
        compiler_params=_params("arbitrary"),
        name="experts",
    )(tile_expert, n_active, slot.astype(jnp.int32), next_expert.astype(jnp.int32),
      idx, idx, xs, w_gate, w_up, w_down)


def _moe_out_body(idx_ref, idx_next_ref, x_ref, route_ref, ys_ref, gf_ref, yp_ref, ysm_ref, db0, db1, sem,
                  *, n_prompt_tiles):
    i = pl.program_id(0)
    nt = pl.num_programs(0)
    tm = x_ref.shape[0]
    bufs = (db0, db1)

    @pl.when(i == 0)
    def _():
        _gather_rows(idx_ref, ys_ref, db0, sem.at[0], 2 * tm)

    for s in range(2):
        @pl.when(jnp.logical_and(i + 1 < nt, (i + 1) % 2 == s))
        def _(s=s):
            _gather_rows(idx_next_ref, ys_ref, bufs[s], sem.at[s], 2 * tm)

    route = route_ref[...]
    w1, w2 = route[:, 2:3], route[:, 3:4]
    for s in range(2):
        @pl.when(i % 2 == s)
        def _(s=s):
            _wait_rows(ys_ref, bufs[s], sem.at[s])
            x = x_ref[...] + w1 * _slab_to_rows(bufs[s], tm) + w2 * _slab_to_rows(bufs[s], tm, first=tm)
            ms = jnp.mean(x * x, axis=-1, keepdims=True)
            y = x * lax.rsqrt(ms + EPS) * gf_ref[...]

            @pl.when(i < n_prompt_tiles)
            def _():
                yp_ref[...] = y

            @pl.when(i >= n_prompt_tiles)
            def _():
                ysm_ref[...] = y


def _moe_out(x, route, ys_sorted, pos1, pos2, g_final, n_prompt, tm):
    m, d = x.shape
    nt = m // tm
    npt = n_prompt // tm
    lanes = route.shape[1]
    idx = jnp.concatenate([pos1.reshape(nt, tm), pos2.reshape(nt, tm)], axis=1).reshape(nt, 1, 2 * tm)
    return pl.pallas_call(
        functools.partial(_moe_out_body, n_prompt_tiles=npt),
        grid=(nt,),
        in_specs=_tile_idx_specs(2 * tm, nt) + [
            pl.BlockSpec((tm, d), lambda i: (i, 0)), pl.BlockSpec((tm, lanes), lambda i: (i, 0)),
            pl.BlockSpec(memory_space=pl.ANY), pl.BlockSpec((1, d), lambda i: (0, 0))],
        out_specs=[pl.BlockSpec((tm, d), lambda i: (jnp.minimum(i, npt - 1), 0)),
                   pl.BlockSpec((tm, d), lambda i: (jnp.maximum(i - npt, 0), 0))],
        out_shape=[jax.ShapeDtypeStruct((n_prompt, d), F32), jax.ShapeDtypeStruct((m - n_prompt, d), F32)],
        scratch_shapes=[pltpu.VMEM((2 * tm * ROW_SLAB, HEAD_DIM), F32),
                        pltpu.VMEM((2 * tm * ROW_SLAB, HEAD_DIM), F32),
                        pltpu.SemaphoreType.DMA((2,))],
        compiler_params=_params("arbitrary"),
        name="moe_out",
    )(idx, idx, x, route, ys_sorted, g_final.reshape(1, d))


def _moe(x, g_ffn, w_rg, b_rg, w_re, b_re, w_gate, w_up, w_down, g_final, n_prompt):
    m, d = x.shape
    lanes = HEAD_DIM
    npad = lanes - N_EXPERTS - N_GROUPS
    w_r = jnp.concatenate([w_re, w_rg, jnp.zeros((d, npad), F32)], axis=1)
    b_r = jnp.concatenate([b_re, b_rg, jnp.zeros((npad,), F32)]).reshape(1, lanes)
    w_r_hi = w_r.astype(BF16)
    w_r_lo = (w_r - w_r_hi.astype(F32)).astype(BF16)
    xs, route, counts = _router(x, g_ffn, w_r_hi, w_r_lo, b_r, 512)

    te = MOE_TILE
    n_tiles = -(-(2 * m + N_EXPERTS * (te - 1)) // te)
    n_tiles += n_tiles % 2
    counts = counts[0, :N_EXPERTS].astype(jnp.int32)
    padded = (counts + te - 1) // te * te
    pad_end = jnp.cumsum(padded)
    pad_off = pad_end - padded
    experts = jnp.arange(N_EXPERTS, dtype=jnp.int32)

    def dest(col_e, col_r):
        e = route[:, col_e].astype(jnp.int32)
        off = jnp.sum(jnp.where(e[:, None] == experts[None, :], pad_off[None, :], 0), axis=1)
        return off + route[:, col_r].astype(jnp.int32)

    pos1, pos2 = dest(0, 4), dest(1, 5)
    tok = jnp.tile(jnp.arange(m, dtype=jnp.int32), 2)
    src_tok = jnp.zeros((n_tiles * te,), jnp.int32).at[jnp.concatenate([pos1, pos2])].set(tok)
    tile_start = jnp.arange(n_tiles, dtype=jnp.int32) * te
    tile_expert = jnp.minimum(jnp.sum(tile_start[:, None] >= pad_end[None, :], axis=1), N_EXPERTS - 1)
    n_active = (pad_end[-1] // te).reshape(1)

    ys_sorted = _experts(xs, src_tok, tile_expert.astype(jnp.int32), n_active.astype(jnp.int32),
                         w_gate, w_up, w_down, te)
    return _moe_out(x, route, ys_sorted, pos1, pos2, g_final, n_prompt, 256)


def _rope_tables(positions):
    half = HEAD_DIM // 2
    inv_freq = ROPE_THETA ** (-jnp.arange(half, dtype=F32) / half)
    ang = positions.astype(F32)[:, None] * inv_freq[None, :]
    cos, sin = jnp.cos(ang), jnp.sin(ang)
    return jnp.concatenate([cos, cos], axis=1), jnp.concatenate([-sin, sin], axis=1)


def kernel(x_prompt, x_sample, cache_swa1, cache_swa2, cache_swa3, state_hgrn, cache_mem_kv, mem_prompt,
           hgrn_lb_logits, norm_mix, w_in, w_proj_attn, w_proj_hgrn, w_out, hgrn_norm, norm_cross, norm_mem,
           w_cq, w_ckv, w_co, norm_ffn, w_rg, b_rg, w_re, b_re, w_e_gate, w_e_up, w_e_down, norm_final):
    bp, seq, d = x_prompt.shape
    bs, dseq, _ = x_sample.shape
    depth = w_in.shape[0]
    assert depth == 1, "single-layer trunk"
    past = cache_swa3.shape[2]
    mem_len = mem_prompt.shape[1]
    np_, ns = bp * seq, bs * dseq
    hw = HGRN_HEADS * HEAD_DIM
    col_hgrn = 3 * ATT_WIDTH
    col_ga = col_hgrn + 4 * hw
    col_gh = col_ga + d
    l = 0

    x_parts = (x_prompt.reshape(np_, d), x_sample.reshape(ns, d))
    pos = jnp.concatenate([jnp.tile(jnp.arange(seq, dtype=jnp.int32), bp),
                           jnp.tile(past + jnp.arange(dseq, dtype=jnp.int32), bs)])
    cos2, sin2 = _rope_tables(pos)

    xn = _rmsnorm2(*x_parts, norm_mix[l], BF16, 512)
    proj = _matmul(xn, w_in[l], F32, 768, ATT_WIDTH, rope=(cos2, sin2), name="proj_in")

    attn = _attn_prompt(proj, bp, seq, 512)
    attn = _attn_sample(proj, (cache_swa1[l], cache_swa2[l], cache_swa3[l]), attn, np_, bs, dseq)

    hgrn, st_p = _hgrn(proj, col_hgrn, hgrn_lb_logits, hgrn_norm[l], None, None, 0, bp, seq, 128, 8)
    hgrn, st_s = _hgrn(proj, col_hgrn, hgrn_lb_logits, hgrn_norm[l], state_hgrn[l], hgrn, np_, bs, dseq, dseq, 4)

    merged = _gated_merge(attn, hgrn, w_proj_attn[l], w_proj_hgrn[l], proj, col_ga, col_gh, 512, 1024, 512)
    x1 = _matmul(merged, w_out[l], F32, 512, 1024, res=x_parts, name="proj_out")

    mem_n = _rmsnorm(mem_prompt.reshape(bp * mem_len, d), norm_mem[l], BF16, 256)
    mkv_p = _matmul(mem_n, w_ckv[l], F32, 256, KV_ROW, name="mem_kv")
    qc = _matmul(x1, w_cq[l], F32, 512, ATT_OUT, norm=norm_cross[l], name="cross_q")
    oc = _cross_attn(qc, mkv_p.reshape(bp, mem_len, KV_ROW), None, 0, bp, seq, 512, 1)
    oc = _cross_attn(qc, cache_mem_kv[l].reshape(bs, mem_len, KV_SLAB, HEAD_DIM), oc, np_, bs, dseq, dseq, 4)
    x2 = _matmul(oc, w_co[l], F32, 512, 1024, res=x1, name="cross_out")

    ff = w_e_gate.shape[-1]
    y_p, y_s = _moe(x2, norm_ffn[l], w_rg[l], b_rg[l], w_re[l], b_re[l],
                    w_e_gate[l].reshape(N_EXPERTS, d, ff), w_e_up[l].reshape(N_EXPERTS, d, ff),
                    w_e_down[l].reshape(N_EXPERTS, ff, d), norm_final, np_)

    swa_p = [_kv_rows(proj, g, 0, bp, seq, min(w, seq), min(w, seq, 256))[None]
             for g, (w, _) in enumerate(ATT_GROUPS)]
    swa_s = [_kv_rows(proj, g, np_, 1, ns, ns, 256).reshape(1, bs, dseq, 2, ATT_HEADS, HEAD_DIM)
             for g in range(len(ATT_GROUPS))]
    return (y_p.reshape(bp, seq, d), y_s.reshape(bs, dseq, d),
            swa_p[0], swa_p[1], swa_p[2], st_p[None],
            mkv_p.reshape(1, bp, mem_len, 2, ATT_HEADS, HEAD_DIM),
            swa_s[0], swa_s[1], swa_s[2], st_s[None])
```

```python
import functools

import jax
import jax.numpy as jnp
from jax import lax
from jax.experimental import pallas as pl
from jax.experimental.pallas import tpu as pltpu

F32 = jnp.float32
BF16 = jnp.bfloat16

ATT_GROUPS = ((128, 1), (512, 4), (2048, 16))
ATT_HEADS = 4
HEAD_DIM = 128
ATT_OUT = ATT_HEADS * HEAD_DIM
ATT_WIDTH = len(ATT_GROUPS) * ATT_OUT
KV_ROW = 2 * ATT_OUT
KV_SLAB = 2 * ATT_HEADS
HGRN_HEADS = 16
ROPE_THETA = 10000.0
EPS = 1e-6
N_EXPERTS = 32
N_GROUPS = 4
EXPERTS_PER_GROUP = 8
NEG = -1e30
ATT_BLOCK = 128
HGRN_SUB = 8
ROW_SLAB = 16
MOE_TILE = 128
VMEM_LIMIT = 56 * 1024 * 1024


def _params(*sem):
    return pltpu.CompilerParams(dimension_semantics=sem, vmem_limit_bytes=VMEM_LIMIT)


def _dot(a, b):
    return jnp.dot(a, b, preferred_element_type=F32)


def _dot_nt(a, b):
    return lax.dot_general(a, b, (((1,), (1,)), ((), ())), preferred_element_type=F32)


def _sigmoid(x):
    return 1.0 / (1.0 + jnp.exp(-x))


def _silu(x):
    return x * _sigmoid(x)


def _rmsnorm_body(x_ref, g_ref, o_ref):
    x = x_ref[...]
    ms = jnp.mean(x * x, axis=-1, keepdims=True)
    o_ref[...] = (x * lax.rsqrt(ms + EPS) * g_ref[...]).astype(o_ref.dtype)


def _rmsnorm(x, g, out_dtype, tm):
    m, d = x.shape
    return pl.pallas_call(
        _rmsnorm_body,
        grid=(m // tm,),
        in_specs=[pl.BlockSpec((tm, d), lambda i: (i, 0)), pl.BlockSpec((1, d), lambda i: (0, 0))],
        out_specs=pl.BlockSpec((tm, d), lambda i: (i, 0)),
        out_shape=jax.ShapeDtypeStruct((m, d), out_dtype),
        compiler_params=_params("arbitrary"),
        name="rmsnorm",
    )(x, g.reshape(1, d))


def _rmsnorm2_body(xa_ref, xb_ref, g_ref, o_ref, *, na_tiles):
    i = pl.program_id(0)
    for ref, cond in ((xa_ref, i < na_tiles), (xb_ref, i >= na_tiles)):
        @pl.when(cond)
        def _(ref=ref):
            x = ref[...]
            ms = jnp.mean(x * x, axis=-1, keepdims=True)
            o_ref[...] = (x * lax.rsqrt(ms + EPS) * g_ref[...]).astype(o_ref.dtype)


def _two_part_specs(tm, width, na_tiles, index=lambda i: i, col=lambda i: 0):
    return [pl.BlockSpec((tm, width), lambda *a: (jnp.minimum(index(*a), na_tiles - 1), col(*a))),
            pl.BlockSpec((tm, width), lambda *a: (jnp.maximum(index(*a) - na_tiles, 0), col(*a)))]


def _rmsnorm2(xa, xb, g, out_dtype, tm):
    (ma, d), mb = xa.shape, xb.shape[0]
    na_tiles = ma // tm
    return pl.pallas_call(
        functools.partial(_rmsnorm2_body, na_tiles=na_tiles),
        grid=((ma + mb) // tm,),
        in_specs=_two_part_specs(tm, d, na_tiles) + [pl.BlockSpec((1, d), lambda i: (0, 0))],
        out_specs=pl.BlockSpec((tm, d), lambda i: (i, 0)),
        out_shape=jax.ShapeDtypeStruct((ma + mb, d), out_dtype),
        compiler_params=_params("arbitrary"),
        name="rmsnorm2",
    )(xa, xb, g.reshape(1, d))


def _mm_body(*refs, has_norm, has_res, has_rope, res_a_tiles):
    refs = list(refs)
    a_ref, w_ref = refs[:2]
    o_ref, wb_ref = refs[-2:]
    extra = refs[2:-2]
    j = pl.program_id(0)

    @pl.when(pl.program_id(1) == 0)
    def _():
        wb_ref[...] = w_ref[...].astype(BF16)

    a = a_ref[...]
    if has_norm:
        g_ref = extra.pop(0)
        ms = jnp.mean(a * a, axis=-1, keepdims=True)
        a = a * lax.rsqrt(ms + EPS) * g_ref[...]
    a = a.astype(BF16)
    if not has_rope:
        acc = _dot(a, wb_ref[...])
        if has_res and res_a_tiles is None:
            acc = acc + extra[0][...]
        if has_res and res_a_tiles is not None:
            i = pl.program_id(1)
            for r_ref, cond in ((extra[0], i < res_a_tiles), (extra[1], i >= res_a_tiles)):
                @pl.when(cond)
                def _(r_ref=r_ref):
                    o_ref[...] = (acc + r_ref[...]).astype(o_ref.dtype)
            return
        o_ref[...] = acc.astype(o_ref.dtype)
        return

    cos_ref, sin_ref = extra[-2:]

    @pl.when(j >= 2)
    def _():
        o_ref[...] = _dot(a, wb_ref[...]).astype(o_ref.dtype)

    @pl.when(j < 2)
    def _():
        cos = cos_ref[...]
        sin = sin_ref[...]
        scale = jnp.where(j == 0, HEAD_DIM ** -0.5, 1.0)
        for p in range(0, ATT_WIDTH, 2 * HEAD_DIM):
            acc = _dot(a, wb_ref[:, p:p + 2 * HEAD_DIM])
            for u in range(2):
                x = acc[:, u * HEAD_DIM:(u + 1) * HEAD_DIM]
                sl = slice(p + u * HEAD_DIM, p + (u + 1) * HEAD_DIM)
                o_ref[:, sl] = ((x * cos + pltpu.roll(x, HEAD_DIM // 2, axis=1) * sin) * scale).astype(o_ref.dtype)


def _matmul(a, w, out_dtype, tm, tn, norm=None, res=None, rope=None, name="matmul"):
    m, k = a.shape
    n = w.shape[1]
    in_specs = [pl.BlockSpec((tm, k), lambda j, i: (i, 0)), pl.BlockSpec((k, tn), lambda j, i: (0, j))]
    args = [a, w]
    if norm is not None:
        in_specs.append(pl.BlockSpec((1, k), lambda j, i: (0, 0)))
        args.append(norm.reshape(1, k))
    res_a_tiles = None
    if isinstance(res, tuple):
        res_a_tiles = res[0].shape[0] // tm
        in_specs.extend(_two_part_specs(tm, tn, res_a_tiles, index=lambda j, i: i, col=lambda j, i: j))
        args.extend(res)
    elif res is not None:
        in_specs.append(pl.BlockSpec((tm, tn), lambda j, i: (i, j)))
        args.append(res)
    if rope is not None:
        assert tn == ATT_WIDTH
        in_specs.extend([pl.BlockSpec((tm, HEAD_DIM), lambda j, i: (i, 0))] * 2)
        args.extend(rope)
    return pl.pallas_call(
        functools.partial(_mm_body, has_norm=norm is not None, has_res=res is not None,
                          has_rope=rope is not None, res_a_tiles=res_a_tiles),
        grid=(n // tn, m // tm),
        in_specs=in_specs,
        out_specs=pl.BlockSpec((tm, tn), lambda j, i: (i, j)),
        out_shape=jax.ShapeDtypeStruct((m, n), out_dtype),
        scratch_shapes=[pltpu.VMEM((k, tn), BF16)],
        compiler_params=_params("arbitrary", "arbitrary"),
        name=name,
    )(*args)


def _prompt_key_blocks(bq):
    table = []
    for g, (w, _) in enumerate(ATT_GROUPS):
        nback = -(-w // bq)
        table.extend((g, back) for back in range(nback, -1, -1))
    return tuple(table)


def _attn_prompt_body(q_ref, k_ref, v_ref, o_ref, m_scr, l_scr, acc_scr, *, bq, table):
    i = pl.program_id(1)
    j = pl.program_id(2)
    ng = len(ATT_GROUPS)

    @pl.when(j == 0)
    def _():
        m_scr[...] = jnp.full(m_scr.shape, NEG, F32)
        l_scr[...] = jnp.zeros(l_scr.shape, F32)
        acc_scr[...] = jnp.zeros(acc_scr.shape, F32)

    row = lax.broadcasted_iota(jnp.int32, (bq, bq), 0)
    col = lax.broadcasted_iota(jnp.int32, (bq, bq), 1)
    for g, (w, d) in enumerate(ATT_GROUPS):
        j0 = min(jj for jj, (gg, _) in enumerate(table) if gg == g)
        nback = max(back for gg, back in table if gg == g)
        back = nback - (j - j0)

        @pl.when(jnp.logical_and(jnp.logical_and(j >= j0, j <= j0 + nback), i >= back))
        def _(g=g, w=w, d=d, back=back):
            dist = back * bq + row - col
            valid = jnp.logical_and(jnp.logical_and(dist >= 0, dist <= w), (dist & (d - 1)) == 0)
            bias = jnp.where(valid, 0.0, NEG)
            heads = range(ATT_HEADS)
            hs = [slice(h * HEAD_DIM, (h + 1) * HEAD_DIM) for h in heads]
            m_old = [m_scr[g * ATT_HEADS + h] for h in heads]
            l_old = [l_scr[g * ATT_HEADS + h] for h in heads]
            a_old = [acc_scr[g, :, hs[h]] for h in heads]
            qs = [q_ref[:, g * ATT_OUT + h * HEAD_DIM: g * ATT_OUT + (h + 1) * HEAD_DIM].astype(BF16) for h in heads]
            ss = [_dot_nt(qs[h], k_ref[:, hs[h]].astype(BF16)) + bias for h in heads]
            m_new = [jnp.maximum(m_old[h], jnp.max(ss[h], axis=1, keepdims=True)) for h in heads]
            ps = [jnp.exp(ss[h] - jnp.concatenate([m_new[h]] * (bq // HEAD_DIM), axis=1)) for h in heads]
            alpha = [jnp.exp(m_old[h] - m_new[h]) for h in heads]
            pv = [_dot(ps[h].astype(BF16), v_ref[:, hs[h]].astype(BF16)) for h in heads]
            for h in heads:
                m_scr[g * ATT_HEADS + h] = m_new[h]
                l_scr[g * ATT_HEADS + h] = alpha[h] * l_old[h] + jnp.sum(ps[h], axis=1, keepdims=True)
                acc_scr[g, :, hs[h]] = alpha[h] * a_old[h] + pv[h]

    @pl.when(j == pl.num_programs(2) - 1)
    def _():
        for h in range(ATT_HEADS):
            sl = slice(h * HEAD_DIM, (h + 1) * HEAD_DIM)
            lse = [m_scr[g * ATT_HEADS + h] + jnp.log(l_scr[g * ATT_HEADS + h]) for g in range(ng)]
            mx = functools.reduce(jnp.maximum, lse)
            ws = [jnp.exp(x - mx) for x in lse]
            num = sum(ws[g] / l_scr[g * ATT_HEADS + h] * acc_scr[g, :, sl] for g in range(ng))
            o_ref[:, sl] = num / sum(ws)


def _attn_prompt(proj, batch, seq, bq):
    n = proj.shape[0]
    ng = len(ATT_GROUPS)
    nq = seq // bq
    table = _prompt_key_blocks(bq)
    groups = jnp.asarray([g for g, _ in table], jnp.int32)
    backs = jnp.asarray([b for _, b in table], jnp.int32)

    def kv_map(part):
        def index(b, i, j, g_ref, back_ref):
            return (b * nq + jnp.maximum(i - back_ref[j], 0), part * ng + g_ref[j])
        return index

    grid_spec = pltpu.PrefetchScalarGridSpec(
        num_scalar_prefetch=2,
        grid=(batch, nq, len(table)),
        in_specs=[pl.BlockSpec((bq, ATT_WIDTH), lambda b, i, j, g_ref, back_ref: (b * nq + i, 0)),
                  pl.BlockSpec((bq, ATT_OUT), kv_map(1)),
                  pl.BlockSpec((bq, ATT_OUT), kv_map(2))],
        out_specs=pl.BlockSpec((bq, ATT_OUT), lambda b, i, j, g_ref, back_ref: (b * nq + i, 0)),
        scratch_shapes=[pltpu.VMEM((ng * ATT_HEADS, bq, HEAD_DIM), F32),
                        pltpu.VMEM((ng * ATT_HEADS, bq, HEAD_DIM), F32),
                        pltpu.VMEM((ng, bq, ATT_OUT), F32)],
    )

    def body(g_ref, back_ref, *refs):
        del g_ref, back_ref
        _attn_prompt_body(*refs, bq=bq, table=table)

    return pl.pallas_call(
        body,
        grid_spec=grid_spec,
        out_shape=jax.ShapeDtypeStruct((n, ATT_OUT), F32),
        compiler_params=_params("arbitrary", "arbitrary", "arbitrary"),
        name="attn_prompt",
    )(groups, backs, proj, proj, proj)


def _head_lanes(ref2d, first, rows):
    return jnp.concatenate([ref2d[pl.ds(first + h, rows, stride=KV_SLAB), :] for h in range(ATT_HEADS)], axis=1)


def _attn_sample_body(q_ref, kn_ref, vn_ref, *rest, tq):
    cache_refs, o_ref = rest[:-2], rest[-1]
    nrow = ATT_HEADS * tq
    nkey = ATT_BLOCK
    rid = lax.broadcasted_iota(jnp.int32, (nrow, 1), 0)
    i_row = rid % tq
    head_row = rid // tq
    lane_head = lax.broadcasted_iota(jnp.int32, (1, ATT_OUT), 1) // HEAD_DIM
    head_mask = head_row == lane_head
    key = lax.broadcasted_iota(jnp.int32, (1, nkey), 1)
    pad = jnp.zeros((nkey - tq, ATT_OUT), F32)
    outs, lses = [], []
    ci = 0
    for g, (_, d) in enumerate(ATT_GROUPS):
        qg = q_ref[:, g * ATT_OUT:(g + 1) * ATT_OUT]
        qrows = jnp.where(head_mask, jnp.concatenate([qg] * ATT_HEADS, axis=0), 0.0).astype(BF16)
        k_new = jnp.concatenate([kn_ref[:, g * ATT_OUT:(g + 1) * ATT_OUT], pad], axis=0)
        v_new = jnp.concatenate([vn_ref[:, g * ATT_OUT:(g + 1) * ATT_OUT], pad], axis=0)
        valid_new = jnp.logical_and(jnp.logical_and(key < tq, key <= i_row), (key % d) == (i_row % d))
        blocks = [(k_new, v_new, valid_new)]
        for r in range(min(d, tq)):
            c2 = cache_refs[ci].reshape(nkey * KV_SLAB, HEAD_DIM)
            ci += 1
            valid = jnp.logical_and((i_row % d) == r, key >= i_row // d)
            blocks.append((_head_lanes(c2, 0, nkey), _head_lanes(c2, ATT_HEADS, nkey), valid))
        ss = [jnp.where(valid, _dot_nt(qrows, kb.astype(BF16)), NEG) for kb, _, valid in blocks]
        m = functools.reduce(jnp.maximum, [jnp.max(s, axis=1, keepdims=True) for s in ss])
        ps = [jnp.exp(s - m) for s in ss]
        l = sum(jnp.sum(p, axis=1, keepdims=True) for p in ps)
        acc = sum(_dot(p.astype(BF16), vb.astype(BF16)) for p, (_, vb, _) in zip(ps, blocks))
        o = acc / l
        lse = m + jnp.log(l)
        outs.append(jnp.concatenate(
            [o[h * tq:(h + 1) * tq, h * HEAD_DIM:(h + 1) * HEAD_DIM] for h in range(ATT_HEADS)], axis=1))
        lses.append(jnp.concatenate(
            [jnp.broadcast_to(lse[h * tq:(h + 1) * tq], (tq, HEAD_DIM)) for h in range(ATT_HEADS)], axis=1))
    mx = jnp.maximum(jnp.maximum(lses[0], lses[1]), lses[2])
    ws = [jnp.exp(x - mx) for x in lses]
    o_ref[...] = (ws[0] * outs[0] + ws[1] * outs[1] + ws[2] * outs[2]) / (ws[0] + ws[1] + ws[2])


def _attn_sample(proj, caches, attn_buf, row0, batch, tq):
    blk0 = row0 // tq
    views, specs = [], []
    for (w, d), c in zip(ATT_GROUPS, caches):
        assert c.shape[1] == w and w // d == ATT_BLOCK, "window buffers must hold exactly one window"
        view = c.reshape(batch, w // d, d, KV_SLAB, HEAD_DIM)
        for r in range(min(d, tq)):
            views.append(view)
            specs.append(pl.BlockSpec((None, w // d, None, KV_SLAB, HEAD_DIM), lambda b, r=r: (b, 0, r, 0, 0)))
    n_in = 3 + len(views)
    qkv = [pl.BlockSpec((tq, ATT_WIDTH), lambda b, part=part: (blk0 + b, part)) for part in range(3)]
    return pl.pallas_call(
        functools.partial(_attn_sample_body, tq=tq),
        grid=(batch,),
        in_specs=qkv + specs + [pl.BlockSpec(memory_space=pl.ANY)],
        out_specs=pl.BlockSpec((tq, ATT_OUT), lambda b: (blk0 + b, 0)),
        out_shape=jax.ShapeDtypeStruct(attn_buf.shape, F32),
        input_output_aliases={n_in: 0},
        compiler_params=_params("arbitrary"),
        name="attn_sample",
    )(proj, proj, proj, *views, attn_buf)


def _kv_rows_body(k_ref, v_ref, o_ref):
    tm = k_ref.shape[0]
    for h in range(ATT_HEADS):
        sl = slice(h * HEAD_DIM, (h + 1) * HEAD_DIM)
        o_ref[pl.ds(h, tm, stride=KV_SLAB), :] = k_ref[:, sl]
        o_ref[pl.ds(ATT_HEADS + h, tm, stride=KV_SLAB), :] = v_ref[:, sl]


def _kv_rows(proj, g, row0, batch, seq, keep, tm):
    ng = len(ATT_GROUPS)
    nt = keep // tm
    rb0 = (row0 + seq - keep) // tm
    per_seq = seq // tm
    out = pl.pallas_call(
        _kv_rows_body,
        grid=(batch, nt),
        in_specs=[pl.BlockSpec((tm, ATT_OUT), lambda b, t: (rb0 + b * per_seq + t, ng + g)),
                  pl.BlockSpec((tm, ATT_OUT), lambda b, t: (rb0 + b * per_seq + t, 2 * ng + g))],
        out_specs=pl.BlockSpec((tm * KV_SLAB, HEAD_DIM), lambda b, t: (b * nt + t, 0)),
        out_shape=jax.ShapeDtypeStruct((batch * keep * KV_SLAB, HEAD_DIM), F32),
        compiler_params=_params("arbitrary", "arbitrary"),
        name=f"kv_rows_g{g}_{tm}",
    )(proj, proj)
    return out.reshape(batch, keep, 2, ATT_HEADS, HEAD_DIM)


def _cumsum_rows(x, seg):
    row = lax.broadcasted_iota(jnp.int32, (x.shape[0], 1), 0) % seg
    sh = 1
    while sh < seg:
        x = x + jnp.where(row >= sh, pltpu.roll(x, sh, axis=0), 0.0)
        sh *= 2
    return x


def _bcast_rows(b, first, period):
    c, w = b.shape
    parts = [jnp.broadcast_to(b[p * period + first: p * period + first + 1, :], (period, w))
             for p in range(c // period)]
    return parts[0] if len(parts) == 1 else jnp.concatenate(parts, axis=0)


def _hgrn_body(*refs, chunk, nseq, has_s0, chained):
    refs = list(refs)
    qh_ref, fh_ref, ih_ref, og_ref, lbl_ref, gn_ref = refs[:6]
    s0_ref = refs[6] if has_s0 else None
    o_ref, so_ref, st_ref = refs[-3:]
    c = chunk
    nh = ATT_HEADS
    t = pl.program_id(2)

    if chained:
        @pl.when(t == 0)
        def _():
            for h in range(nh):
                st_ref[h] = s0_ref[0, h].T if has_s0 else jnp.zeros((HEAD_DIM, HEAD_DIM), F32)

    lbl = lbl_ref[...]
    e = jnp.exp(lbl - jnp.max(lbl, axis=0, keepdims=True))
    lb = e[0:1, :] / jnp.sum(e, axis=0, keepdims=True)
    f = lb + (1.0 - lb) * _sigmoid(fh_ref[...])
    kk = 1.0 - f
    qf = _silu(qh_ref[...])
    b = _cumsum_rows(jnp.log(f), c)
    vv = ih_ref[...]

    row = lax.broadcasted_iota(jnp.int32, (c, c), 0)
    col = lax.broadcasted_iota(jnp.int32, (c, c), 1)
    rid = lax.broadcasted_iota(jnp.int32, (nseq * c, 1), 0)
    seqs = [slice(q * c, (q + 1) * c) for q in range(nseq)]
    heads = [slice(h * HEAD_DIM, (h + 1) * HEAD_DIM) for h in range(nh)]

    a = [[jnp.zeros((c, c), F32) for _ in range(nh)] for _ in range(nseq)]
    s = c // 2
    while s >= HGRN_SUB:
        ref = _bcast_rows(b, s - 1, 2 * s)
        second = (rid % (2 * s)) >= s
        dlt = b - ref
        ee = jnp.exp(jnp.where(second, dlt, -dlt))
        ql = jnp.where(second, qf * ee, 0.0).astype(BF16)
        kl = jnp.where(second, 0.0, kk * ee).astype(BF16)
        same = (row // (2 * s)) == (col // (2 * s))
        for q in range(nseq):
            for h in range(nh):
                a[q][h] = a[q][h] + jnp.where(same, _dot_nt(ql[seqs[q], heads[h]], kl[seqs[q], heads[h]]), 0.0)
        s //= 2
    sub = min(HGRN_SUB, c)
    ref = _bcast_rows(b, 0, sub)
    dlt = b - ref
    qd = (qf * jnp.exp(dlt)).astype(BF16)
    kd = (kk * jnp.exp(jnp.minimum(-dlt, 80.0))).astype(BF16)
    diag = jnp.logical_and((row // sub) == (col // sub), col <= row)
    qe = (qf * jnp.exp(b)).astype(BF16)
    b_last = _bcast_rows(b, c - 1, c)
    kend = (kk * jnp.exp(b_last - b)).astype(BF16)
    dec = jnp.exp(b_last)
    gn = gn_ref[...]
    gate = _silu(og_ref[...])
    vb = vv.astype(BF16)
    eye = (lax.broadcasted_iota(jnp.int32, (HEAD_DIM, HEAD_DIM), 0)
           == lax.broadcasted_iota(jnp.int32, (HEAD_DIM, HEAD_DIM), 1))
    tn_dims = (((0,), (0,)), ((), ()))
    for h in range(nh):
        sl = heads[h]
        st = st_ref[h] if chained else None
        for q in range(nseq):
            rs = seqs[q]
            ah = a[q][h] + jnp.where(diag, _dot_nt(qd[rs, sl], kd[rs, sl]), 0.0)
            intra = _dot(ah.astype(BF16), vb[rs, sl])
            dec_q = dec[q * c:q * c + 1, sl]
            if chained:
                o = _dot_nt(qe[rs, sl], st.astype(BF16)) + intra
                st = st * dec_q + lax.dot_general(vb[rs, sl], kend[rs, sl], tn_dims, preferred_element_type=F32)
            else:
                s_kv = s0_ref[q, h]
                o = _dot(qe[rs, sl], s_kv.astype(BF16)) + intra
                dec_col = jnp.sum(jnp.where(eye, jnp.broadcast_to(dec_q, (HEAD_DIM, HEAD_DIM)), 0.0),
                                  axis=1, keepdims=True)
                so_ref[q, h] = s_kv * dec_col + lax.dot_general(kend[rs, sl], vb[rs, sl], tn_dims,
                                                                preferred_element_type=F32)
            ms = jnp.mean(o * o, axis=-1, keepdims=True)
            o_ref[rs, sl] = (o * lax.rsqrt(ms + EPS) * gn * gate[rs, sl]).astype(o_ref.dtype)
        if chained:
            st_ref[h] = st

            @pl.when(t == pl.num_programs(2) - 1)
            def _(h=h, st=st):
                so_ref[0, h] = st.T


def _hgrn(proj, col0, lb_logits, gnorm, s0, out_buf, row0, batch, seq, chunk, nseq):
    n = proj.shape[0]
    chained = seq != chunk
    assert chained or s0 is not None
    nt = seq // (chunk * nseq) if chained else 1
    nstate = 1 if chained else nseq
    hb = HGRN_HEADS // ATT_HEADS
    width = HGRN_HEADS * HEAD_DIM
    cb0 = col0 // ATT_OUT
    rows = nseq * chunk
    rb0 = row0 // rows

    def col(j):
        return pl.BlockSpec((rows, ATT_OUT), lambda b, h, t, j=j: (rb0 + b * nt + t, cb0 + j * hb + h))

    in_specs = [col(0), col(1), col(2), col(3),
                pl.BlockSpec((lb_logits.shape[0], ATT_OUT), lambda b, h, t: (0, h)),
                pl.BlockSpec((1, HEAD_DIM), lambda b, h, t: (0, 0))]
    args = [proj, proj, proj, proj, lb_logits, gnorm.reshape(1, HEAD_DIM)]
    state_spec = pl.BlockSpec((nstate, ATT_HEADS, HEAD_DIM, HEAD_DIM), lambda b, h, t: (b, h, 0, 0))
    if s0 is not None:
        in_specs.append(state_spec)
        args.append(s0)
    aliases = {}
    if out_buf is not None:
        aliases = {len(args): 0}
        in_specs.append(pl.BlockSpec(memory_space=pl.ANY))
        args.append(out_buf)
    return pl.pallas_call(
        functools.partial(_hgrn_body, chunk=chunk, nseq=nseq, has_s0=s0 is not None, chained=chained),
        grid=(batch // nstate, hb, nt),
        in_specs=in_specs,
        out_specs=[pl.BlockSpec((rows, ATT_OUT), lambda b, h, t: (rb0 + b * nt + t, h)), state_spec],
        out_shape=[jax.ShapeDtypeStruct((n, width), BF16),
                   jax.ShapeDtypeStruct((batch, HGRN_HEADS, HEAD_DIM, HEAD_DIM), F32)],
        scratch_shapes=[pltpu.VMEM((ATT_HEADS, HEAD_DIM, HEAD_DIM), F32)],
        input_output_aliases=aliases,
        compiler_params=_params("arbitrary", "arbitrary", "arbitrary"),
        name=f"hgrn_c{chunk}",
    )(*args)


def _gated_merge_body(att_ref, hg_ref, wa_ref, wh_ref, *rest, ngate):
    ga_refs, gh_refs = rest[:ngate], rest[ngate:2 * ngate]
    o_ref, wab_ref, whb_ref = rest[2 * ngate:]

    @pl.when(pl.program_id(1) == 0)
    def _():
        wab_ref[...] = wa_ref[...].astype(BF16)
        whb_ref[...] = wh_ref[...].astype(BF16)

    pa = _dot(att_ref[...].astype(BF16), wab_ref[...])
    ph = _dot(hg_ref[...].astype(BF16), whb_ref[...])
    ga = jnp.concatenate([r[...] for r in ga_refs], axis=1)
    gh = jnp.concatenate([r[...] for r in gh_refs], axis=1)
    o_ref[...] = (_sigmoid(ga) * pa + _sigmoid(gh) * ph).astype(o_ref.dtype)


def _gated_merge(attn, hgrn, w_pa, w_ph, proj, col_ga, col_gh, tm, tn, tg):
    m = attn.shape[0]
    n = w_pa.shape[1]
    ka, kh = w_pa.shape[0], w_ph.shape[0]
    assert col_ga % tg == 0 and col_gh % tg == 0 and tn % tg == 0, "gate columns must start on a gate block"
    ngate = tn // tg
    gate = lambda c0, u: pl.BlockSpec((tm, tg), lambda j, i: (i, c0 // tg + j * ngate + u))
    gates = [gate(col_ga, u) for u in range(ngate)] + [gate(col_gh, u) for u in range(ngate)]
    return pl.pallas_call(
        functools.partial(_gated_merge_body, ngate=ngate),
        grid=(n // tn, m // tm),
        in_specs=[pl.BlockSpec((tm, ka), lambda j, i: (i, 0)),
                  pl.BlockSpec((tm, kh), lambda j, i: (i, 0)),
                  pl.BlockSpec((ka, tn), lambda j, i: (0, j)),
                  pl.BlockSpec((kh, tn), lambda j, i: (0, j))] + gates,
        out_specs=pl.BlockSpec((tm, tn), lambda j, i: (i, j)),
        out_shape=jax.ShapeDtypeStruct((m, n), BF16),
        scratch_shapes=[pltpu.VMEM((ka, tn), BF16), pltpu.VMEM((kh, tn), BF16)],
        compiler_params=_params("arbitrary", "arbitrary"),
        name="gated_merge",
    )(attn, hgrn, w_pa, w_ph, *([proj] * (2 * ngate)))


def _cross_body(q_ref, kv_ref, *rest, slab_rows, nseq):
    o_ref = rest[-1]
    scale = HEAD_DIM ** -0.5
    tq = q_ref.shape[0] // nseq
    mem_len = kv_ref.shape[1]
    if slab_rows:
        nrow = ATT_HEADS * tq
        head_row = lax.broadcasted_iota(jnp.int32, (nrow, 1), 0) // tq
        lane_head = lax.broadcasted_iota(jnp.int32, (1, ATT_OUT), 1) // HEAD_DIM
        head_mask = head_row == lane_head
        for b in range(nseq):
            rs = slice(b * tq, (b + 1) * tq)
            kv = kv_ref.at[b].reshape(mem_len * KV_SLAB, HEAD_DIM)
            qb = q_ref[rs, :] * scale
            qrows = jnp.where(head_mask, jnp.concatenate([qb] * ATT_HEADS, axis=0), 0.0).astype(BF16)
            s = _dot_nt(qrows, _head_lanes(kv, 0, mem_len).astype(BF16))
            p = jnp.exp(s - jnp.max(s, axis=1, keepdims=True))
            o = _dot(p.astype(BF16), _head_lanes(kv, ATT_HEADS, mem_len).astype(BF16))
            o = o / jnp.sum(p, axis=1, keepdims=True)
            o_ref[rs, :] = jnp.concatenate(
                [o[h * tq:(h + 1) * tq, h * HEAD_DIM:(h + 1) * HEAD_DIM] for h in range(ATT_HEADS)], axis=1)
        return
    for b in range(nseq):
        rs = slice(b * tq, (b + 1) * tq)
        kv = kv_ref.at[b]
        for h in range(ATT_HEADS):
            sl = slice(h * HEAD_DIM, (h + 1) * HEAD_DIM)
            q = (q_ref[rs, sl] * scale).astype(BF16)
            k = kv[:, sl].astype(BF16)
            v = kv[:, ATT_OUT + h * HEAD_DIM: ATT_OUT + (h + 1) * HEAD_DIM].astype(BF16)
            s = _dot_nt(q, k)
            p = jnp.exp(s - jnp.max(s, axis=1, keepdims=True))
            o_ref[rs, sl] = _dot(p.astype(BF16), v) / jnp.sum(p, axis=1, keepdims=True)


def _cross_attn(q, mem_kv, out_buf, row0, batch, seq, tq, nseq):
    n = q.shape[0]
    nq = seq // tq
    assert nseq == 1 or nq == 1
    rows = nseq * tq
    rb0 = row0 // rows
    slab_rows = mem_kv.ndim == 4
    kv_spec = pl.BlockSpec((nseq,) + mem_kv.shape[1:], lambda b, i: (b,) + (0,) * (mem_kv.ndim - 1))
    in_specs = [pl.BlockSpec((rows, ATT_OUT), lambda b, i: (rb0 + b * nq + i, 0)), kv_spec]
    args = [q, mem_kv]
    aliases = {}
    if out_buf is not None:
        aliases = {2: 0}
        in_specs.append(pl.BlockSpec(memory_space=pl.ANY))
        args.append(out_buf)
    return pl.pallas_call(
        functools.partial(_cross_body, slab_rows=slab_rows, nseq=nseq),
        grid=(batch // nseq, nq),
        in_specs=in_specs,
        out_specs=pl.BlockSpec((rows, ATT_OUT), lambda b, i: (rb0 + b * nq + i, 0)),
        out_shape=jax.ShapeDtypeStruct((n, ATT_OUT), F32),
        input_output_aliases=aliases,
        compiler_params=_params("arbitrary", "arbitrary"),
        name=f"cross_attn_t{tq}",
    )(*args)


def _router_body(x_ref, g_ref, whi_ref, wlo_ref, b_ref, tri_ref, xs_ref, route_ref, cnt_ref, base_scr):
    tm = x_ref.shape[0]

    @pl.when(pl.program_id(0) == 0)
    def _():
        base_scr[...] = jnp.zeros_like(base_scr)

    x = x_ref[...]
    ms = jnp.mean(x * x, axis=-1, keepdims=True)
    xn = x * lax.rsqrt(ms + EPS) * g_ref[...]
    for c in range(ROW_SLAB):
        xs_ref[pl.ds(c, tm, stride=ROW_SLAB), :] = xn[:, c * HEAD_DIM:(c + 1) * HEAD_DIM]
    hi = xn.astype(BF16)
    lo = (xn - hi.astype(F32)).astype(BF16)
    logits = _dot(hi, whi_ref[...]) + _dot(lo, whi_ref[...]) + _dot(hi, wlo_ref[...]) + b_ref[...]
    lane = lax.broadcasted_iota(jnp.int32, logits.shape, 1)
    big = jnp.int32(1 << 20)
    is_g = jnp.logical_and(lane >= N_EXPERTS, lane < N_EXPERTS + N_GROUPS)
    lg = jnp.where(is_g, logits, NEG)
    mg = jnp.max(lg, axis=1, keepdims=True)
    p_top = 1.0 / jnp.sum(jnp.where(is_g, jnp.exp(lg - mg), 0.0), axis=1, keepdims=True)
    g_idx = jnp.min(jnp.where(jnp.logical_and(is_g, lg == mg), lane, big), axis=1, keepdims=True) - N_EXPERTS
    in_grp = jnp.logical_and(lane < N_EXPERTS, lane // EXPERTS_PER_GROUP == g_idx)
    le = jnp.where(in_grp, logits, NEG)
    v1 = jnp.max(le, axis=1, keepdims=True)
    i1 = jnp.min(jnp.where(jnp.logical_and(in_grp, le == v1), lane, big), axis=1, keepdims=True)
    rest = jnp.logical_and(in_grp, lane != i1)
    le2 = jnp.where(rest, logits, NEG)
    v2 = jnp.max(le2, axis=1, keepdims=True)
    i2 = jnp.min(jnp.where(jnp.logical_and(rest, le2 == v2), lane, big), axis=1, keepdims=True)
    e2 = jnp.exp(v2 - v1)
    w1 = p_top / (1.0 + e2)
    w2 = p_top * e2 / (1.0 + e2)
    hit1, hit2 = lane == i1, lane == i2
    hits = jnp.where(jnp.logical_or(hit1, hit2), 1.0, 0.0)
    before = _dot(tri_ref[...], hits.astype(BF16)) + base_scr[...]
    r1 = jnp.sum(jnp.where(hit1, before, 0.0), axis=1, keepdims=True)
    r2 = jnp.sum(jnp.where(hit2, before, 0.0), axis=1, keepdims=True)
    base_scr[...] += jnp.sum(hits, axis=0, keepdims=True)
    cnt_ref[...] = base_scr[...]
    cols = (i1.astype(F32), i2.astype(F32), w1, w2, r1, r2)
    route = jnp.zeros(logits.shape, F32)
    for c, val in enumerate(cols):
        route = jnp.where(lane == c, val, route)
    route_ref[...] = route


def _router(x, g, w_hi, w_lo, bias, tm):
    m, d = x.shape
    lanes = w_hi.shape[1]
    const = lambda shape: pl.BlockSpec(shape, lambda i: (0, 0))
    tri = jnp.tril(jnp.ones((tm, tm), F32), -1).astype(BF16)
    return pl.pallas_call(
        _router_body,
        grid=(m // tm,),
        in_specs=[pl.BlockSpec((tm, d), lambda i: (i, 0)), const((1, d)), const((d, lanes)), const((d, lanes)),
                  const((1, lanes)), const((tm, tm))],
        out_specs=[pl.BlockSpec((tm * ROW_SLAB, HEAD_DIM), lambda i: (i, 0)),
                   pl.BlockSpec((tm, lanes), lambda i: (i, 0)), const((1, lanes))],
        out_shape=[jax.ShapeDtypeStruct((m * ROW_SLAB, HEAD_DIM), F32), jax.ShapeDtypeStruct((m, lanes), F32),
                   jax.ShapeDtypeStruct((1, lanes), F32)],
        scratch_shapes=[pltpu.VMEM((1, lanes), F32)],
        compiler_params=_params("arbitrary"),
        name="router",
    )(x, g.reshape(1, d), w_hi, w_lo, bias, tri)


def _gather_rows(idx_ref, src_ref, dst_ref, sem, n):
    def issue(p, carry):
        r = pl.multiple_of(idx_ref[0, p] * ROW_SLAB, ROW_SLAB)
        o = pl.multiple_of(p * ROW_SLAB, ROW_SLAB)
        pltpu.make_async_copy(src_ref.at[pl.ds(r, ROW_SLAB), :], dst_ref.at[pl.ds(o, ROW_SLAB), :], sem).start()
        return carry

    lax.fori_loop(0, n, issue, 0, unroll=8)


def _wait_rows(src_ref, dst_ref, sem):
    pltpu.make_async_copy(src_ref.at[pl.ds(0, dst_ref.shape[0]), :], dst_ref, sem).wait()


def _slab_to_rows(ref, rows, first=0):
    return jnp.concatenate([ref[pl.ds(first * ROW_SLAB + c, rows, stride=ROW_SLAB), :] for c in range(ROW_SLAB)],
                           axis=1)


def _tile_idx_specs(tile, steps):
    def cur(t, *_):
        return (t, 0, 0)

    def nxt(t, *_):
        return (jnp.minimum(t + 1, steps - 1), 0, 0)

    return [pl.BlockSpec((None, 1, tile), cur, memory_space=pltpu.SMEM),
            pl.BlockSpec((None, 1, tile), nxt, memory_space=pltpu.SMEM)]


def _experts_body(te_ref, na_ref, slot_ref, next_ref, idx_ref, idx_next_ref, xs_ref, wg_ref, wu_ref, wd_ref, o_ref,
                  xb0, xb1, sem, wgf, wuf, wdf, wsem, wgb, wub, wdb, *, te_rows):
    t = pl.program_id(0)
    na = na_ref[0]
    active = t < na
    bufs = (xb0, xb1)

    def weight_copies(e, s):
        return [pltpu.make_async_copy(src.at[e], dst.at[s], wsem.at[s])
                for src, dst in ((wg_ref, wgf), (wu_ref, wuf), (wd_ref, wdf))]

    @pl.when(jnp.logical_and(t == 0, active))
    def _():
        for cp in weight_copies(te_ref[0], 0):
            cp.start()
        _gather_rows(idx_ref, xs_ref, xb0, sem.at[0], te_rows)

    for s in range(2):
        @pl.when(jnp.logical_and(t + 1 < na, (t + 1) % 2 == s))
        def _(s=s):
            _gather_rows(idx_next_ref, xs_ref, bufs[s], sem.at[s], te_rows)

    changed = jnp.logical_or(t == 0, te_ref[t] != te_ref[jnp.maximum(t - 1, 0)])
    for s in range(2):
        @pl.when(jnp.logical_and(jnp.logical_and(active, changed), slot_ref[t] == s))
        def _(s=s):
            for cp in weight_copies(te_ref[t], s):
                cp.wait()
            wgb[...] = wgf[s].astype(BF16)
            wub[...] = wuf[s].astype(BF16)
            wdb[...] = wdf[s].astype(BF16)

            @pl.when(next_ref[t] >= 0)
            def _():
                for cp in weight_copies(next_ref[t], 1 - s):
                    cp.start()

    for s in range(2):
        @pl.when(jnp.logical_and(active, t % 2 == s))
        def _(s=s):
            _wait_rows(xs_ref, bufs[s], sem.at[s])
            x = _slab_to_rows(bufs[s], te_rows).astype(BF16)
            h = (_silu(_dot(x, wgb[...])) * _dot(x, wub[...])).astype(BF16)
            for c in range(0, ROW_SLAB, 2):
                o = _dot(h, wdb[:, c * HEAD_DIM:(c + 2) * HEAD_DIM])
                o_ref[pl.ds(c, te_rows, stride=ROW_SLAB), :] = o[:, :HEAD_DIM]
                o_ref[pl.ds(c + 1, te_rows, stride=ROW_SLAB), :] = o[:, HEAD_DIM:]

    @pl.when(jnp.logical_not(active))
    def _():
        o_ref[...] = jnp.zeros_like(o_ref)


def _experts(xs, src_tok, tile_expert, n_active, w_gate, w_up, w_down, te_rows):
    ne, d, ff = w_gate.shape
    nt = tile_expert.shape[0]
    tiles = jnp.arange(nt, dtype=jnp.int32)
    first = jnp.logical_and(jnp.concatenate([jnp.ones((1,), bool), tile_expert[1:] != tile_expert[:-1]]),
                            tiles < n_active[0])
    slot = (jnp.cumsum(first.astype(jnp.int32)) - 1) % 2
    later_first = jnp.concatenate([jnp.where(first, tiles, nt)[1:], jnp.full((1,), nt, jnp.int32)])
    next_first = lax.cummin(later_first, axis=0, reverse=True)
    next_expert = jnp.where(next_first < nt, tile_expert[jnp.minimum(next_first, nt - 1)], -1)
    any_spec = pl.BlockSpec(memory_space=pl.ANY)
    grid_spec = pltpu.PrefetchScalarGridSpec(
        num_scalar_prefetch=4,
        grid=(nt,),
        in_specs=_tile_idx_specs(te_rows, nt) + [any_spec] * 4,
        out_specs=pl.BlockSpec((te_rows * ROW_SLAB, HEAD_DIM), lambda t, *_: (t, 0)),
        scratch_shapes=[pltpu.VMEM((te_rows * ROW_SLAB, HEAD_DIM), F32),
                        pltpu.VMEM((te_rows * ROW_SLAB, HEAD_DIM), F32),
                        pltpu.SemaphoreType.DMA((2,)),
                        pltpu.VMEM((2, d, ff), F32), pltpu.VMEM((2, d, ff), F32), pltpu.VMEM((2, ff, d), F32),
                        pltpu.SemaphoreType.DMA((2,)),
                        pltpu.VMEM((d, ff), BF16), pltpu.VMEM((d, ff), BF16), pltpu.VMEM((ff, d), BF16)],
    )
    idx = src_tok.reshape(nt, 1, te_rows)
    return pl.pallas_call(
        functools.partial(_experts_body, te_rows=te_rows),
        grid_spec=grid_spec,
        out_shape=jax.ShapeDtypeStruct((nt * te_rows * ROW_SLAB, HEAD_DIM), F32),
        compiler_params=_params("arbitrary"),
        name="experts",
    )(tile_expert, n_active, slot.astype(jnp.int32), next_expert.astype(jnp.int32),
      idx, idx, xs, w_gate, w_up, w_down)


def _moe_out_body(idx_ref, idx_next_ref, x_ref, route_ref, ys_ref, gf_ref, yp_ref, ysm_ref, db0, db1, sem,
                  *, n_prompt_tiles):
    i = pl.program_id(0)
    nt = pl.num_programs(0)
    tm = x_ref.shape[0]
    bufs = (db0, db1)

    @pl.when(i == 0)
    def _():
        _gather_rows(idx_ref, ys_ref, db0, sem.at[0], 2 * tm)

    for s in range(2):
        @pl.when(jnp.logical_and(i + 1 < nt, (i + 1) % 2 == s))
        def _(s=s):
            _gather_rows(idx_next_ref, ys_ref, bufs[s], sem.at[s], 2 * tm)

    route = route_ref[...]
    w1, w2 = route[:, 2:3], route[:, 3:4]
    for s in range(2):
        @pl.when(i % 2 == s)
        def _(s=s):
            _wait_rows(ys_ref, bufs[s], sem.at[s])
            x = x_ref[...] + w1 * _slab_to_rows(bufs[s], tm) + w2 * _slab_to_rows(bufs[s], tm, first=tm)
            ms = jnp.mean(x * x, axis=-1, keepdims=True)
            y = x * lax.rsqrt(ms + EPS) * gf_ref[...]

            @pl.when(i < n_prompt_tiles)
            def _():
                yp_ref[...] = y

            @pl.when(i >= n_prompt_tiles)
            def _():
                ysm_ref[...] = y


def _moe_out(x, route, ys_sorted, pos1, pos2, g_final, n_prompt, tm):
    m, d = x.shape
    nt = m // tm
    npt = n_prompt // tm
    lanes = route.shape[1]
    idx = jnp.concatenate([pos1.reshape(nt, tm), pos2.reshape(nt, tm)], axis=1).reshape(nt, 1, 2 * tm)
    return pl.pallas_call(
        functools.partial(_moe_out_body, n_prompt_tiles=npt),
        grid=(nt,),
        in_specs=_tile_idx_specs(2 * tm, nt) + [
            pl.BlockSpec((tm, d), lambda i: (i, 0)), pl.BlockSpec((tm, lanes), lambda i: (i, 0)),
            pl.BlockSpec(memory_space=pl.ANY), pl.BlockSpec((1, d), lambda i: (0, 0))],
        out_specs=[pl.BlockSpec((tm, d), lambda i: (jnp.minimum(i, npt - 1), 0)),
                   pl.BlockSpec((tm, d), lambda i: (jnp.maximum(i - npt, 0), 0))],
        out_shape=[jax.ShapeDtypeStruct((n_prompt, d), F32), jax.ShapeDtypeStruct((m - n_prompt, d), F32)],
        scratch_shapes=[pltpu.VMEM((2 * tm * ROW_SLAB, HEAD_DIM), F32),
                        pltpu.VMEM((2 * tm * ROW_SLAB, HEAD_DIM), F32),
                        pltpu.SemaphoreType.DMA((2,))],
        compiler_params=_params("arbitrary"),
        name="moe_out",
    )(idx, idx, x, route, ys_sorted, g_final.reshape(1, d))


def _moe(x, g_ffn, w_rg, b_rg, w_re, b_re, w_gate, w_up, w_down, g_final, n_prompt):
    m, d = x.shape
    lanes = HEAD_DIM
    npad = lanes - N_EXPERTS - N_GROUPS
    w_r = jnp.concatenate([w_re, w_rg, jnp.zeros((d, npad), F32)], axis=1)
    b_r = jnp.concatenate([b_re, b_rg, jnp.zeros((npad,), F32)]).reshape(1, lanes)
    w_r_hi = w_r.astype(BF16)
    w_r_lo = (w_r - w_r_hi.astype(F32)).astype(BF16)
    xs, route, counts = _router(x, g_ffn, w_r_hi, w_r_lo, b_r, 512)

    te = MOE_TILE
    n_tiles = -(-(2 * m + N_EXPERTS * (te - 1)) // te)
    n_tiles += n_tiles % 2
    counts = counts[0, :N_EXPERTS].astype(jnp.int32)
    padded = (counts + te - 1) // te * te
    pad_end = jnp.cumsum(padded)
    pad_off = pad_end - padded
    experts = jnp.arange(N_EXPERTS, dtype=jnp.int32)

    def dest(col_e, col_r):
        e = route[:, col_e].astype(jnp.int32)
        off = jnp.sum(jnp.where(e[:, None] == experts[None, :], pad_off[None, :], 0), axis=1)
        return off + route[:, col_r].astype(jnp.int32)

    pos1, pos2 = dest(0, 4), dest(1, 5)
    tok = jnp.tile(jnp.arange(m, dtype=jnp.int32), 2)
    src_tok = jnp.zeros((n_tiles * te,), jnp.int32).at[jnp.concatenate([pos1, pos2])].set(tok)
    tile_start = jnp.arange(n_tiles, dtype=jnp.int32) * te
    tile_expert = jnp.minimum(jnp.sum(tile_start[:, None] >= pad_end[None, :], axis=1), N_EXPERTS - 1)
    n_active = (pad_end[-1] // te).reshape(1)

    ys_sorted = _experts(xs, src_tok, tile_expert.astype(jnp.int32), n_active.astype(jnp.int32),
                         w_gate, w_up, w_down, te)
    return _moe_out(x, route, ys_sorted, pos1, pos2, g_final, n_prompt, 256)


def _rope_tables(positions):
    half = HEAD_DIM // 2
    inv_freq = ROPE_THETA ** (-jnp.arange(half, dtype=F32) / half)
    ang = positions.astype(F32)[:, None] * inv_freq[None, :]
    cos, sin = jnp.cos(ang), jnp.sin(ang)
    return jnp.concatenate([cos, cos], axis=1), jnp.concatenate([-sin, sin], axis=1)


def kernel(x_prompt, x_sample, cache_swa1, cache_swa2, cache_swa3, state_hgrn, cache_mem_kv, mem_prompt,
           hgrn_lb_logits, norm_mix, w_in, w_proj_attn, w_proj_hgrn, w_out, hgrn_norm, norm_cross, norm_mem,
           w_cq, w_ckv, w_co, norm_ffn, w_rg, b_rg, w_re, b_re, w_e_gate, w_e_up, w_e_down, norm_final):
    bp, seq, d = x_prompt.shape
    bs, dseq, _ = x_sample.shape
    depth = w_in.shape[0]
    assert depth == 1, "single-layer trunk"
    past = cache_swa3.shape[2]
    mem_len = mem_prompt.shape[1]
    np_, ns = bp * seq, bs * dseq
    hw = HGRN_HEADS * HEAD_DIM
    col_hgrn = 3 * ATT_WIDTH
    col_ga = col_hgrn + 4 * hw
    col_gh = col_ga + d
    l = 0

    x_parts = (x_prompt.reshape(np_, d), x_sample.reshape(ns, d))
    pos = jnp.concatenate([jnp.tile(jnp.arange(seq, dtype=jnp.int32), bp),
                           jnp.tile(past + jnp.arange(dseq, dtype=jnp.int32), bs)])
    cos2, sin2 = _rope_tables(pos)

    xn = _rmsnorm2(*x_parts, norm_mix[l], BF16, 512)
    proj = _matmul(xn, w_in[l], F32, 768, ATT_WIDTH, rope=(cos2, sin2), name="proj_in")

    attn = _attn_prompt(proj, bp, seq, 512)
    attn = _attn_sample(proj, (cache_swa1[l], cache_swa2[l], cache_swa3[l]), attn, np_, bs, dseq)

    hgrn, st_p = _hgrn(proj, col_hgrn, hgrn_lb_logits, hgrn_norm[l], None, None, 0, bp, seq, 128, 8)
    hgrn, st_s = _hgrn(proj, col_hgrn, hgrn_lb_logits, hgrn_norm[l], state_hgrn[l], hgrn, np_, bs, dseq, dseq, 8)

    merged = _gated_merge(attn, hgrn, w_proj_attn[l], w_proj_hgrn[l], proj, col_ga, col_gh, 512, 1024, 512)
    x1 = _matmul(merged, w_out[l], F32, 512, 1024, res=x_parts, name="proj_out")

    mem_n = _rmsnorm(mem_prompt.reshape(bp * mem_len, d), norm_mem[l], BF16, 256)
    mkv_p = _matmul(mem_n, w_ckv[l], F32, 256, KV_ROW, name="mem_kv")
    qc = _matmul(x1, w_cq[l], F32, 512, ATT_OUT, norm=norm_cross[l], name="cross_q")
    oc = _cross_attn(qc, mkv_p.reshape(bp, mem_len, KV_ROW), None, 0, bp, seq, 512, 1)
    oc = _cross_attn(qc, cache_mem_kv[l].reshape(bs, mem_len, KV_SLAB, HEAD_DIM), oc, np_, bs, dseq, dseq, 4)
    x2 = _matmul(oc, w_co[l], F32, 512, 1024, res=x1, name="cross_out")

    ff = w_e_gate.shape[-1]
    y_p, y_s = _moe(x2, norm_ffn[l], w_rg[l], b_rg[l], w_re[l], b_re[l],
                    w_e_gate[l].reshape(N_EXPERTS, d, ff), w_e_up[l].reshape(N_EXPERTS, d, ff),
                    w_e_down[l].reshape(N_EXPERTS, ff, d), norm_final, np_)

    swa_p = [_kv_rows(proj, g, 0, bp, seq, min(w, seq), min(w, seq, 256))[None]
             for g, (w, _) in enumerate(ATT_GROUPS)]
    swa_s = [_kv_rows(proj, g, np_, 1, ns, ns, 256).reshape(1, bs, dseq, 2, ATT_HEADS, HEAD_DIM)
             for g in range(len(ATT_GROUPS))]
    return (y_p.reshape(bp, seq, d), y_s.reshape(bs, dseq, d),
            swa_p[0], swa_p[1], swa_p[2], st_p[None],
            mkv_p.reshape(1, bp, mem_len, 2, ATT_HEADS, HEAD_DIM),
            swa_s[0], swa_s[1], swa_s[2], st_s[None])
```
